```python
import math
import jax
import jax.numpy as jnp
from jax import lax
import numpy as np

D_MODEL = 4096
BATCH = 1
SEQ = 16384
DEPTH = 1

MLA_HEADS = 16
MLA_Q_LORA = 1024
MLA_KV_LORA = 512
MLA_NOPE_DIM = 128
MLA_ROPE_DIM = 64
MLA_V_DIM = 128
MLA_Q_BLOCK = 128
ROPE_THETA = 10000.0
MOBA_HEADS = 16
MOBA_HEAD_DIM = 128
MOBA_BLOCK = 256
MOBA_TOPK = 3
MOBA_Q_CHUNK = 32
REL_BUCKETS = 32
REL_MAX_DIST = 128
D_FF = 11008
RMS_EPS = 1e-6

MLA_Q_UP = MLA_HEADS * (MLA_NOPE_DIM + MLA_ROPE_DIM)
MLA_KV_UP = MLA_HEADS * (MLA_NOPE_DIM + MLA_V_DIM)
MLA_OUT = MLA_HEADS * MLA_V_DIM
MOBA_WIDTH = MOBA_HEADS * MOBA_HEAD_DIM
IN_SIZES = (MLA_Q_LORA, MLA_KV_LORA, MLA_ROPE_DIM, MOBA_WIDTH, MOBA_WIDTH, MOBA_WIDTH, D_MODEL, D_MODEL)
IN_COLS = MLA_Q_LORA + MLA_KV_LORA + MLA_ROPE_DIM + 3 * MOBA_WIDTH + 2 * D_MODEL

kernel_name = "hybrid_mla_moba_gated_macaron"


def rms_norm(x, g):
    xf = x.astype(jnp.float32)
    y = xf * lax.rsqrt(jnp.mean(xf * xf, axis=-1, keepdims=True) + RMS_EPS)
    return y.astype(x.dtype) * g


def swiglu(h, w_in, w_out):
    gu = h @ w_in
    g, u = gu[..., :D_FF], gu[..., D_FF:]
    return (jax.nn.silu(g) * u) @ w_out


def split_cols(t, sizes):
    outs = []
    o = 0
    for s in sizes:
        outs.append(t[..., o:o + s])
        o += s
    return outs


def apply_rope(x, cos, sin):
    half = x.shape[-1] // 2
    x1, x2 = x[..., :half], x[..., half:]
    return jnp.concatenate([x1 * cos - x2 * sin, x2 * cos + x1 * sin], axis=-1)


def t5_bucket(n):
    max_exact = REL_BUCKETS // 2
    n_f = jnp.maximum(n, max_exact).astype(jnp.float32)
    large = max_exact + (jnp.log(n_f / max_exact) / math.log(REL_MAX_DIST / max_exact)
                         * (REL_BUCKETS - max_exact)).astype(jnp.int32)
    large = jnp.minimum(large, REL_BUCKETS - 1)
    return jnp.where(n < max_exact, n, large)


def mla_attention(c_q, c_kv, k_rope, cos, sin, norm_cq, w_uq, norm_ckv, w_ukv):
    B, S, _ = c_q.shape
    q = (rms_norm(c_q, norm_cq) @ w_uq).reshape(B, S, MLA_HEADS, MLA_NOPE_DIM + MLA_ROPE_DIM)
    q_nope = q[..., :MLA_NOPE_DIM]
    q_rope = apply_rope(q[..., MLA_NOPE_DIM:], cos[:, :, None, :], sin[:, :, None, :])
    kv = (rms_norm(c_kv, norm_ckv) @ w_ukv).reshape(B, S, MLA_HEADS, MLA_NOPE_DIM + MLA_V_DIM)
    k_nope, v = kv[..., :MLA_NOPE_DIM], kv[..., MLA_NOPE_DIM:]
    k_r = apply_rope(k_rope, cos, sin)
    scale = (MLA_NOPE_DIM + MLA_ROPE_DIM) ** -0.5
    k_idx = jnp.arange(S)

    def block(i):
        s0 = i * MLA_Q_BLOCK
        qn = lax.dynamic_slice_in_dim(q_nope, s0, MLA_Q_BLOCK, axis=1)
        qr = lax.dynamic_slice_in_dim(q_rope, s0, MLA_Q_BLOCK, axis=1)
        logits = (jnp.einsum('bqhd,bkhd->bhqk', qn, k_nope, preferred_element_type=jnp.float32)
                  + jnp.einsum('bqhd,bkd->bhqk', qr, k_r, preferred_element_type=jnp.float32)) * scale
        causal = k_idx[None, :] <= (s0 + jnp.arange(MLA_Q_BLOCK))[:, None]
        logits = jnp.where(causal, logits, -jnp.inf)
        p = jax.nn.softmax(logits, axis=-1).astype(v.dtype)
        return jnp.einsum('bhqk,bkhd->bqhd', p, v)

    out = lax.map(block, jnp.arange(S // MLA_Q_BLOCK))
    return jnp.moveaxis(out, 0, 1).reshape(B, S, MLA_OUT)


def moba_attention_single(q, k, v, pos, rel_bias):
    S, H, Dh = q.shape
    n_blk = -(-S // MOBA_BLOCK)
    s_pad = n_blk * MOBA_BLOCK
    pad = s_pad - S
    qp = jnp.pad(q, ((0, pad), (0, 0), (0, 0)))
    kp = jnp.pad(k, ((0, pad), (0, 0), (0, 0)))
    vp = jnp.pad(v, ((0, pad), (0, 0), (0, 0)))
    posp = jnp.pad(pos, (0, pad))
    kb = kp.reshape(n_blk, MOBA_BLOCK, H, Dh)
    vb = vp.reshape(n_blk, MOBA_BLOCK, H, Dh)
    posb = posp.reshape(n_blk, MOBA_BLOCK)
    k_mean = jnp.mean(kb.astype(jnp.float32), axis=1)
    q_blk = jnp.arange(s_pad) // MOBA_BLOCK
    gate = jnp.einsum('shd,nhd->shn', qp.astype(jnp.float32), k_mean)
    past = jnp.arange(n_blk)[None, None, :] < q_blk[:, None, None]
    gate = jnp.where(past, gate, -jnp.inf)
    n_sel = min(MOBA_TOPK, n_blk)
    _, sel = lax.top_k(gate, n_sel)
    sel_valid = sel < q_blk[:, None, None]
    kb_h = jnp.transpose(kb, (2, 0, 1, 3))
    vb_h = jnp.transpose(vb, (2, 0, 1, 3))
    bias_h = rel_bias.T
    h_idx = jnp.arange(H)
    scale = Dh ** -0.5

    def chunk(c):
        s0 = c * MOBA_Q_CHUNK
        qc = lax.dynamic_slice_in_dim(qp, s0, MOBA_Q_CHUNK, 0)
        sel_c = lax.dynamic_slice_in_dim(sel, s0, MOBA_Q_CHUNK, 0)
        valid_c = lax.dynamic_slice_in_dim(sel_valid, s0, MOBA_Q_CHUNK, 0)
        pos_qc = lax.dynamic_slice_in_dim(posp, s0, MOBA_Q_CHUNK, 0)
        k_sel = kb_h[h_idx[None, :, None], sel_c]
        v_sel = vb_h[h_idx[None, :, None], sel_c]
        pos_sel = posb[sel_c]
        b_sel = t5_bucket(jnp.maximum(pos_qc[:, None, None, None] - pos_sel, 0))
        logit_sel = (jnp.einsum('qhd,qhjkd->qhjk', qc, k_sel, preferred_element_type=jnp.float32) * scale
                     + bias_h[h_idx[None, :, None, None], b_sel])
        logit_sel = jnp.where(valid_c[..., None], logit_sel, -jnp.inf)
        logit_sel = logit_sel.reshape(MOBA_Q_CHUNK, H, n_sel * MOBA_BLOCK)
        ob = s0 // MOBA_BLOCK
        k_own = lax.dynamic_index_in_dim(kb, ob, 0, keepdims=False)
        v_own = lax.dynamic_index_in_dim(vb, ob, 0, keepdims=False)
        pos_own = lax.dynamic_index_in_dim(posb, ob, 0, keepdims=False)
        b_own = t5_bucket(jnp.maximum(pos_qc[:, None] - pos_own[None, :], 0))
        logit_own = (jnp.einsum('qhd,khd->qhk', qc, k_own, preferred_element_type=jnp.float32) * scale
                     + jnp.transpose(rel_bias[b_own], (0, 2, 1)))
        causal = (ob * MOBA_BLOCK + jnp.arange(MOBA_BLOCK))[None, :] <= (s0 + jnp.arange(MOBA_Q_CHUNK))[:, None]
        logit_own = jnp.where(causal[:, None, :], logit_own, -jnp.inf)
        p = jax.nn.softmax(jnp.concatenate([logit_sel, logit_own], axis=-1), axis=-1).astype(v.dtype)
        p_sel = p[..., :n_sel * MOBA_BLOCK].reshape(MOBA_Q_CHUNK, H, n_sel, MOBA_BLOCK)
        p_own = p[..., n_sel * MOBA_BLOCK:]
        return (jnp.einsum('qhjk,qhjkd->qhd', p_sel, v_sel)
                + jnp.einsum('qhk,khd->qhd', p_own, v_own))

    out = lax.map(chunk, jnp.arange(s_pad // MOBA_Q_CHUNK))
    return out.reshape(s_pad, H, Dh)[:S]


def hybrid_layer(x, cos, sin, positions, rel_bias, norm_ffn1, w_ffn1_in, w_ffn1_out, norm_mix, w_in,
                 norm_cq, w_uq, norm_ckv, w_ukv, w_br_a, w_br_b, w_out, norm_ffn2, w_ffn2_in, w_ffn2_out):
    B, S, _ = x.shape
    x = x + 0.5 * swiglu(rms_norm(x, norm_ffn1), w_ffn1_in, w_ffn1_out)
    h = rms_norm(x, norm_mix)
    c_q, c_kv, k_rope, q_b, k_b, v_b, gl_a, gl_b = split_cols(h @ w_in, IN_SIZES)
    o_a = mla_attention(c_q, c_kv, k_rope, cos, sin, norm_cq, w_uq, norm_ckv, w_ukv)
    shp = (B, S, MOBA_HEADS, MOBA_HEAD_DIM)
    o_b = jax.vmap(moba_attention_single, in_axes=(0, 0, 0, 0, None))(
        q_b.reshape(shp), k_b.reshape(shp), v_b.reshape(shp), positions, rel_bias).reshape(B, S, MOBA_WIDTH)
    merged = jax.nn.sigmoid(gl_a) * (o_a @ w_br_a) + jax.nn.sigmoid(gl_b) * (o_b @ w_br_b)
    x = x + merged @ w_out
    x = x + 0.5 * swiglu(rms_norm(x, norm_ffn2), w_ffn2_in, w_ffn2_out)
    return x


def setup_inputs(seed: int = 0) -> dict:
    key = jax.random.key(seed)
    ks = jax.random.split(key, 20)
    f32 = jnp.float32

    def w(k, shape, fan_in):
        return jax.random.normal(k, shape, f32) * (fan_in ** -0.5)

    def gain(k, shape):
        return 1.0 + 0.01 * jax.random.normal(k, shape, f32)

    x = jax.random.normal(ks[0], (BATCH, SEQ, D_MODEL), f32)
    positions = jnp.broadcast_to(jnp.arange(SEQ, dtype=jnp.int32)[None, :], (BATCH, SEQ))
    return {
        "x": x,
        "positions": positions,
        "rel_bias": 0.1 * jax.random.normal(ks[1], (REL_BUCKETS, MOBA_HEADS), f32),
        "norm_ffn1": gain(ks[2], (DEPTH, D_MODEL)),
        "w_ffn1_in": w(ks[3], (DEPTH, D_MODEL, 2 * D_FF), D_MODEL),
        "w_ffn1_out": w(ks[4], (DEPTH, D_FF, D_MODEL), D_FF),
        "norm_mix": gain(ks[5], (DEPTH, D_MODEL)),
        "w_in": w(ks[6], (DEPTH, D_MODEL, IN_COLS), D_MODEL),
        "norm_cq": gain(ks[7], (DEPTH, MLA_Q_LORA)),
        "w_uq": w(ks[8], (DEPTH, MLA_Q_LORA, MLA_Q_UP), MLA_Q_LORA),
        "norm_ckv": gain(ks[9], (DEPTH, MLA_KV_LORA)),
        "w_ukv": w(ks[10], (DEPTH, MLA_KV_LORA, MLA_KV_UP), MLA_KV_LORA),
        "w_br_a": w(ks[11], (DEPTH, MLA_OUT, D_MODEL), MLA_OUT),
        "w_br_b": w(ks[12], (DEPTH, MOBA_WIDTH, D_MODEL), MOBA_WIDTH),
        "w_out": w(ks[13], (DEPTH, D_MODEL, D_MODEL), D_MODEL),
        "norm_ffn2": gain(ks[14], (DEPTH, D_MODEL)),
        "w_ffn2_in": w(ks[15], (DEPTH, D_MODEL, 2 * D_FF), D_MODEL),
        "w_ffn2_out": w(ks[16], (DEPTH, D_FF, D_MODEL), D_FF),
        "norm_final": gain(ks[17], (D_MODEL,)),
    }


def reference(x, positions, rel_bias, norm_ffn1, w_ffn1_in, w_ffn1_out, norm_mix, w_in, norm_cq, w_uq,
              norm_ckv, w_ukv, w_br_a, w_br_b, w_out, norm_ffn2, w_ffn2_in, w_ffn2_out, norm_final):
    inv_freq = ROPE_THETA ** (-jnp.arange(0, MLA_ROPE_DIM, 2, dtype=jnp.float32) / MLA_ROPE_DIM)
    ang = positions[..., None].astype(jnp.float32) * inv_freq
    cos = jnp.cos(ang).astype(x.dtype)
    sin = jnp.sin(ang).astype(x.dtype)
    for l in range(DEPTH):
        x = hybrid_layer(x, cos, sin, positions, rel_bias, norm_ffn1[l], w_ffn1_in[l], w_ffn1_out[l],
                         norm_mix[l], w_in[l], norm_cq[l], w_uq[l], norm_ckv[l], w_ukv[l], w_br_a[l],
                         w_br_b[l], w_out[l], norm_ffn2[l], w_ffn2_in[l], w_ffn2_out[l])
    return rms_norm(x, norm_final)
```

```python
import functools
import math

import jax
import jax.numpy as jnp
from jax import lax
from jax.experimental import pallas as pl
from jax.experimental.pallas import tpu as pltpu

F32 = jnp.float32
BF16 = jnp.bfloat16

RMS_EPS = 1e-6
ROPE_THETA = 10000.0

MLA_HEADS = 16
MLA_Q_LORA = 1024
MLA_KV_LORA = 512
MLA_NOPE = 128
MLA_ROPE = 64
MLA_V = 128
MLA_QK_PAD = 256

MOBA_HEADS = 16
MOBA_DH = 128
MOBA_BLOCK = 256
MOBA_TOPK = 3
REL_BUCKETS = 32
REL_MAX_DIST = 128
BIAS_TABLE = 128

V7X_LANES = 128
V7X_VMEM_LIMIT = 56 * 1024 * 1024

COL_CQ = 0
COL_CKV = 1024
COL_KROPE = 1536
COL_MOBA_Q = 2048
COL_MOBA_K = 4096
COL_MOBA_V = 6144
COL_GATE_A = 8192
COL_GATE_B = 12288
PROJ_COLS = 16384

_NT = (((1,), (1,)), ((), ()))


def _params(sem):
    return pltpu.CompilerParams(dimension_semantics=sem, vmem_limit_bytes=V7X_VMEM_LIMIT)


def _row_resident_spec(tm, d):
    return pl.BlockSpec((tm, d), lambda i, j: (i, 0), pipeline_mode=pl.Buffered(1))


def _rms(xf, gain):
    ms = jnp.mean(xf * xf, axis=-1, keepdims=True)
    return xf * lax.rsqrt(ms + RMS_EPS) * gain


def _sigmoid(z):
    return 1.0 / (1.0 + jnp.exp(-z))


def _ffn_body(x_ref, g_ref, wg_ref, wu_ref, wo_ref, *rest, final_norm):
    if final_norm:
        gf_ref, o_ref, h_ref = rest
    else:
        o_ref, h_ref = rest
    j = pl.program_id(1)

    @pl.when(j == 0)
    def _():
        xf = x_ref[...]
        h_ref[...] = _rms(xf, g_ref[...]).astype(BF16)
        o_ref[...] = xf

    h = h_ref[...]
    g = jnp.dot(h, wg_ref[...], preferred_element_type=F32)
    u = jnp.dot(h, wu_ref[...], preferred_element_type=F32)
    a = (g * _sigmoid(g) * (0.5 * u)).astype(BF16)
    o_ref[...] += jnp.dot(a, wo_ref[...], preferred_element_type=F32)

    if final_norm:
        @pl.when(j == pl.num_programs(1) - 1)
        def _():
            o_ref[...] = _rms(o_ref[...], gf_ref[...])


def _ffn(x, gain, w_in, w_out, final_gain=None, *, tm=512, tf=256):
    s, d = x.shape
    f = w_out.shape[0]
    nf = f // tf
    assert s % tm == 0 and f % tf == 0 and w_in.shape == (d, 2 * f)
    in_specs = [
        _row_resident_spec(tm, d),
        pl.BlockSpec((1, d), lambda i, j: (0, 0)),
        pl.BlockSpec((d, tf), lambda i, j: (0, j)),
        pl.BlockSpec((d, tf), lambda i, j: (0, j + nf)),
        pl.BlockSpec((tf, d), lambda i, j: (j, 0)),
    ]
    args = [x, gain.reshape(1, d), w_in, w_in, w_out]
    if final_gain is not None:
        in_specs.append(pl.BlockSpec((1, d), lambda i, j: (0, 0)))
        args.append(final_gain.reshape(1, d))
    return pl.pallas_call(
        functools.partial(_ffn_body, final_norm=final_gain is not None),
        grid=(s // tm, nf),
        in_specs=in_specs,
        out_specs=pl.BlockSpec((tm, d), lambda i, j: (i, 0)),
        out_shape=jax.ShapeDtypeStruct((s, d), F32),
        scratch_shapes=[pltpu.VMEM((tm, d), BF16)],
        compiler_params=_params(("arbitrary", "arbitrary")),
        name="ffn",
    )(*args)


def _proj_body(x_ref, g_ref, w_ref, o_ref, h_ref):
    @pl.when(pl.program_id(1) == 0)
    def _():
        h_ref[...] = _rms(x_ref[...], g_ref[...]).astype(BF16)

    o_ref[...] = jnp.dot(h_ref[...], w_ref[...], preferred_element_type=F32).astype(o_ref.dtype)


def _proj(x, gain, w, *, tm=512, tn=1024):
    s, d = x.shape
    n = w.shape[1]
    assert s % tm == 0 and n % tn == 0
    return pl.pallas_call(
        _proj_body,
        grid=(s // tm, n // tn),
        in_specs=[
            _row_resident_spec(tm, d),
            pl.BlockSpec((1, d), lambda i, j: (0, 0)),
            pl.BlockSpec((d, tn), lambda i, j: (0, j)),
        ],
        out_specs=pl.BlockSpec((tm, tn), lambda i, j: (i, j)),
        out_shape=jax.ShapeDtypeStruct((s, n), BF16),
        scratch_shapes=[pltpu.VMEM((tm, d), BF16)],
        compiler_params=_params(("arbitrary", "arbitrary")),
        name="proj",
    )(x, gain.reshape(1, d), w)


def _rope_lanes(r, cf, s1, s2):
    return r * cf + pltpu.roll(r, 96, 1) * s1 + pltpu.roll(r, 32, 1) * s2


def _rope_tables(positions):
    half = MLA_ROPE // 2
    inv_freq = ROPE_THETA ** (-jnp.arange(0, MLA_ROPE, 2, dtype=F32) / MLA_ROPE)
    ang = positions[:, None].astype(F32) * inv_freq
    cos, sin = jnp.cos(ang), jnp.sin(ang)
    z = jnp.zeros_like(cos)
    pad = jnp.zeros((positions.shape[0], V7X_LANES - MLA_ROPE), F32)
    cf = jnp.concatenate([cos, cos, pad], axis=1)
    s1 = jnp.concatenate([-sin, z, pad], axis=1)
    s2 = jnp.concatenate([z, sin, pad], axis=1)
    return cf, s1, s2


def _qup_body(c_ref, g_ref, w_ref, cf_ref, s1_ref, s2_ref, q_ref, h_ref, *, scale):
    @pl.when(pl.program_id(1) == 0)
    def _():
        h_ref[...] = _rms(c_ref[...].astype(F32), g_ref[...]).astype(BF16)

    y = jnp.dot(h_ref[...], w_ref[...], preferred_element_type=F32)
    q_ref[0, :, :MLA_NOPE] = (y[:, :MLA_NOPE] * scale).astype(q_ref.dtype)
    r = _rope_lanes(y[:, MLA_NOPE:], cf_ref[...], s1_ref[...], s2_ref[...])
    q_ref[0, :, MLA_NOPE:] = (r * scale).astype(q_ref.dtype)


def _q_up(proj, gain, w_q, tables, *, tm=1024):
    s = proj.shape[0]
    cf, s1, s2 = tables
    scale = (MLA_NOPE + MLA_ROPE) ** -0.5
    tab_spec = pl.BlockSpec((tm, V7X_LANES), lambda i, h: (i, 0))
    return pl.pallas_call(
        functools.partial(_qup_body, scale=scale),
        grid=(s // tm, MLA_HEADS),
        in_specs=[
            pl.BlockSpec((tm, MLA_Q_LORA), lambda i, h: (i, COL_CQ // MLA_Q_LORA)),
            pl.BlockSpec((1, MLA_Q_LORA), lambda i, h: (0, 0)),
            pl.BlockSpec((MLA_Q_LORA, MLA_QK_PAD), lambda i, h: (0, h)),
            tab_spec, tab_spec, tab_spec,
        ],
        out_specs=pl.BlockSpec((1, tm, MLA_QK_PAD), lambda i, h: (h, i, 0)),
        out_shape=jax.ShapeDtypeStruct((MLA_HEADS, s, MLA_QK_PAD), BF16),
        scratch_shapes=[pltpu.VMEM((tm, MLA_Q_LORA), BF16)],
        compiler_params=_params(("arbitrary", "arbitrary")),
        name="q_up",
    )(proj, gain.reshape(1, MLA_Q_LORA), w_q, cf, s1, s2)


def _kvup_body(c_ref, kr_ref, g_ref, w_ref, cf_ref, s1_ref, s2_ref, k_ref, v_ref, h_ref, r_ref):
    @pl.when(pl.program_id(1) == 0)
    def _():
        h_ref[...] = _rms(c_ref[...].astype(F32), g_ref[...]).astype(BF16)
        r_ref[...] = _rope_lanes(kr_ref[...].astype(F32), cf_ref[...], s1_ref[...], s2_ref[...]).astype(BF16)

    y = jnp.dot(h_ref[...], w_ref[...], preferred_element_type=F32)
    k_ref[0, :, :MLA_NOPE] = y[:, :MLA_NOPE].astype(k_ref.dtype)
    k_ref[0, :, MLA_NOPE:] = r_ref[...]
    v_ref[0] = y[:, MLA_NOPE:].astype(v_ref.dtype)


def _kv_up(proj, gain, w_ukv, tables, *, tm=1024):
    s = proj.shape[0]
    cf, s1, s2 = tables
    tab_spec = pl.BlockSpec((tm, V7X_LANES), lambda i, h: (i, 0))
    return pl.pallas_call(
        _kvup_body,
        grid=(s // tm, MLA_HEADS),
        in_specs=[
            pl.BlockSpec((tm, MLA_KV_LORA), lambda i, h: (i, COL_CKV // MLA_KV_LORA)),
            pl.BlockSpec((tm, V7X_LANES), lambda i, h: (i, COL_KROPE // V7X_LANES)),
            pl.BlockSpec((1, MLA_KV_LORA), lambda i, h: (0, 0)),
            pl.BlockSpec((MLA_KV_LORA, MLA_NOPE + MLA_V), lambda i, h: (0, h)),
            tab_spec, tab_spec, tab_spec,
        ],
        out_specs=[
            pl.BlockSpec((1, tm, MLA_QK_PAD), lambda i, h: (h, i, 0)),
            pl.BlockSpec((1, tm, MLA_V), lambda i, h: (h, i, 0)),
        ],
        out_shape=[
            jax.ShapeDtypeStruct((MLA_HEADS, s, MLA_QK_PAD), BF16),
            jax.ShapeDtypeStruct((MLA_HEADS, s, MLA_V), BF16),
        ],
        scratch_shapes=[pltpu.VMEM((tm, MLA_KV_LORA), BF16), pltpu.VMEM((tm, V7X_LANES), BF16)],
        compiler_params=_params(("arbitrary", "arbitrary")),
        name="kv_up",
    )(proj, proj, gain.reshape(1, MLA_KV_LORA), w_ukv, cf, s1, s2)


def _softmax_first(s, v):
    m = jnp.max(s, axis=1, keepdims=True)
    p = jnp.exp(s - m)
    l = jnp.sum(p, axis=1, keepdims=True)
    acc = jnp.dot(p.astype(v.dtype), v, preferred_element_type=F32)
    return m, l, acc


def _softmax_step(s, v, m, l, acc):
    m_new = jnp.maximum(m, jnp.max(s, axis=1, keepdims=True))
    alpha = jnp.exp(m - m_new)
    p = jnp.exp(s - m_new)
    l = alpha * l + jnp.sum(p, axis=1, keepdims=True)
    acc = alpha * acc + jnp.dot(p.astype(v.dtype), v, preferred_element_type=F32)
    return m_new, l, acc


def _mla_body(q_ref, k_ref, v_ref, o_ref, *, t):
    i = pl.program_id(1)
    q = q_ref[0]

    def tile(n):
        off = pl.multiple_of(n * t, t)
        s = lax.dot_general(q, k_ref[0, pl.ds(off, t), :], _NT, preferred_element_type=F32)
        return s, v_ref[0, pl.ds(off, t), :]

    s, v = tile(i)
    row = lax.broadcasted_iota(jnp.int32, (t, t), 0)
    col = lax.broadcasted_iota(jnp.int32, (t, t), 1)
    carry = _softmax_first(jnp.where(col <= row, s, -jnp.inf), v)

    def body(n, c):
        s, v = tile(n)
        return _softmax_step(s, v, *c)

    m, l, acc = lax.fori_loop(0, i, body, carry)
    o_ref[...] = (acc / l).astype(o_ref.dtype)


def _mla(q, k, v, *, t=512):
    h, s, _ = q.shape
    return pl.pallas_call(
        functools.partial(_mla_body, t=t),
        grid=(h, s // t),
        in_specs=[
            pl.BlockSpec((1, t, MLA_QK_PAD), lambda h, i: (h, i, 0)),
            pl.BlockSpec((1, s, MLA_QK_PAD), lambda h, i: (h, 0, 0)),
            pl.BlockSpec((1, s, MLA_V), lambda h, i: (h, 0, 0)),
        ],
        out_specs=pl.BlockSpec((t, MLA_V), lambda h, i: (i, h)),
        out_shape=jax.ShapeDtypeStruct((s, h * MLA_V), BF16),
        compiler_params=_params(("arbitrary", "arbitrary")),
        name="mla",
    )(q, k, v)


def _t5_bucket(n):
    max_exact = REL_BUCKETS // 2
    n_f = jnp.maximum(n, max_exact).astype(F32)
    large = max_exact + (jnp.log(n_f / max_exact) / math.log(REL_MAX_DIST / max_exact)
                         * (REL_BUCKETS - max_exact)).astype(jnp.int32)
    large = jnp.minimum(large, REL_BUCKETS - 1)
    return jnp.where(n < max_exact, n, large)


def _moba_body(pmin_ref, pmax_ref, q_ref, k_ref, v_ref, pq_ref, pk_ref, tbl_ref, o_ref, kmean_ref, s_ref,
               *, nblk, scale):
    i = pl.program_id(1)
    blk = MOBA_BLOCK

    @pl.when(i == 0)
    def _():
        kmean_ref[...] = jnp.zeros_like(kmean_ref)

        def fill(n, c):
            kb = k_ref[pl.ds(pl.multiple_of(n * blk, blk), blk), :].astype(F32)
            kmean_ref[pl.ds(n, 1), :] = jnp.sum(kb, axis=0, keepdims=True) * (1.0 / blk)
            return c

        lax.fori_loop(0, nblk, fill, 0)

    q = q_ref[...]

    gate = lax.dot_general(q.astype(F32), kmean_ref[...], _NT, preferred_element_type=F32)
    lane = lax.broadcasted_iota(jnp.int32, gate.shape, 1)
    lane_f = lane.astype(F32)
    g = jnp.where(lane < i, gate, -jnp.inf)
    sel = []
    for _ in range(min(MOBA_TOPK, nblk)):
        mx = jnp.max(g, axis=1, keepdims=True)
        idx = jnp.min(jnp.where(g == mx, lane_f, float(V7X_LANES)), axis=1, keepdims=True)
        sel.append(jnp.where(mx > -jnp.inf, idx, -1.0))
        g = jnp.where(lane_f == idx, -jnp.inf, g)

    def tile(n):
        off = pl.multiple_of(n * blk, blk)
        s_ref[...] = lax.dot_general(q, k_ref[pl.ds(off, blk), :], _NT, preferred_element_type=F32) * scale
        far = (pmin_ref[i] - pmax_ref[n]) >= BIAS_TABLE - 1

        @pl.when(far)
        def _():
            s_ref[...] += tbl_ref[0, :, BIAS_TABLE - 1:BIAS_TABLE]

        @pl.when(jnp.logical_not(far))
        def _():
            d = jnp.clip(pq_ref[...] - pk_ref[:, pl.ds(off, blk)], 0, BIAS_TABLE - 1)
            tb = jnp.broadcast_to(tbl_ref[0], (blk, BIAS_TABLE))
            for c in range(blk // V7X_LANES):
                cs = slice(c * V7X_LANES, (c + 1) * V7X_LANES)
                s_ref[:, cs] += jnp.take_along_axis(tb, d[:, cs], axis=1)

        return s_ref[...], v_ref[pl.ds(off, blk), :]

    s, v = tile(i)
    row = lax.broadcasted_iota(jnp.int32, (blk, blk), 0)
    col = lax.broadcasted_iota(jnp.int32, (blk, blk), 1)
    carry = _softmax_first(jnp.where(col <= row, s, -jnp.inf), v)

    def body(n, c):
        s, v = tile(n)
        n_f = n.astype(F32)
        picked = sel[0] == n_f
        for t in sel[1:]:
            picked = jnp.logical_or(picked, t == n_f)
        return _softmax_step(jnp.where(picked, s, -jnp.inf), v, *c)

    m, l, acc = lax.fori_loop(0, i, body, carry)
    o_ref[...] = (acc / l).astype(o_ref.dtype)


def _moba(proj, positions, rel_bias):
    s = proj.shape[0]
    blk = MOBA_BLOCK
    nblk = s // blk
    assert s % blk == 0 and nblk <= V7X_LANES
    pos_blocks = positions.reshape(nblk, blk)
    pmin = jnp.min(pos_blocks, axis=1)
    pmax = jnp.max(pos_blocks, axis=1)
    table = rel_bias[_t5_bucket(jnp.arange(BIAS_TABLE, dtype=jnp.int32))]
    table = jnp.transpose(table).reshape(MOBA_HEADS, 1, BIAS_TABLE).astype(F32)
    qc, kc, vc = (c // MOBA_DH for c in (COL_MOBA_Q, COL_MOBA_K, COL_MOBA_V))
    grid_spec = pltpu.PrefetchScalarGridSpec(
        num_scalar_prefetch=2,
        grid=(MOBA_HEADS, nblk),
        in_specs=[
            pl.BlockSpec((blk, MOBA_DH), lambda h, i, *_: (i, qc + h)),
            pl.BlockSpec((s, MOBA_DH), lambda h, i, *_: (0, kc + h)),
            pl.BlockSpec((s, MOBA_DH), lambda h, i, *_: (0, vc + h)),
            pl.BlockSpec((blk, 1), lambda h, i, *_: (i, 0)),
            pl.BlockSpec((1, s), lambda h, i, *_: (0, 0)),
            pl.BlockSpec((1, 1, BIAS_TABLE), lambda h, i, *_: (h, 0, 0)),
        ],
        out_specs=pl.BlockSpec((blk, MOBA_DH), lambda h, i, *_: (i, h)),
        scratch_shapes=[pltpu.VMEM((V7X_LANES, MOBA_DH), F32), pltpu.VMEM((blk, blk), F32)],
    )
    return pl.pallas_call(
        functools.partial(_moba_body, nblk=nblk, scale=MOBA_DH ** -0.5),
        grid_spec=grid_spec,
        out_shape=jax.ShapeDtypeStruct((s, MOBA_HEADS * MOBA_DH), BF16),
        compiler_params=_params(("arbitrary", "arbitrary")),
        name="moba",
    )(pmin, pmax, proj, proj, proj, positions.reshape(s, 1), positions.reshape(1, s), table)


def _merge_body(oa_ref, ob_ref, ga_ref, gb_ref, wa_ref, wb_ref, wo_ref, x_ref, o_ref):
    @pl.when(pl.program_id(1) == 0)
    def _():
        o_ref[...] = x_ref[...]

    ma = jnp.dot(oa_ref[...], wa_ref[...], preferred_element_type=F32)
    mb = jnp.dot(ob_ref[...], wb_ref[...], preferred_element_type=F32)
    mg = _sigmoid(ga_ref[...].astype(F32)) * ma + _sigmoid(gb_ref[...].astype(F32)) * mb
    o_ref[...] += jnp.dot(mg.astype(BF16), wo_ref[...], preferred_element_type=F32)


def _merge(o_a, o_b, proj, w_a, w_b, w_out, x, *, tm=512, tn=256):
    s, d = x.shape
    ka, kb = o_a.shape[1], o_b.shape[1]
    ga, gb = COL_GATE_A // tn, COL_GATE_B // tn
    return pl.pallas_call(
        _merge_body,
        grid=(s // tm, d // tn),
        in_specs=[
            pl.BlockSpec((tm, ka), lambda i, j: (i, 0)),
            pl.BlockSpec((tm, kb), lambda i, j: (i, 0)),
            pl.BlockSpec((tm, tn), lambda i, j: (i, ga + j)),
            pl.BlockSpec((tm, tn), lambda i, j: (i, gb + j)),
            pl.BlockSpec((ka, tn), lambda i, j: (0, j)),
            pl.BlockSpec((kb, tn), lambda i, j: (0, j)),
            pl.BlockSpec((tn, d), lambda i, j: (j, 0)),
            _row_resident_spec(tm, d),
        ],
        out_specs=pl.BlockSpec((tm, d), lambda i, j: (i, 0)),
        out_shape=jax.ShapeDtypeStruct((s, d), F32),
        compiler_params=_params(("arbitrary", "arbitrary")),
        name="merge",
    )(o_a, o_b, proj, proj, w_a, w_b, w_out, x)


def _proj_layout(w_in):
    d = w_in.shape[0]
    lat = MLA_Q_LORA + MLA_KV_LORA + MLA_ROPE
    pad = jnp.zeros((d, COL_MOBA_Q - lat), w_in.dtype)
    return jnp.concatenate([w_in[:, :lat], pad, w_in[:, lat:]], axis=1).astype(BF16)


def _q_layout(w_uq):
    r = w_uq.shape[0]
    w = w_uq.reshape(r, MLA_HEADS, MLA_NOPE + MLA_ROPE)
    w = jnp.pad(w, ((0, 0), (0, 0), (0, MLA_QK_PAD - MLA_NOPE - MLA_ROPE)))
    return w.reshape(r, MLA_HEADS * MLA_QK_PAD).astype(BF16)


def kernel(x, positions, rel_bias, norm_ffn1, w_ffn1_in, w_ffn1_out, norm_mix, w_in, norm_cq, w_uq, norm_ckv, w_ukv, w_br_a, w_br_b, w_out, norm_ffn2, w_ffn2_in, w_ffn2_out, norm_final):
    b, s, d = x.shape
    assert b == 1 and norm_ffn1.shape[0] == 1
    pos = positions[0]
    x0 = x[0]

    x1 = _ffn(x0, norm_ffn1[0], w_ffn1_in[0].astype(BF16), w_ffn1_out[0].astype(BF16))
    proj = _proj(x1, norm_mix[0], _proj_layout(w_in[0]))
    tables = _rope_tables(pos)
    q = _q_up(proj, norm_cq[0], _q_layout(w_uq[0]), tables)
    k, v = _kv_up(proj, norm_ckv[0], w_ukv[0].astype(BF16), tables)
    o_a = _mla(q, k, v)
    o_b = _moba(proj, pos, rel_bias)
    x2 = _merge(o_a, o_b, proj, w_br_a[0].astype(BF16), w_br_b[0].astype(BF16), w_out[0].astype(BF16), x1)
    out = _ffn(x2, norm_ffn2[0], w_ffn2_in[0].astype(BF16), w_ffn2_out[0].astype(BF16), norm_final)
    return out[None]
```

```python
import functools
import math

import jax
import jax.numpy as jnp
from jax import lax
from jax.experimental import pallas as pl
from jax.experimental.pallas import tpu as pltpu

F32 = jnp.float32
BF16 = jnp.bfloat16

RMS_EPS = 1e-6
ROPE_THETA = 10000.0

MLA_HEADS = 16
MLA_Q_LORA = 1024
MLA_KV_LORA = 512
MLA_NOPE = 128
MLA_ROPE = 64
MLA_V = 128
MLA_QK_PAD = 256

MOBA_HEADS = 16
MOBA_DH = 128
MOBA_BLOCK = 256
MOBA_TOPK = 3
REL_BUCKETS = 32
REL_MAX_DIST = 128
BIAS_TABLE = 128

V7X_LANES = 128
V7X_VMEM_LIMIT = 56 * 1024 * 1024

COL_CQ = 0
COL_CKV = 1024
COL_KROPE = 1536
COL_MOBA_Q = 2048
COL_MOBA_K = 4096
COL_MOBA_V = 6144
COL_GATE_A = 8192
COL_GATE_B = 12288
PROJ_COLS = 16384

_NT = (((1,), (1,)), ((), ()))


def _params(sem):
    return pltpu.CompilerParams(dimension_semantics=sem, vmem_limit_bytes=V7X_VMEM_LIMIT)


def _row_resident_spec(tm, d):
    return pl.BlockSpec((tm, d), lambda i, j: (i, 0), pipeline_mode=pl.Buffered(1))


def _rms(xf, gain):
    ms = jnp.mean(xf * xf, axis=-1, keepdims=True)
    return xf * lax.rsqrt(ms + RMS_EPS) * gain


def _sigmoid(z):
    return 1.0 / (1.0 + jnp.exp(-z))


def _ffn_body(x_ref, g_ref, wg_ref, wu_ref, wo_ref, *rest, final_norm):
    if final_norm:
        gf_ref, o_ref, h_ref = rest
    else:
        o_ref, h_ref = rest
    j = pl.program_id(1)

    @pl.when(j == 0)
    def _():
        xf = x_ref[...]
        h_ref[...] = _rms(xf, g_ref[...]).astype(BF16)
        o_ref[...] = xf

    h = h_ref[...]
    g = jnp.dot(h, wg_ref[...], preferred_element_type=F32)
    u = jnp.dot(h, wu_ref[...], preferred_element_type=F32)
    a = (g * _sigmoid(g) * (0.5 * u)).astype(BF16)
    o_ref[...] += jnp.dot(a, wo_ref[...], preferred_element_type=F32)

    if final_norm:
        @pl.when(j == pl.num_programs(1) - 1)
        def _():
            o_ref[...] = _rms(o_ref[...], gf_ref[...])


def _ffn(x, gain, w_in, w_out, final_gain=None, *, tm=512, tf=256):
    s, d = x.shape
    f = w_out.shape[0]
    nf = f // tf
    assert s % tm == 0 and f % tf == 0 and w_in.shape == (d, 2 * f)
    in_specs = [
        _row_resident_spec(tm, d),
        pl.BlockSpec((1, d), lambda i, j: (0, 0)),
        pl.BlockSpec((d, tf), lambda i, j: (0, j)),
        pl.BlockSpec((d, tf), lambda i, j: (0, j + nf)),
        pl.BlockSpec((tf, d), lambda i, j: (j, 0)),
    ]
    args = [x, gain.reshape(1, d), w_in, w_in, w_out]
    if final_gain is not None:
        in_specs.append(pl.BlockSpec((1, d), lambda i, j: (0, 0)))
        args.append(final_gain.reshape(1, d))
    return pl.pallas_call(
        functools.partial(_ffn_body, final_norm=final_gain is not None),
        grid=(s // tm, nf),
        in_specs=in_specs,
        out_specs=pl.BlockSpec((tm, d), lambda i, j: (i, 0)),
        out_shape=jax.ShapeDtypeStruct((s, d), F32),
        scratch_shapes=[pltpu.VMEM((tm, d), BF16)],
        compiler_params=_params(("arbitrary", "arbitrary")),
        name="ffn",
    )(*args)


def _proj_body(x_ref, g_ref, w_ref, o_ref, h_ref):
    @pl.when(pl.program_id(1) == 0)
    def _():
        h_ref[...] = _rms(x_ref[...], g_ref[...]).astype(BF16)

    o_ref[...] = jnp.dot(h_ref[...], w_ref[...], preferred_element_type=F32).astype(o_ref.dtype)


def _proj(x, gain, w, *, tm=512, tn=1024):
    s, d = x.shape
    n = w.shape[1]
    assert s % tm == 0 and n % tn == 0
    return pl.pallas_call(
        _proj_body,
        grid=(s // tm, n // tn),
        in_specs=[
            _row_resident_spec(tm, d),
            pl.BlockSpec((1, d), lambda i, j: (0, 0)),
            pl.BlockSpec((d, tn), lambda i, j: (0, j)),
        ],
        out_specs=pl.BlockSpec((tm, tn), lambda i, j: (i, j)),
        out_shape=jax.ShapeDtypeStruct((s, n), BF16),
        scratch_shapes=[pltpu.VMEM((tm, d), BF16)],
        compiler_params=_params(("arbitrary", "arbitrary")),
        name="proj",
    )(x, gain.reshape(1, d), w)


def _rope_lanes(r, cf, s1, s2):
    return r * cf + pltpu.roll(r, 96, 1) * s1 + pltpu.roll(r, 32, 1) * s2


def _rope_tables(positions):
    half = MLA_ROPE // 2
    inv_freq = ROPE_THETA ** (-jnp.arange(0, MLA_ROPE, 2, dtype=F32) / MLA_ROPE)
    ang = positions[:, None].astype(F32) * inv_freq
    cos, sin = jnp.cos(ang), jnp.sin(ang)
    z = jnp.zeros_like(cos)
    pad = jnp.zeros((positions.shape[0], V7X_LANES - MLA_ROPE), F32)
    cf = jnp.concatenate([cos, cos, pad], axis=1)
    s1 = jnp.concatenate([-sin, z, pad], axis=1)
    s2 = jnp.concatenate([z, sin, pad], axis=1)
    return cf, s1, s2


def _qup_body(c_ref, g_ref, w_ref, cf_ref, s1_ref, s2_ref, q_ref, h_ref, *, scale):
    @pl.when(pl.program_id(1) == 0)
    def _():
        h_ref[...] = _rms(c_ref[...].astype(F32), g_ref[...]).astype(BF16)

    y = jnp.dot(h_ref[...], w_ref[...], preferred_element_type=F32)
    q_ref[0, :, :MLA_NOPE] = (y[:, :MLA_NOPE] * scale).astype(q_ref.dtype)
    r = _rope_lanes(y[:, MLA_NOPE:], cf_ref[...], s1_ref[...], s2_ref[...])
    q_ref[0, :, MLA_NOPE:] = (r * scale).astype(q_ref.dtype)


def _q_up(proj, gain, w_q, tables, *, tm=1024):
    s = proj.shape[0]
    cf, s1, s2 = tables
    scale = (MLA_NOPE + MLA_ROPE) ** -0.5
    tab_spec = pl.BlockSpec((tm, V7X_LANES), lambda i, h: (i, 0))
    return pl.pallas_call(
        functools.partial(_qup_body, scale=scale),
        grid=(s // tm, MLA_HEADS),
        in_specs=[
            pl.BlockSpec((tm, MLA_Q_LORA), lambda i, h: (i, COL_CQ // MLA_Q_LORA)),
            pl.BlockSpec((1, MLA_Q_LORA), lambda i, h: (0, 0)),
            pl.BlockSpec((MLA_Q_LORA, MLA_QK_PAD), lambda i, h: (0, h)),
            tab_spec, tab_spec, tab_spec,
        ],
        out_specs=pl.BlockSpec((1, tm, MLA_QK_PAD), lambda i, h: (h, i, 0)),
        out_shape=jax.ShapeDtypeStruct((MLA_HEADS, s, MLA_QK_PAD), BF16),
        scratch_shapes=[pltpu.VMEM((tm, MLA_Q_LORA), BF16)],
        compiler_params=_params(("arbitrary", "arbitrary")),
        name="q_up",
    )(proj, gain.reshape(1, MLA_Q_LORA), w_q, cf, s1, s2)


def _kvup_body(c_ref, kr_ref, g_ref, w_ref, cf_ref, s1_ref, s2_ref, k_ref, v_ref, h_ref, r_ref):
    @pl.when(pl.program_id(1) == 0)
    def _():
        h_ref[...] = _rms(c_ref[...].astype(F32), g_ref[...]).astype(BF16)
        r_ref[...] = _rope_lanes(kr_ref[...].astype(F32), cf_ref[...], s1_ref[...], s2_ref[...]).astype(BF16)

    y = jnp.dot(h_ref[...], w_ref[...], preferred_element_type=F32)
    k_ref[0, :, :MLA_NOPE] = y[:, :MLA_NOPE].astype(k_ref.dtype)
    k_ref[0, :, MLA_NOPE:] = r_ref[...]
    v_ref[0] = y[:, MLA_NOPE:].astype(v_ref.dtype)


def _kv_up(proj, gain, w_ukv, tables, *, tm=1024):
    s = proj.shape[0]
    cf, s1, s2 = tables
    tab_spec = pl.BlockSpec((tm, V7X_LANES), lambda i, h: (i, 0))
    return pl.pallas_call(
        _kvup_body,
        grid=(s // tm, MLA_HEADS),
        in_specs=[
            pl.BlockSpec((tm, MLA_KV_LORA), lambda i, h: (i, COL_CKV // MLA_KV_LORA)),
            pl.BlockSpec((tm, V7X_LANES), lambda i, h: (i, COL_KROPE // V7X_LANES)),
            pl.BlockSpec((1, MLA_KV_LORA), lambda i, h: (0, 0)),
            pl.BlockSpec((MLA_KV_LORA, MLA_NOPE + MLA_V), lambda i, h: (0, h)),
            tab_spec, tab_spec, tab_spec,
        ],
        out_specs=[
            pl.BlockSpec((1, tm, MLA_QK_PAD), lambda i, h: (h, i, 0)),
            pl.BlockSpec((1, tm, MLA_V), lambda i, h: (h, i, 0)),
        ],
        out_shape=[
            jax.ShapeDtypeStruct((MLA_HEADS, s, MLA_QK_PAD), BF16),
            jax.ShapeDtypeStruct((MLA_HEADS, s, MLA_V), BF16),
        ],
        scratch_shapes=[pltpu.VMEM((tm, MLA_KV_LORA), BF16), pltpu.VMEM((tm, V7X_LANES), BF16)],
        compiler_params=_params(("arbitrary", "arbitrary")),
        name="kv_up",
    )(proj, proj, gain.reshape(1, MLA_KV_LORA), w_ukv, cf, s1, s2)


def _softmax_first(s, v):
    m = jnp.max(s, axis=1, keepdims=True)
    p = jnp.exp(s - m)
    l = jnp.sum(p, axis=1, keepdims=True)
    acc = jnp.dot(p.astype(v.dtype), v, preferred_element_type=F32)
    return m, l, acc


def _softmax_step(s, v, m, l, acc):
    m_new = jnp.maximum(m, jnp.max(s, axis=1, keepdims=True))
    alpha = jnp.exp(m - m_new)
    p = jnp.exp(s - m_new)
    l = alpha * l + jnp.sum(p, axis=1, keepdims=True)
    acc = alpha * acc + jnp.dot(p.astype(v.dtype), v, preferred_element_type=F32)
    return m_new, l, acc


def _mla_body(q_ref, k_ref, v_ref, o_ref, *, t):
    i = pl.program_id(1)
    q = q_ref[0]

    def tile(n):
        off = pl.multiple_of(n * t, t)
        s = lax.dot_general(q, k_ref[0, pl.ds(off, t), :], _NT, preferred_element_type=F32)
        return s, v_ref[0, pl.ds(off, t), :]

    s, v = tile(i)
    row = lax.broadcasted_iota(jnp.int32, (t, t), 0)
    col = lax.broadcasted_iota(jnp.int32, (t, t), 1)
    carry = _softmax_first(jnp.where(col <= row, s, -jnp.inf), v)

    def body(n, c):
        s, v = tile(n)
        return _softmax_step(s, v, *c)

    m, l, acc = lax.fori_loop(0, i, body, carry)
    o_ref[...] = (acc / l).astype(o_ref.dtype)


def _mla(q, k, v, *, t=1024):
    h, s, _ = q.shape
    return pl.pallas_call(
        functools.partial(_mla_body, t=t),
        grid=(h, s // t),
        in_specs=[
            pl.BlockSpec((1, t, MLA_QK_PAD), lambda h, i: (h, i, 0)),
            pl.BlockSpec((1, s, MLA_QK_PAD), lambda h, i: (h, 0, 0)),
            pl.BlockSpec((1, s, MLA_V), lambda h, i: (h, 0, 0)),
        ],
        out_specs=pl.BlockSpec((t, MLA_V), lambda h, i: (i, h)),
        out_shape=jax.ShapeDtypeStruct((s, h * MLA_V), BF16),
        compiler_params=_params(("arbitrary", "arbitrary")),
        name="mla",
    )(q, k, v)


def _t5_bucket(n):
    max_exact = REL_BUCKETS // 2
    n_f = jnp.maximum(n, max_exact).astype(F32)
    large = max_exact + (jnp.log(n_f / max_exact) / math.log(REL_MAX_DIST / max_exact)
                         * (REL_BUCKETS - max_exact)).astype(jnp.int32)
    large = jnp.minimum(large, REL_BUCKETS - 1)
    return jnp.where(n < max_exact, n, large)


def _moba_body(pmin_ref, pmax_ref, q_ref, k_ref, v_ref, pq_ref, pk_ref, tbl_ref, o_ref,
               kmean_ref, vt_ref, m_ref, l_ref, acc_ref, *, nblk, scale, group):
    i = pl.program_id(1)
    blk = MOBA_BLOCK

    @pl.when(i == 0)
    def _():
        kmean_ref[...] = jnp.zeros_like(kmean_ref)

        def fill(n, c):
            rows = pl.ds(pl.multiple_of(n * blk, blk), blk)
            kmean_ref[pl.ds(n, 1), :] = jnp.sum(k_ref[rows, :].astype(F32), axis=0, keepdims=True) * (1.0 / blk)
            vt_ref[:, rows] = v_ref[rows, :].astype(F32).T.astype(vt_ref.dtype)
            return c

        lax.fori_loop(0, nblk, fill, 0)

    q = q_ref[...]
    qs = (q.astype(F32) * scale).astype(q.dtype)

    gate = lax.dot_general(kmean_ref[...], q.astype(F32), _NT, preferred_element_type=F32)
    bid = lax.broadcasted_iota(jnp.int32, gate.shape, 0)
    bid_f = bid.astype(F32)
    g = jnp.where(bid < i, gate, -jnp.inf)
    sel = []
    for _ in range(min(MOBA_TOPK, nblk)):
        mx = jnp.max(g, axis=0, keepdims=True)
        idx = jnp.min(jnp.where(g == mx, bid_f, float(V7X_LANES)), axis=0, keepdims=True)
        sel.append(jnp.where(mx > -jnp.inf, idx, -1.0))
        g = jnp.where(bid_f == idx, -jnp.inf, g)

    def pick_mask(n):
        n_f = n.astype(F32)
        picked = sel[0] == n_f
        for t in sel[1:]:
            picked = jnp.logical_or(picked, t == n_f)
        return jnp.where(picked, 0.0, -jnp.inf)

    def is_far(n):
        return (pmin_ref[i] - pmax_ref[n]) >= BIAS_TABLE - 1

    def scores(off, width):
        return lax.dot_general(k_ref[pl.ds(off, width), :], qs, _NT, preferred_element_type=F32)

    def near_bias(off):
        d = jnp.clip(pq_ref[...] - pk_ref[:, pl.ds(off, blk)], 0, BIAS_TABLE - 1)
        tb = jnp.broadcast_to(tbl_ref[0], (blk, BIAS_TABLE))
        parts = [jnp.take_along_axis(tb, d[:, c * V7X_LANES:(c + 1) * V7X_LANES], axis=1)
                 for c in range(blk // V7X_LANES)]
        return jnp.concatenate(parts, axis=1).T

    def update(s, off, width):
        m = m_ref[...]
        m_new = jnp.maximum(m, jnp.max(s, axis=0, keepdims=True))
        alpha = jnp.exp(m - m_new)
        p = jnp.exp(s - m_new)
        l_ref[...] = alpha * l_ref[...] + jnp.sum(p, axis=0, keepdims=True)
        pv = jnp.dot(vt_ref[:, pl.ds(off, width)], p.astype(vt_ref.dtype), preferred_element_type=F32)
        acc_ref[...] = alpha * acc_ref[...] + pv
        m_ref[...] = m_new

    off_i = pl.multiple_of(i * blk, blk)
    key = lax.broadcasted_iota(jnp.int32, (blk, blk), 0)
    qry = lax.broadcasted_iota(jnp.int32, (blk, blk), 1)
    s = jnp.where(key <= qry, scores(off_i, blk) + near_bias(off_i), -jnp.inf)
    m = jnp.max(s, axis=0, keepdims=True)
    p = jnp.exp(s - m)
    m_ref[...] = m
    l_ref[...] = jnp.sum(p, axis=0, keepdims=True)
    acc_ref[...] = jnp.dot(vt_ref[:, pl.ds(off_i, blk)], p.astype(vt_ref.dtype), preferred_element_type=F32)

    def far_group(g, c):
        base = g * group
        off = pl.multiple_of(base * blk, group * blk)
        masks = []
        for j in range(group):
            n = base + j
            masks.append(jnp.broadcast_to(jnp.where(is_far(n), pick_mask(n), -jnp.inf), (blk, blk)))
        update(scores(off, group * blk) + jnp.concatenate(masks, axis=0), off, group * blk)
        return c

    lax.fori_loop(0, (i + group - 1) // group, far_group, 0)

    def near_block(n, c):
        @pl.when(jnp.logical_not(is_far(n)))
        def _():
            off = pl.multiple_of(n * blk, blk)
            update(scores(off, blk) + near_bias(off) + pick_mask(n), off, blk)

        return c

    lax.fori_loop(0, i, near_block, 0)
    o_ref[...] = (acc_ref[...] / l_ref[...]).T.astype(o_ref.dtype)


def _moba(proj, positions, rel_bias, *, group=4):
    s = proj.shape[0]
    blk = MOBA_BLOCK
    nblk = s // blk
    assert s % blk == 0 and nblk <= V7X_LANES and nblk % group == 0
    pos_blocks = positions.reshape(nblk, blk)
    pmin = jnp.min(pos_blocks, axis=1)
    pmax = jnp.max(pos_blocks, axis=1)
    table = rel_bias[_t5_bucket(jnp.arange(BIAS_TABLE, dtype=jnp.int32))]
    table = jnp.transpose(table).reshape(MOBA_HEADS, 1, BIAS_TABLE).astype(F32)
    table = table - table[:, :, BIAS_TABLE - 1:]
    qc, kc, vc = (c // MOBA_DH for c in (COL_MOBA_Q, COL_MOBA_K, COL_MOBA_V))
    grid_spec = pltpu.PrefetchScalarGridSpec(
        num_scalar_prefetch=2,
        grid=(MOBA_HEADS, nblk),
        in_specs=[
            pl.BlockSpec((blk, MOBA_DH), lambda h, i, *_: (i, qc + h)),
            pl.BlockSpec((s, MOBA_DH), lambda h, i, *_: (0, kc + h)),
            pl.BlockSpec((s, MOBA_DH), lambda h, i, *_: (0, vc + h)),
            pl.BlockSpec((blk, 1), lambda h, i, *_: (i, 0)),
            pl.BlockSpec((1, s), lambda h, i, *_: (0, 0)),
            pl.BlockSpec((1, 1, BIAS_TABLE), lambda h, i, *_: (h, 0, 0)),
        ],
        out_specs=pl.BlockSpec((blk, MOBA_DH), lambda h, i, *_: (i, h)),
        scratch_shapes=[
            pltpu.VMEM((V7X_LANES, MOBA_DH), F32),
            pltpu.VMEM((MOBA_DH, s), BF16),
            pltpu.VMEM((1, blk), F32),
            pltpu.VMEM((1, blk), F32),
            pltpu.VMEM((MOBA_DH, blk), F32),
        ],
    )
    return pl.pallas_call(
        functools.partial(_moba_body, nblk=nblk, scale=MOBA_DH ** -0.5, group=group),
        grid_spec=grid_spec,
        out_shape=jax.ShapeDtypeStruct((s, MOBA_HEADS * MOBA_DH), BF16),
        compiler_params=_params(("arbitrary", "arbitrary")),
        name="moba",
    )(pmin, pmax, proj, proj, proj, positions.reshape(s, 1), positions.reshape(1, s), table)


def _merge_body(oa_ref, ob_ref, ga_ref, gb_ref, wa_ref, wb_ref, wo_ref, x_ref, o_ref):
    @pl.when(pl.program_id(1) == 0)
    def _():
        o_ref[...] = x_ref[...]

    ma = jnp.dot(oa_ref[...], wa_ref[...], preferred_element_type=F32)
    mb = jnp.dot(ob_ref[...], wb_ref[...], preferred_element_type=F32)
    mg = _sigmoid(ga_ref[...].astype(F32)) * ma + _sigmoid(gb_ref[...].astype(F32)) * mb
    o_ref[...] += jnp.dot(mg.astype(BF16), wo_ref[...], preferred_element_type=F32)


def _merge(o_a, o_b, proj, w_a, w_b, w_out, x, *, tm=512, tn=256):
    s, d = x.shape
    ka, kb = o_a.shape[1], o_b.shape[1]
    ga, gb = COL_GATE_A // tn, COL_GATE_B // tn
    return pl.pallas_call(
        _merge_body,
        grid=(s // tm, d // tn),
        in_specs=[
            pl.BlockSpec((tm, ka), lambda i, j: (i, 0)),
            pl.BlockSpec((tm, kb), lambda i, j: (i, 0)),
            pl.BlockSpec((tm, tn), lambda i, j: (i, ga + j)),
            pl.BlockSpec((tm, tn), lambda i, j: (i, gb + j)),
            pl.BlockSpec((ka, tn), lambda i, j: (0, j)),
            pl.BlockSpec((kb, tn), lambda i, j: (0, j)),
            pl.BlockSpec((tn, d), lambda i, j: (j, 0)),
            _row_resident_spec(tm, d),
        ],
        out_specs=pl.BlockSpec((tm, d), lambda i, j: (i, 0)),
        out_shape=jax.ShapeDtypeStruct((s, d), F32),
        compiler_params=_params(("arbitrary", "arbitrary")),
        name="merge",
    )(o_a, o_b, proj, proj, w_a, w_b, w_out, x)


def _proj_layout(w_in):
    d = w_in.shape[0]
    lat = MLA_Q_LORA + MLA_KV_LORA + MLA_ROPE
    pad = jnp.zeros((d, COL_MOBA_Q - lat), w_in.dtype)
    return jnp.concatenate([w_in[:, :lat], pad, w_in[:, lat:]], axis=1).astype(BF16)


def _q_layout(w_uq):
    r = w_uq.shape[0]
    w = w_uq.reshape(r, MLA_HEADS, MLA_NOPE + MLA_ROPE)
    w = jnp.pad(w, ((0, 0), (0, 0), (0, MLA_QK_PAD - MLA_NOPE - MLA_ROPE)))
    return w.reshape(r, MLA_HEADS * MLA_QK_PAD).astype(BF16)


def kernel(x, positions, rel_bias, norm_ffn1, w_ffn1_in, w_ffn1_out, norm_mix, w_in, norm_cq, w_uq, norm_ckv, w_ukv, w_br_a, w_br_b, w_out, norm_ffn2, w_ffn2_in, w_ffn2_out, norm_final):
    b, s, d = x.shape
    assert b == 1 and norm_ffn1.shape[0] == 1
    pos = positions[0]
    x0 = x[0]

    x1 = _ffn(x0, norm_ffn1[0], w_ffn1_in[0].astype(BF16), w_ffn1_out[0].astype(BF16))
    proj = _proj(x1, norm_mix[0], _proj_layout(w_in[0]))
    tables = _rope_tables(pos)
    q = _q_up(proj, norm_cq[0], _q_layout(w_uq[0]), tables)
    k, v = _kv_up(proj, norm_ckv[0], w_ukv[0].astype(BF16), tables)
    o_a = _mla(q, k, v)
    o_b = _moba(proj, pos, rel_bias)
    x2 = _merge(o_a, o_b, proj, w_br_a[0].astype(BF16), w_br_b[0].astype(BF16), w_out[0].astype(BF16), x1)
    out = _ffn(x2, norm_ffn2[0], w_ffn2_in[0].astype(BF16), w_ffn2_out[0].astype(BF16), norm_final)
    return out[None]
```

```python
import functools
import math

import jax
import jax.numpy as jnp
from jax import lax
from jax.experimental import pallas as pl
from jax.experimental.pallas import tpu as pltpu

F32 = jnp.float32
BF16 = jnp.bfloat16

RMS_EPS = 1e-6
ROPE_THETA = 10000.0
LOG2E = math.log2(math.e)
MASKED = -1e30

MLA_HEADS = 16
MLA_Q_LORA = 1024
MLA_KV_LORA = 512
MLA_NOPE = 128
MLA_ROPE = 64
MLA_V = 128
MLA_QK_PAD = 256

MOBA_HEADS = 16
MOBA_DH = 128
MOBA_BLOCK = 256
MOBA_TOPK = 3
REL_BUCKETS = 32
REL_MAX_DIST = 128
BIAS_TABLE = 128

V7X_LANES = 128
V7X_VMEM_LIMIT = 56 * 1024 * 1024

COL_CQ = 0
COL_CKV = 1024
COL_KROPE = 1536
COL_MOBA_Q = 2048
COL_MOBA_K = 4096
COL_MOBA_V = 6144
COL_GATE_A = 8192
COL_GATE_B = 12288
PROJ_COLS = 16384

_NT = (((1,), (1,)), ((), ()))


def _params(sem):
    return pltpu.CompilerParams(dimension_semantics=sem, vmem_limit_bytes=V7X_VMEM_LIMIT)


def _row_resident_spec(tm, d):
    return pl.BlockSpec((tm, d), lambda i, j: (i, 0), pipeline_mode=pl.Buffered(1))


def _rms(xf, gain):
    ms = jnp.mean(xf * xf, axis=-1, keepdims=True)
    return xf * lax.rsqrt(ms + RMS_EPS) * gain


def _sigmoid(z):
    return 1.0 / (1.0 + jnp.exp(-z))


def _ffn_body(x_ref, g_ref, wg_ref, wu_ref, wo_ref, *rest, final_norm):
    if final_norm:
        gf_ref, o_ref, h_ref = rest
    else:
        o_ref, h_ref = rest
    j = pl.program_id(1)

    @pl.when(j == 0)
    def _():
        xf = x_ref[...]
        h_ref[...] = _rms(xf, g_ref[...]).astype(BF16)
        o_ref[...] = xf

    h = h_ref[...]
    g = jnp.dot(h, wg_ref[...], preferred_element_type=F32)
    u = jnp.dot(h, wu_ref[...], preferred_element_type=F32)
    a = (g * _sigmoid(g) * (0.5 * u)).astype(BF16)
    o_ref[...] += jnp.dot(a, wo_ref[...], preferred_element_type=F32)

    if final_norm:
        @pl.when(j == pl.num_programs(1) - 1)
        def _():
            o_ref[...] = _rms(o_ref[...], gf_ref[...])


def _ffn(x, gain, w_in, w_out, final_gain=None, *, tm=512, tf=256):
    s, d = x.shape
    f = w_out.shape[0]
    nf = f // tf
    assert s % tm == 0 and f % tf == 0 and w_in.shape == (d, 2 * f)
    in_specs = [
        _row_resident_spec(tm, d),
        pl.BlockSpec((1, d), lambda i, j: (0, 0)),
        pl.BlockSpec((d, tf), lambda i, j: (0, j)),
        pl.BlockSpec((d, tf), lambda i, j: (0, j + nf)),
        pl.BlockSpec((tf, d), lambda i, j: (j, 0)),
    ]
    args = [x, gain.reshape(1, d), w_in, w_in, w_out]
    if final_gain is not None:
        in_specs.append(pl.BlockSpec((1, d), lambda i, j: (0, 0)))
        args.append(final_gain.reshape(1, d))
    return pl.pallas_call(
        functools.partial(_ffn_body, final_norm=final_gain is not None),
        grid=(s // tm, nf),
        in_specs=in_specs,
        out_specs=pl.BlockSpec((tm, d), lambda i, j: (i, 0)),
        out_shape=jax.ShapeDtypeStruct((s, d), F32),
        scratch_shapes=[pltpu.VMEM((tm, d), BF16)],
        compiler_params=_params(("arbitrary", "arbitrary")),
        name="ffn",
    )(*args)


def _proj_body(x_ref, g_ref, w_ref, o_ref, h_ref):
    @pl.when(pl.program_id(1) == 0)
    def _():
        h_ref[...] = _rms(x_ref[...], g_ref[...]).astype(BF16)

    o_ref[...] = jnp.dot(h_ref[...], w_ref[...], preferred_element_type=F32).astype(o_ref.dtype)


def _proj(x, gain, w, *, tm=512, tn=1024):
    s, d = x.shape
    n = w.shape[1]
    assert s % tm == 0 and n % tn == 0
    return pl.pallas_call(
        _proj_body,
        grid=(s // tm, n // tn),
        in_specs=[
            _row_resident_spec(tm, d),
            pl.BlockSpec((1, d), lambda i, j: (0, 0)),
            pl.BlockSpec((d, tn), lambda i, j: (0, j)),
        ],
        out_specs=pl.BlockSpec((tm, tn), lambda i, j: (i, j)),
        out_shape=jax.ShapeDtypeStruct((s, n), BF16),
        scratch_shapes=[pltpu.VMEM((tm, d), BF16)],
        compiler_params=_params(("arbitrary", "arbitrary")),
        name="proj",
    )(x, gain.reshape(1, d), w)


def _rope_lanes(r, cf, s1, s2):
    return r * cf + pltpu.roll(r, 96, 1) * s1 + pltpu.roll(r, 32, 1) * s2


def _rope_tables(positions):
    half = MLA_ROPE // 2
    inv_freq = ROPE_THETA ** (-jnp.arange(0, MLA_ROPE, 2, dtype=F32) / MLA_ROPE)
    ang = positions[:, None].astype(F32) * inv_freq
    cos, sin = jnp.cos(ang), jnp.sin(ang)
    z = jnp.zeros_like(cos)
    pad = jnp.zeros((positions.shape[0], V7X_LANES - MLA_ROPE), F32)
    cf = jnp.concatenate([cos, cos, pad], axis=1)
    s1 = jnp.concatenate([-sin, z, pad], axis=1)
    s2 = jnp.concatenate([z, sin, pad], axis=1)
    return cf, s1, s2


def _qup_body(c_ref, g_ref, w_ref, cf_ref, s1_ref, s2_ref, q_ref, h_ref, *, scale):
    @pl.when(pl.program_id(1) == 0)
    def _():
        h_ref[...] = _rms(c_ref[...].astype(F32), g_ref[...]).astype(BF16)

    y = jnp.dot(h_ref[...], w_ref[...], preferred_element_type=F32)
    q_ref[0, :, :MLA_NOPE] = (y[:, :MLA_NOPE] * scale).astype(q_ref.dtype)
    r = _rope_lanes(y[:, MLA_NOPE:], cf_ref[...], s1_ref[...], s2_ref[...])
    q_ref[0, :, MLA_NOPE:] = (r * scale).astype(q_ref.dtype)


def _q_up(proj, gain, w_q, tables, *, tm=1024):
    s = proj.shape[0]
    cf, s1, s2 = tables
    scale = (MLA_NOPE + MLA_ROPE) ** -0.5 * LOG2E
    tab_spec = pl.BlockSpec((tm, V7X_LANES), lambda i, h: (i, 0))
    return pl.pallas_call(
        functools.partial(_qup_body, scale=scale),
        grid=(s // tm, MLA_HEADS),
        in_specs=[
            pl.BlockSpec((tm, MLA_Q_LORA), lambda i, h: (i, COL_CQ // MLA_Q_LORA)),
            pl.BlockSpec((1, MLA_Q_LORA), lambda i, h: (0, 0)),
            pl.BlockSpec((MLA_Q_LORA, MLA_QK_PAD), lambda i, h: (0, h)),
            tab_spec, tab_spec, tab_spec,
        ],
        out_specs=pl.BlockSpec((1, tm, MLA_QK_PAD), lambda i, h: (h, i, 0)),
        out_shape=jax.ShapeDtypeStruct((MLA_HEADS, s, MLA_QK_PAD), BF16),
        scratch_shapes=[pltpu.VMEM((tm, MLA_Q_LORA), BF16)],
        compiler_params=_params(("arbitrary", "arbitrary")),
        name="q_up",
    )(proj, gain.reshape(1, MLA_Q_LORA), w_q, cf, s1, s2)


def _kvup_body(c_ref, kr_ref, g_ref, w_ref, cf_ref, s1_ref, s2_ref, k_ref, v_ref, h_ref, r_ref):
    @pl.when(pl.program_id(1) == 0)
    def _():
        h_ref[...] = _rms(c_ref[...].astype(F32), g_ref[...]).astype(BF16)
        r_ref[...] = _rope_lanes(kr_ref[...].astype(F32), cf_ref[...], s1_ref[...], s2_ref[...]).astype(BF16)

    y = jnp.dot(h_ref[...], w_ref[...], preferred_element_type=F32)
    k_ref[0, :, :MLA_NOPE] = y[:, :MLA_NOPE].astype(k_ref.dtype)
    k_ref[0, :, MLA_NOPE:] = r_ref[...]
    v_ref[0] = y[:, MLA_NOPE:].astype(v_ref.dtype)


def _kv_up(proj, gain, w_ukv, tables, *, tm=1024):
    s = proj.shape[0]
    cf, s1, s2 = tables
    tab_spec = pl.BlockSpec((tm, V7X_LANES), lambda i, h: (i, 0))
    return pl.pallas_call(
        _kvup_body,
        grid=(s // tm, MLA_HEADS),
        in_specs=[
            pl.BlockSpec((tm, MLA_KV_LORA), lambda i, h: (i, COL_CKV // MLA_KV_LORA)),
            pl.BlockSpec((tm, V7X_LANES), lambda i, h: (i, COL_KROPE // V7X_LANES)),
            pl.BlockSpec((1, MLA_KV_LORA), lambda i, h: (0, 0)),
            pl.BlockSpec((MLA_KV_LORA, MLA_NOPE + MLA_V), lambda i, h: (0, h)),
            tab_spec, tab_spec, tab_spec,
        ],
        out_specs=[
            pl.BlockSpec((1, tm, MLA_QK_PAD), lambda i, h: (h, i, 0)),
            pl.BlockSpec((1, tm, MLA_V), lambda i, h: (h, i, 0)),
        ],
        out_shape=[
            jax.ShapeDtypeStruct((MLA_HEADS, s, MLA_QK_PAD), BF16),
            jax.ShapeDtypeStruct((MLA_HEADS, s, MLA_V), BF16),
        ],
        scratch_shapes=[pltpu.VMEM((tm, MLA_KV_LORA), BF16), pltpu.VMEM((tm, V7X_LANES), BF16)],
        compiler_params=_params(("arbitrary", "arbitrary")),
        name="kv_up",
    )(proj, proj, gain.reshape(1, MLA_KV_LORA), w_ukv, cf, s1, s2)


def _softmax_step(s, v, m, l, acc):
    m_new = jnp.maximum(m, jnp.max(s, axis=1, keepdims=True))
    alpha = jnp.exp2(m - m_new)
    p = jnp.exp2(s - m_new)
    l = alpha * l + jnp.sum(p, axis=1, keepdims=True)
    acc = alpha * acc + jnp.dot(p.astype(v.dtype), v, preferred_element_type=F32)
    return m_new, l, acc


def _mla_body(q_ref, k_ref, v_ref, o_ref, *, t):
    i = pl.program_id(1)
    q = q_ref[0]

    def scores(n):
        return lax.dot_general(q, k_ref[0, pl.ds(pl.multiple_of(n * t, t), t), :], _NT, preferred_element_type=F32)

    def values(n):
        return v_ref[0, pl.ds(pl.multiple_of(n * t, t), t), :]

    def body(n, c):
        return _softmax_step(scores(n), values(n), *c)

    init = (jnp.full((t, 1), -jnp.inf, F32), jnp.zeros((t, 1), F32), jnp.zeros((t, MLA_V), F32))
    m, l, acc = lax.fori_loop(0, i, body, init)
    row = lax.broadcasted_iota(jnp.int32, (t, t), 0)
    col = lax.broadcasted_iota(jnp.int32, (t, t), 1)
    m, l, acc = _softmax_step(jnp.where(col <= row, scores(i), -jnp.inf), values(i), m, l, acc)
    o_ref[...] = (acc / l).astype(o_ref.dtype)


def _mla(q, k, v, *, t=1024):
    h, s, _ = q.shape
    return pl.pallas_call(
        functools.partial(_mla_body, t=t),
        grid=(h, s // t),
        in_specs=[
            pl.BlockSpec((1, t, MLA_QK_PAD), lambda h, i: (h, i, 0)),
            pl.BlockSpec((1, s, MLA_QK_PAD), lambda h, i: (h, 0, 0)),
            pl.BlockSpec((1, s, MLA_V), lambda h, i: (h, 0, 0)),
        ],
        out_specs=pl.BlockSpec((t, MLA_V), lambda h, i: (i, h)),
        out_shape=jax.ShapeDtypeStruct((s, h * MLA_V), BF16),
        compiler_params=_params(("arbitrary", "arbitrary")),
        name="mla",
    )(q, k, v)


def _t5_bucket(n):
    max_exact = REL_BUCKETS // 2
    n_f = jnp.maximum(n, max_exact).astype(F32)
    large = max_exact + (jnp.log(n_f / max_exact) / math.log(REL_MAX_DIST / max_exact)
                         * (REL_BUCKETS - max_exact)).astype(jnp.int32)
    large = jnp.minimum(large, REL_BUCKETS - 1)
    return jnp.where(n < max_exact, n, large)


def _moba_body(pmin_ref, pmax_ref, q_ref, k_ref, v_ref, pq_ref, pk_ref, pmaxl_ref, tbl_ref, o_ref,
               kmean_ref, kaug_ref, vt_ref, m_ref, l_ref, acc_ref, *, nblk, scale, group):
    i = pl.program_id(1)
    blk = MOBA_BLOCK
    lane_id = lax.broadcasted_iota(jnp.int32, (blk, V7X_LANES), 1)

    @pl.when(i == 0)
    def _():
        kmean_ref[...] = jnp.zeros_like(kmean_ref)

        def fill(n, c):
            rows = pl.ds(pl.multiple_of(n * blk, blk), blk)
            kb = k_ref[rows, :]
            kmean_ref[pl.ds(n, 1), :] = jnp.sum(kb.astype(F32), axis=0, keepdims=True) * (1.0 / blk)
            kaug_ref[rows, :MOBA_DH] = kb
            kaug_ref[rows, MOBA_DH:] = jnp.where(lane_id == n, 1.0, 0.0).astype(kaug_ref.dtype)
            vt_ref[:, rows] = v_ref[rows, :].astype(F32).T.astype(vt_ref.dtype)
            return c

        lax.fori_loop(0, nblk, fill, 0)

    q = q_ref[...]
    qs = (q.astype(F32) * scale).astype(q.dtype)

    gate = lax.dot_general(q.astype(F32), kmean_ref[...], _NT, preferred_element_type=F32)
    lane_f = lane_id.astype(F32)
    g = jnp.where(lane_id < i, gate, -jnp.inf)
    picked = jnp.zeros(gate.shape, F32)
    for _ in range(min(MOBA_TOPK, nblk)):
        mx = jnp.max(g, axis=1, keepdims=True)
        idx = jnp.min(jnp.where(g == mx, lane_f, float(V7X_LANES)), axis=1, keepdims=True)
        hit = lane_f == jnp.where(mx > -jnp.inf, idx, -1.0)
        picked = jnp.where(hit, 1.0, picked)
        g = jnp.where(hit, -jnp.inf, g)

    far_lanes = (pmin_ref[i] - pmaxl_ref[...]) >= BIAS_TABLE - 1
    q_any = jnp.concatenate([qs, jnp.where(picked > 0.0, 0.0, MASKED).astype(qs.dtype)], axis=1)
    q_far = jnp.concatenate([qs, jnp.where(far_lanes, jnp.where(picked > 0.0, 0.0, MASKED), MASKED).astype(qs.dtype)],
                            axis=1)

    def is_far(n):
        return (pmin_ref[i] - pmax_ref[n]) >= BIAS_TABLE - 1

    def masked_scores(off, width, q_aug):
        return lax.dot_general(kaug_ref[pl.ds(off, width), :], q_aug, _NT, preferred_element_type=F32)

    def near_bias(off):
        d = jnp.clip(pq_ref[...] - pk_ref[:, pl.ds(off, blk)], 0, BIAS_TABLE - 1)
        tb = jnp.broadcast_to(tbl_ref[0], (blk, BIAS_TABLE))
        parts = [jnp.take_along_axis(tb, d[:, c * V7X_LANES:(c + 1) * V7X_LANES], axis=1)
                 for c in range(blk // V7X_LANES)]
        return jnp.concatenate(parts, axis=1).T

    def update(s, off, width):
        m = m_ref[...]
        m_new = jnp.maximum(m, jnp.max(s, axis=0, keepdims=True))
        alpha = jnp.exp2(m - m_new)
        p = jnp.exp2(s - m_new)
        l_ref[...] = alpha * l_ref[...] + jnp.sum(p, axis=0, keepdims=True)
        pv = jnp.dot(vt_ref[:, pl.ds(off, width)], p.astype(vt_ref.dtype), preferred_element_type=F32)
        acc_ref[...] = alpha * acc_ref[...] + pv
        m_ref[...] = m_new

    off_i = pl.multiple_of(i * blk, blk)
    key = lax.broadcasted_iota(jnp.int32, (blk, blk), 0)
    qry = lax.broadcasted_iota(jnp.int32, (blk, blk), 1)
    s = lax.dot_general(k_ref[pl.ds(off_i, blk), :], qs, _NT, preferred_element_type=F32) + near_bias(off_i)
    s = jnp.where(key <= qry, s, -jnp.inf)
    m = jnp.max(s, axis=0, keepdims=True)
    p = jnp.exp2(s - m)
    m_ref[...] = m
    l_ref[...] = jnp.sum(p, axis=0, keepdims=True)
    acc_ref[...] = jnp.dot(vt_ref[:, pl.ds(off_i, blk)], p.astype(vt_ref.dtype), preferred_element_type=F32)

    span = group * blk

    def far_scores(g):
        return masked_scores(pl.multiple_of(g * span, span), span, q_far)

    def far_group(g, s):
        s_next = far_scores(jnp.minimum(g + 1, nblk // group - 1))
        update(s, pl.multiple_of(g * span, span), span)
        return s_next

    lax.fori_loop(0, (i + group - 1) // group, far_group, far_scores(0))

    def near_block(n, c):
        @pl.when(jnp.logical_not(is_far(n)))
        def _():
            off = pl.multiple_of(n * blk, blk)
            update(masked_scores(off, blk, q_any) + near_bias(off), off, blk)

        return c

    lax.fori_loop(0, i, near_block, 0)
    o_ref[...] = (acc_ref[...] / l_ref[...]).T.astype(o_ref.dtype)


def _moba(proj, positions, rel_bias, *, group=4):
    s = proj.shape[0]
    blk = MOBA_BLOCK
    nblk = s // blk
    assert s % blk == 0 and nblk <= V7X_LANES and nblk % group == 0
    pos_blocks = positions.reshape(nblk, blk)
    pmin = jnp.min(pos_blocks, axis=1)
    pmax = jnp.max(pos_blocks, axis=1)
    table = rel_bias[_t5_bucket(jnp.arange(BIAS_TABLE, dtype=jnp.int32))]
    table = jnp.transpose(table).reshape(MOBA_HEADS, 1, BIAS_TABLE).astype(F32)
    table = (table - table[:, :, BIAS_TABLE - 1:]) * LOG2E
    pmax_lanes = jnp.pad(pmax, (0, V7X_LANES - nblk)).reshape(1, V7X_LANES)
    qc, kc, vc = (c // MOBA_DH for c in (COL_MOBA_Q, COL_MOBA_K, COL_MOBA_V))
    grid_spec = pltpu.PrefetchScalarGridSpec(
        num_scalar_prefetch=2,
        grid=(MOBA_HEADS, nblk),
        in_specs=[
            pl.BlockSpec((blk, MOBA_DH), lambda h, i, *_: (i, qc + h)),
            pl.BlockSpec((s, MOBA_DH), lambda h, i, *_: (0, kc + h)),
            pl.BlockSpec((s, MOBA_DH), lambda h, i, *_: (0, vc + h)),
            pl.BlockSpec((blk, 1), lambda h, i, *_: (i, 0)),
            pl.BlockSpec((1, s), lambda h, i, *_: (0, 0)),
            pl.BlockSpec((1, V7X_LANES), lambda h, i, *_: (0, 0)),
            pl.BlockSpec((1, 1, BIAS_TABLE), lambda h, i, *_: (h, 0, 0)),
        ],
        out_specs=pl.BlockSpec((blk, MOBA_DH), lambda h, i, *_: (i, h)),
        scratch_shapes=[
            pltpu.VMEM((V7X_LANES, MOBA_DH), F32),
            pltpu.VMEM((s, 2 * MOBA_DH), BF16),
            pltpu.VMEM((MOBA_DH, s), BF16),
            pltpu.VMEM((1, blk), F32),
            pltpu.VMEM((1, blk), F32),
            pltpu.VMEM((MOBA_DH, blk), F32),
        ],
    )
    return pl.pallas_call(
        functools.partial(_moba_body, nblk=nblk, scale=MOBA_DH ** -0.5 * LOG2E, group=group),
        grid_spec=grid_spec,
        out_shape=jax.ShapeDtypeStruct((s, MOBA_HEADS * MOBA_DH), BF16),
        compiler_params=_params(("arbitrary", "arbitrary")),
        name="moba",
    )(pmin, pmax, proj, proj, proj, positions.reshape(s, 1), positions.reshape(1, s), pmax_lanes, table)


def _merge_body(oa_ref, ob_ref, ga_ref, gb_ref, wa_ref, wb_ref, wo_ref, x_ref, o_ref):
    @pl.when(pl.program_id(1) == 0)
    def _():
        o_ref[...] = x_ref[...]

    ma = jnp.dot(oa_ref[...], wa_ref[...], preferred_element_type=F32)
    mb = jnp.dot(ob_ref[...], wb_ref[...], preferred_element_type=F32)
    mg = _sigmoid(ga_ref[...].astype(F32)) * ma + _sigmoid(gb_ref[...].astype(F32)) * mb
    o_ref[...] += jnp.dot(mg.astype(BF16), wo_ref[...], preferred_element_type=F32)


def _merge(o_a, o_b, proj, w_a, w_b, w_out, x, *, tm=512, tn=256):
    s, d = x.shape
    ka, kb = o_a.shape[1], o_b.shape[1]
    ga, gb = COL_GATE_A // tn, COL_GATE_B // tn
    return pl.pallas_call(
        _merge_body,
        grid=(s // tm, d // tn),
        in_specs=[
            pl.BlockSpec((tm, ka), lambda i, j: (i, 0)),
            pl.BlockSpec((tm, kb), lambda i, j: (i, 0)),
            pl.BlockSpec((tm, tn), lambda i, j: (i, ga + j)),
            pl.BlockSpec((tm, tn), lambda i, j: (i, gb + j)),
            pl.BlockSpec((ka, tn), lambda i, j: (0, j)),
            pl.BlockSpec((kb, tn), lambda i, j: (0, j)),
            pl.BlockSpec((tn, d), lambda i, j: (j, 0)),
            _row_resident_spec(tm, d),
        ],
        out_specs=pl.BlockSpec((tm, d), lambda i, j: (i, 0)),
        out_shape=jax.ShapeDtypeStruct((s, d), F32),
        compiler_params=_params(("arbitrary", "arbitrary")),
        name="merge",
    )(o_a, o_b, proj, proj, w_a, w_b, w_out, x)


def _proj_layout(w_in):
    d = w_in.shape[0]
    lat = MLA_Q_LORA + MLA_KV_LORA + MLA_ROPE
    pad = jnp.zeros((d, COL_MOBA_Q - lat), w_in.dtype)
    return jnp.concatenate([w_in[:, :lat], pad, w_in[:, lat:]], axis=1).astype(BF16)


def _q_layout(w_uq):
    r = w_uq.shape[0]
    w = w_uq.reshape(r, MLA_HEADS, MLA_NOPE + MLA_ROPE)
    w = jnp.pad(w, ((0, 0), (0, 0), (0, MLA_QK_PAD - MLA_NOPE - MLA_ROPE)))
    return w.reshape(r, MLA_HEADS * MLA_QK_PAD).astype(BF16)


def kernel(x, positions, rel_bias, norm_ffn1, w_ffn1_in, w_ffn1_out, norm_mix, w_in, norm_cq, w_uq, norm_ckv, w_ukv, w_br_a, w_br_b, w_out, norm_ffn2, w_ffn2_in, w_ffn2_out, norm_final):
    b, s, d = x.shape
    assert b == 1 and norm_ffn1.shape[0] == 1
    pos = positions[0]
    x0 = x[0]

    x1 = _ffn(x0, norm_ffn1[0], w_ffn1_in[0].astype(BF16), w_ffn1_out[0].astype(BF16))
    proj = _proj(x1, norm_mix[0], _proj_layout(w_in[0]))
    tables = _rope_tables(pos)
    q = _q_up(proj, norm_cq[0], _q_layout(w_uq[0]), tables)
    k, v = _kv_up(proj, norm_ckv[0], w_ukv[0].astype(BF16), tables)
    o_a = _mla(q, k, v)
    o_b = _moba(proj, pos, rel_bias)
    x2 = _merge(o_a, o_b, proj, w_br_a[0].astype(BF16), w_br_b[0].astype(BF16), w_out[0].astype(BF16), x1)
    out = _ffn(x2, norm_ffn2[0], w_ffn2_in[0].astype(BF16), w_ffn2_out[0].astype(BF16), norm_final)
    return out[None]
```

```python
import functools
import math

import jax
import jax.numpy as jnp
from jax import lax
from jax.experimental import pallas as pl
from jax.experimental.pallas import tpu as pltpu

F32 = jnp.float32
BF16 = jnp.bfloat16

RMS_EPS = 1e-6
ROPE_THETA = 10000.0
LOG2E = math.log2(math.e)
MASKED = -1e30

MLA_HEADS = 16
MLA_Q_LORA = 1024
MLA_KV_LORA = 512
MLA_NOPE = 128
MLA_ROPE = 64
MLA_V = 128
MLA_QK_PAD = 256

MOBA_HEADS = 16
MOBA_DH = 128
MOBA_BLOCK = 256
MOBA_TOPK = 3
REL_BUCKETS = 32
REL_MAX_DIST = 128
BIAS_TABLE = 128

V7X_LANES = 128
V7X_VMEM_LIMIT = 56 * 1024 * 1024

COL_CQ = 0
COL_CKV = 1024
COL_KROPE = 1536
COL_MOBA_Q = 2048
COL_MOBA_K = 4096
COL_MOBA_V = 6144
COL_GATE_A = 8192
COL_GATE_B = 12288
PROJ_COLS = 16384

_NT = (((1,), (1,)), ((), ()))


def _params(sem):
    return pltpu.CompilerParams(dimension_semantics=sem, vmem_limit_bytes=V7X_VMEM_LIMIT)


def _row_resident_spec(tm, d):
    return pl.BlockSpec((tm, d), lambda i, j: (i, 0), pipeline_mode=pl.Buffered(1))


def _rms(xf, gain):
    ms = jnp.mean(xf * xf, axis=-1, keepdims=True)
    return xf * lax.rsqrt(ms + RMS_EPS) * gain


def _sigmoid(z):
    return 1.0 / (1.0 + jnp.exp(-z))


def _ffn_body(x_ref, g_ref, wg_ref, wu_ref, wo_ref, *rest, final_norm):
    if final_norm:
        gf_ref, o_ref, h_ref = rest
    else:
        o_ref, h_ref = rest
    j = pl.program_id(1)

    @pl.when(j == 0)
    def _():
        xf = x_ref[...]
        h_ref[...] = _rms(xf, g_ref[...]).astype(BF16)
        o_ref[...] = xf

    h = h_ref[...]
    g = jnp.dot(h, wg_ref[...], preferred_element_type=F32)
    u = jnp.dot(h, wu_ref[...], preferred_element_type=F32)
    a = (g * _sigmoid(g) * (0.5 * u)).astype(BF16)
    o_ref[...] += jnp.dot(a, wo_ref[...], preferred_element_type=F32)

    if final_norm:
        @pl.when(j == pl.num_programs(1) - 1)
        def _():
            o_ref[...] = _rms(o_ref[...], gf_ref[...])


def _ffn(x, gain, w_in, w_out, final_gain=None, *, tm=512, tf=256):
    s, d = x.shape
    f = w_out.shape[0]
    nf = f // tf
    assert s % tm == 0 and f % tf == 0 and w_in.shape == (d, 2 * f)
    in_specs = [
        _row_resident_spec(tm, d),
        pl.BlockSpec((1, d), lambda i, j: (0, 0)),
        pl.BlockSpec((d, tf), lambda i, j: (0, j)),
        pl.BlockSpec((d, tf), lambda i, j: (0, j + nf)),
        pl.BlockSpec((tf, d), lambda i, j: (j, 0)),
    ]
    args = [x, gain.reshape(1, d), w_in, w_in, w_out]
    if final_gain is not None:
        in_specs.append(pl.BlockSpec((1, d), lambda i, j: (0, 0)))
        args.append(final_gain.reshape(1, d))
    return pl.pallas_call(
        functools.partial(_ffn_body, final_norm=final_gain is not None),
        grid=(s // tm, nf),
        in_specs=in_specs,
        out_specs=pl.BlockSpec((tm, d), lambda i, j: (i, 0)),
        out_shape=jax.ShapeDtypeStruct((s, d), F32),
        scratch_shapes=[pltpu.VMEM((tm, d), BF16)],
        compiler_params=_params(("arbitrary", "arbitrary")),
        name="ffn",
    )(*args)


def _proj_body(x_ref, g_ref, w_ref, o_ref, h_ref):
    @pl.when(pl.program_id(1) == 0)
    def _():
        h_ref[...] = _rms(x_ref[...], g_ref[...]).astype(BF16)

    o_ref[...] = jnp.dot(h_ref[...], w_ref[...], preferred_element_type=F32).astype(o_ref.dtype)


def _proj(x, gain, w, *, tm=512, tn=1024):
    s, d = x.shape
    n = w.shape[1]
    assert s % tm == 0 and n % tn == 0
    return pl.pallas_call(
        _proj_body,
        grid=(s // tm, n // tn),
        in_specs=[
            _row_resident_spec(tm, d),
            pl.BlockSpec((1, d), lambda i, j: (0, 0)),
            pl.BlockSpec((d, tn), lambda i, j: (0, j)),
        ],
        out_specs=pl.BlockSpec((tm, tn), lambda i, j: (i, j)),
        out_shape=jax.ShapeDtypeStruct((s, n), BF16),
        scratch_shapes=[pltpu.VMEM((tm, d), BF16)],
        compiler_params=_params(("arbitrary", "arbitrary")),
        name="proj",
    )(x, gain.reshape(1, d), w)


def _rope_lanes(r, cf, s1, s2):
    return r * cf + pltpu.roll(r, 96, 1) * s1 + pltpu.roll(r, 32, 1) * s2


def _rope_tables(positions):
    half = MLA_ROPE // 2
    inv_freq = ROPE_THETA ** (-jnp.arange(0, MLA_ROPE, 2, dtype=F32) / MLA_ROPE)
    ang = positions[:, None].astype(F32) * inv_freq
    cos, sin = jnp.cos(ang), jnp.sin(ang)
    z = jnp.zeros_like(cos)
    pad = jnp.zeros((positions.shape[0], V7X_LANES - MLA_ROPE), F32)
    cf = jnp.concatenate([cos, cos, pad], axis=1)
    s1 = jnp.concatenate([-sin, z, pad], axis=1)
    s2 = jnp.concatenate([z, sin, pad], axis=1)
    return cf, s1, s2


def _qup_body(c_ref, g_ref, w_ref, cf_ref, s1_ref, s2_ref, q_ref, h_ref, *, scale):
    @pl.when(pl.program_id(1) == 0)
    def _():
        h_ref[...] = _rms(c_ref[...].astype(F32), g_ref[...]).astype(BF16)

    y = jnp.dot(h_ref[...], w_ref[...], preferred_element_type=F32)
    q_ref[0, :, :MLA_NOPE] = (y[:, :MLA_NOPE] * scale).astype(q_ref.dtype)
    r = _rope_lanes(y[:, MLA_NOPE:], cf_ref[...], s1_ref[...], s2_ref[...])
    q_ref[0, :, MLA_NOPE:] = (r * scale).astype(q_ref.dtype)


def _q_up(proj, gain, w_q, tables, *, tm=1024):
    s = proj.shape[0]
    cf, s1, s2 = tables
    scale = (MLA_NOPE + MLA_ROPE) ** -0.5 * LOG2E
    tab_spec = pl.BlockSpec((tm, V7X_LANES), lambda i, h: (i, 0))
    return pl.pallas_call(
        functools.partial(_qup_body, scale=scale),
        grid=(s // tm, MLA_HEADS),
        in_specs=[
            pl.BlockSpec((tm, MLA_Q_LORA), lambda i, h: (i, COL_CQ // MLA_Q_LORA)),
            pl.BlockSpec((1, MLA_Q_LORA), lambda i, h: (0, 0)),
            pl.BlockSpec((MLA_Q_LORA, MLA_QK_PAD), lambda i, h: (0, h)),
            tab_spec, tab_spec, tab_spec,
        ],
        out_specs=pl.BlockSpec((1, tm, MLA_QK_PAD), lambda i, h: (h, i, 0)),
        out_shape=jax.ShapeDtypeStruct((MLA_HEADS, s, MLA_QK_PAD), BF16),
        scratch_shapes=[pltpu.VMEM((tm, MLA_Q_LORA), BF16)],
        compiler_params=_params(("arbitrary", "arbitrary")),
        name="q_up",
    )(proj, gain.reshape(1, MLA_Q_LORA), w_q, cf, s1, s2)


def _kvup_body(c_ref, kr_ref, g_ref, w_ref, cf_ref, s1_ref, s2_ref, k_ref, v_ref, h_ref, r_ref):
    @pl.when(pl.program_id(1) == 0)
    def _():
        h_ref[...] = _rms(c_ref[...].astype(F32), g_ref[...]).astype(BF16)
        r_ref[...] = _rope_lanes(kr_ref[...].astype(F32), cf_ref[...], s1_ref[...], s2_ref[...]).astype(BF16)

    y = jnp.dot(h_ref[...], w_ref[...], preferred_element_type=F32)
    k_ref[0, :, :MLA_NOPE] = y[:, :MLA_NOPE].astype(k_ref.dtype)
    k_ref[0, :, MLA_NOPE:] = r_ref[...]
    v_ref[0] = y[:, MLA_NOPE:].T.astype(v_ref.dtype)


def _kv_up(proj, gain, w_ukv, tables, *, tm=1024):
    s = proj.shape[0]
    cf, s1, s2 = tables
    tab_spec = pl.BlockSpec((tm, V7X_LANES), lambda i, h: (i, 0))
    return pl.pallas_call(
        _kvup_body,
        grid=(s // tm, MLA_HEADS),
        in_specs=[
            pl.BlockSpec((tm, MLA_KV_LORA), lambda i, h: (i, COL_CKV // MLA_KV_LORA)),
            pl.BlockSpec((tm, V7X_LANES), lambda i, h: (i, COL_KROPE // V7X_LANES)),
            pl.BlockSpec((1, MLA_KV_LORA), lambda i, h: (0, 0)),
            pl.BlockSpec((MLA_KV_LORA, MLA_NOPE + MLA_V), lambda i, h: (0, h)),
            tab_spec, tab_spec, tab_spec,
        ],
        out_specs=[
            pl.BlockSpec((1, tm, MLA_QK_PAD), lambda i, h: (h, i, 0)),
            pl.BlockSpec((1, MLA_V, tm), lambda i, h: (h, 0, i)),
        ],
        out_shape=[
            jax.ShapeDtypeStruct((MLA_HEADS, s, MLA_QK_PAD), BF16),
            jax.ShapeDtypeStruct((MLA_HEADS, MLA_V, s), BF16),
        ],
        scratch_shapes=[pltpu.VMEM((tm, MLA_KV_LORA), BF16), pltpu.VMEM((tm, V7X_LANES), BF16)],
        compiler_params=_params(("arbitrary", "arbitrary")),
        name="kv_up",
    )(proj, proj, gain.reshape(1, MLA_KV_LORA), w_ukv, cf, s1, s2)


def _softmax_step_t(s, vt, m, l, acc):
    m_new = jnp.maximum(m, jnp.max(s, axis=0, keepdims=True))
    alpha = jnp.exp2(m - m_new)
    p = jnp.exp2(s - m_new)
    l = alpha * l + jnp.sum(p, axis=0, keepdims=True)
    acc = alpha * acc + jnp.dot(vt, p.astype(vt.dtype), preferred_element_type=F32)
    return m_new, l, acc


def _softmax_chunks_t(scores_fn, vt_fn, n, last_chunk, state, s_ref, p_ref):
    m_ref, l_ref, acc_ref, a_ref = state

    def flush(g, slot):
        pv = jnp.dot(vt_fn(jnp.maximum(g, 0)), p_ref[slot], preferred_element_type=F32)
        acc_ref[...] = a_ref[...] * acc_ref[...] + pv

    def step(g, cur):
        s_ref[1 - cur] = scores_fn(jnp.minimum(g + 1, last_chunk))
        flush(g - 1, 1 - cur)
        s = s_ref[cur]
        m = m_ref[...]
        m_new = jnp.maximum(m, jnp.max(s, axis=0, keepdims=True))
        alpha = jnp.exp2(m - m_new)
        p = jnp.exp2(s - m_new)
        l_ref[...] = alpha * l_ref[...] + jnp.sum(p, axis=0, keepdims=True)
        p_ref[cur] = p.astype(p_ref.dtype)
        m_ref[...] = m_new
        a_ref[...] = alpha

    s_ref[0] = scores_fn(0)
    p_ref[1] = jnp.zeros(p_ref.shape[1:], p_ref.dtype)
    a_ref[...] = jnp.ones(a_ref.shape, a_ref.dtype)

    def pair(k, c):
        step(2 * k, 0)
        step(2 * k + 1, 1)
        return c

    lax.fori_loop(0, n // 2, pair, 0)

    @pl.when(n % 2 == 1)
    def _():
        step(n - 1, 0)

    flush(n - 1, (n + 1) % 2)


def _softmax_scratch(dv, tq, span):
    return [
        pltpu.VMEM((1, tq), F32), pltpu.VMEM((1, tq), F32), pltpu.VMEM((dv, tq), F32), pltpu.VMEM((1, tq), F32),
        pltpu.VMEM((2, span, tq), F32), pltpu.VMEM((2, span, tq), BF16),
    ]


def _mla_body(q_ref, k_ref, vt_ref, o_ref, m_ref, l_ref, acc_ref, a_ref, s_ref, p_ref, *, tq, span):
    i = pl.program_id(1)
    q = q_ref[0]
    state = (m_ref, l_ref, acc_ref, a_ref)

    def scores(g):
        return lax.dot_general(k_ref[0, pl.ds(pl.multiple_of(g * span, span), span), :], q, _NT,
                               preferred_element_type=F32)

    def values_t(g):
        return vt_ref[0, :, pl.ds(pl.multiple_of(g * span, span), span)]

    n_full = (i * tq) // span
    last = k_ref.shape[1] // span - 1
    m_ref[...] = jnp.full(m_ref.shape, -jnp.inf, F32)
    l_ref[...] = jnp.zeros(l_ref.shape, F32)
    acc_ref[...] = jnp.zeros(acc_ref.shape, F32)
    _softmax_chunks_t(scores, values_t, n_full, last, state, s_ref, p_ref)
    key = n_full * span + lax.broadcasted_iota(jnp.int32, (span, tq), 0)
    qry = i * tq + lax.broadcasted_iota(jnp.int32, (span, tq), 1)
    m, l, acc = _softmax_step_t(jnp.where(key <= qry, scores(n_full), -jnp.inf), values_t(n_full),
                                m_ref[...], l_ref[...], acc_ref[...])
    o_ref[...] = (acc / l).T.astype(o_ref.dtype)


def _mla(q, k, vt, *, tq=256, span=1024):
    h, s, _ = q.shape
    assert s % span == 0 and span % tq == 0
    return pl.pallas_call(
        functools.partial(_mla_body, tq=tq, span=span),
        grid=(h, s // tq),
        in_specs=[
            pl.BlockSpec((1, tq, MLA_QK_PAD), lambda h, i: (h, i, 0)),
            pl.BlockSpec((1, s, MLA_QK_PAD), lambda h, i: (h, 0, 0)),
            pl.BlockSpec((1, MLA_V, s), lambda h, i: (h, 0, 0)),
        ],
        out_specs=pl.BlockSpec((tq, MLA_V), lambda h, i: (i, h)),
        out_shape=jax.ShapeDtypeStruct((s, h * MLA_V), BF16),
        scratch_shapes=_softmax_scratch(MLA_V, tq, span),
        compiler_params=_params(("arbitrary", "arbitrary")),
        name="mla",
    )(q, k, vt)


def _t5_bucket(n):
    max_exact = REL_BUCKETS // 2
    n_f = jnp.maximum(n, max_exact).astype(F32)
    large = max_exact + (jnp.log(n_f / max_exact) / math.log(REL_MAX_DIST / max_exact)
                         * (REL_BUCKETS - max_exact)).astype(jnp.int32)
    large = jnp.minimum(large, REL_BUCKETS - 1)
    return jnp.where(n < max_exact, n, large)


def _moba_body(pmin_ref, pmax_ref, q_ref, k_ref, v_ref, pq_ref, pk_ref, pmaxl_ref, tbl_ref, o_ref,
               kmean_ref, kaug_ref, vt_ref, m_ref, l_ref, acc_ref, a_ref, s_ref, p_ref, *, nblk, scale, group):
    i = pl.program_id(1)
    blk = MOBA_BLOCK
    lane_id = lax.broadcasted_iota(jnp.int32, (blk, V7X_LANES), 1)

    @pl.when(i == 0)
    def _():
        kmean_ref[...] = jnp.zeros_like(kmean_ref)

        def fill(n, c):
            rows = pl.ds(pl.multiple_of(n * blk, blk), blk)
            kb = k_ref[rows, :]
            kmean_ref[pl.ds(n, 1), :] = jnp.sum(kb.astype(F32), axis=0, keepdims=True) * (1.0 / blk)
            kaug_ref[rows, :MOBA_DH] = kb
            kaug_ref[rows, MOBA_DH:] = jnp.where(lane_id == n, 1.0, 0.0).astype(kaug_ref.dtype)
            vt_ref[:, rows] = v_ref[rows, :].astype(F32).T.astype(vt_ref.dtype)
            return c

        lax.fori_loop(0, nblk, fill, 0)

    q = q_ref[...]
    qs = (q.astype(F32) * scale).astype(q.dtype)

    gate = lax.dot_general(q.astype(F32), kmean_ref[...], _NT, preferred_element_type=F32)
    lane_f = lane_id.astype(F32)
    g = jnp.where(lane_id < i, gate, -jnp.inf)
    picked = jnp.zeros(gate.shape, F32)
    for _ in range(min(MOBA_TOPK, nblk)):
        mx = jnp.max(g, axis=1, keepdims=True)
        idx = jnp.min(jnp.where(g == mx, lane_f, float(V7X_LANES)), axis=1, keepdims=True)
        hit = lane_f == jnp.where(mx > -jnp.inf, idx, -1.0)
        picked = jnp.where(hit, 1.0, picked)
        g = jnp.where(hit, -jnp.inf, g)

    far_lanes = (pmin_ref[i] - pmaxl_ref[...]) >= BIAS_TABLE - 1
    q_any = jnp.concatenate([qs, jnp.where(picked > 0.0, 0.0, MASKED).astype(qs.dtype)], axis=1)
    q_far = jnp.concatenate([qs, jnp.where(far_lanes, jnp.where(picked > 0.0, 0.0, MASKED), MASKED).astype(qs.dtype)],
                            axis=1)

    def is_far(n):
        return (pmin_ref[i] - pmax_ref[n]) >= BIAS_TABLE - 1

    def masked_scores(off, width, q_aug):
        return lax.dot_general(kaug_ref[pl.ds(off, width), :], q_aug, _NT, preferred_element_type=F32)

    def near_bias(off):
        d = jnp.clip(pq_ref[...] - pk_ref[:, pl.ds(off, blk)], 0, BIAS_TABLE - 1)
        tb = jnp.broadcast_to(tbl_ref[0], (blk, BIAS_TABLE))
        parts = [jnp.take_along_axis(tb, d[:, c * V7X_LANES:(c + 1) * V7X_LANES], axis=1)
                 for c in range(blk // V7X_LANES)]
        return jnp.concatenate(parts, axis=1).T

    off_i = pl.multiple_of(i * blk, blk)
    key = lax.broadcasted_iota(jnp.int32, (blk, blk), 0)
    qry = lax.broadcasted_iota(jnp.int32, (blk, blk), 1)
    s = lax.dot_general(k_ref[pl.ds(off_i, blk), :], qs, _NT, preferred_element_type=F32) + near_bias(off_i)
    s = jnp.where(key <= qry, s, -jnp.inf)
    m = jnp.max(s, axis=0, keepdims=True)
    p = jnp.exp2(s - m)
    m_ref[...] = m
    l_ref[...] = jnp.sum(p, axis=0, keepdims=True)
    acc_ref[...] = jnp.dot(vt_ref[:, pl.ds(off_i, blk)], p.astype(vt_ref.dtype), preferred_element_type=F32)

    span = group * blk

    def far_scores(g):
        return masked_scores(pl.multiple_of(g * span, span), span, q_far)

    def far_values_t(g):
        return vt_ref[:, pl.ds(pl.multiple_of(g * span, span), span)]

    _softmax_chunks_t(far_scores, far_values_t, (i + group - 1) // group, nblk // group - 1,
                      (m_ref, l_ref, acc_ref, a_ref), s_ref, p_ref)

    def near_block(n, c):
        @pl.when(jnp.logical_not(is_far(n)))
        def _():
            off = pl.multiple_of(n * blk, blk)
            s = masked_scores(off, blk, q_any) + near_bias(off)
            m_ref[...], l_ref[...], acc_ref[...] = _softmax_step_t(
                s, vt_ref[:, pl.ds(off, blk)], m_ref[...], l_ref[...], acc_ref[...])

        return c

    lax.fori_loop(0, i, near_block, 0)
    o_ref[...] = (acc_ref[...] / l_ref[...]).T.astype(o_ref.dtype)


def _moba(proj, positions, rel_bias, *, group=4):
    s = proj.shape[0]
    blk = MOBA_BLOCK
    nblk = s // blk
    assert s % blk == 0 and nblk <= V7X_LANES and nblk % group == 0
    pos_blocks = positions.reshape(nblk, blk)
    pmin = jnp.min(pos_blocks, axis=1)
    pmax = jnp.max(pos_blocks, axis=1)
    table = rel_bias[_t5_bucket(jnp.arange(BIAS_TABLE, dtype=jnp.int32))]
    table = jnp.transpose(table).reshape(MOBA_HEADS, 1, BIAS_TABLE).astype(F32)
    table = (table - table[:, :, BIAS_TABLE - 1:]) * LOG2E
    pmax_lanes = jnp.pad(pmax, (0, V7X_LANES - nblk)).reshape(1, V7X_LANES)
    qc, kc, vc = (c // MOBA_DH for c in (COL_MOBA_Q, COL_MOBA_K, COL_MOBA_V))
    grid_spec = pltpu.PrefetchScalarGridSpec(
        num_scalar_prefetch=2,
        grid=(MOBA_HEADS, nblk),
        in_specs=[
            pl.BlockSpec((blk, MOBA_DH), lambda h, i, *_: (i, qc + h)),
            pl.BlockSpec((s, MOBA_DH), lambda h, i, *_: (0, kc + h)),
            pl.BlockSpec((s, MOBA_DH), lambda h, i, *_: (0, vc + h)),
            pl.BlockSpec((blk, 1), lambda h, i, *_: (i, 0)),
            pl.BlockSpec((1, s), lambda h, i, *_: (0, 0)),
            pl.BlockSpec((1, V7X_LANES), lambda h, i, *_: (0, 0)),
            pl.BlockSpec((1, 1, BIAS_TABLE), lambda h, i, *_: (h, 0, 0)),
        ],
        out_specs=pl.BlockSpec((blk, MOBA_DH), lambda h, i, *_: (i, h)),
        scratch_shapes=[
            pltpu.VMEM((V7X_LANES, MOBA_DH), F32),
            pltpu.VMEM((s, 2 * MOBA_DH), BF16),
            pltpu.VMEM((MOBA_DH, s), BF16),
        ] + _softmax_scratch(MOBA_DH, blk, group * blk),
    )
    return pl.pallas_call(
        functools.partial(_moba_body, nblk=nblk, scale=MOBA_DH ** -0.5 * LOG2E, group=group),
        grid_spec=grid_spec,
        out_shape=jax.ShapeDtypeStruct((s, MOBA_HEADS * MOBA_DH), BF16),
        compiler_params=_params(("arbitrary", "arbitrary")),
        name="moba",
    )(pmin, pmax, proj, proj, proj, positions.reshape(s, 1), positions.reshape(1, s), pmax_lanes, table)


def _merge_body(oa_ref, ob_ref, ga_ref, gb_ref, wa_ref, wb_ref, wo_ref, x_ref, o_ref):
    @pl.when(pl.program_id(1) == 0)
    def _():
        o_ref[...] = x_ref[...]

    ma = jnp.dot(oa_ref[...], wa_ref[...], preferred_element_type=F32)
    mb = jnp.dot(ob_ref[...], wb_ref[...], preferred_element_type=F32)
    mg = _sigmoid(ga_ref[...].astype(F32)) * ma + _sigmoid(gb_ref[...].astype(F32)) * mb
    o_ref[...] += jnp.dot(mg.astype(BF16), wo_ref[...], preferred_element_type=F32)


def _merge(o_a, o_b, proj, w_a, w_b, w_out, x, *, tm=512, tn=256):
    s, d = x.shape
    ka, kb = o_a.shape[1], o_b.shape[1]
    ga, gb = COL_GATE_A // tn, COL_GATE_B // tn
    return pl.pallas_call(
        _merge_body,
        grid=(s // tm, d // tn),
        in_specs=[
            pl.BlockSpec((tm, ka), lambda i, j: (i, 0)),
            pl.BlockSpec((tm, kb), lambda i, j: (i, 0)),
            pl.BlockSpec((tm, tn), lambda i, j: (i, ga + j)),
            pl.BlockSpec((tm, tn), lambda i, j: (i, gb + j)),
            pl.BlockSpec((ka, tn), lambda i, j: (0, j)),
            pl.BlockSpec((kb, tn), lambda i, j: (0, j)),
            pl.BlockSpec((tn, d), lambda i, j: (j, 0)),
            _row_resident_spec(tm, d),
        ],
        out_specs=pl.BlockSpec((tm, d), lambda i, j: (i, 0)),
        out_shape=jax.ShapeDtypeStruct((s, d), F32),
        compiler_params=_params(("arbitrary", "arbitrary")),
        name="merge",
    )(o_a, o_b, proj, proj, w_a, w_b, w_out, x)


def _proj_layout(w_in):
    d = w_in.shape[0]
    lat = MLA_Q_LORA + MLA_KV_LORA + MLA_ROPE
    pad = jnp.zeros((d, COL_MOBA_Q - lat), w_in.dtype)
    return jnp.concatenate([w_in[:, :lat], pad, w_in[:, lat:]], axis=1).astype(BF16)


def _q_layout(w_uq):
    r = w_uq.shape[0]
    w = w_uq.reshape(r, MLA_HEADS, MLA_NOPE + MLA_ROPE)
    w = jnp.pad(w, ((0, 0), (0, 0), (0, MLA_QK_PAD - MLA_NOPE - MLA_ROPE)))
    return w.reshape(r, MLA_HEADS * MLA_QK_PAD).astype(BF16)


def kernel(x, positions, rel_bias, norm_ffn1, w_ffn1_in, w_ffn1_out, norm_mix, w_in, norm_cq, w_uq, norm_ckv, w_ukv, w_br_a, w_br_b, w_out, norm_ffn2, w_ffn2_in, w_ffn2_out, norm_final):
    b, s, d = x.shape
    assert b == 1 and norm_ffn1.shape[0] == 1
    pos = positions[0]
    x0 = x[0]

    x1 = _ffn(x0, norm_ffn1[0], w_ffn1_in[0].astype(BF16), w_ffn1_out[0].astype(BF16))
    proj = _proj(x1, norm_mix[0], _proj_layout(w_in[0]))
    tables = _rope_tables(pos)
    q = _q_up(proj, norm_cq[0], _q_layout(w_uq[0]), tables)
    k, v = _kv_up(proj, norm_ckv[0], w_ukv[0].astype(BF16), tables)
    o_a = _mla(q, k, v)
    o_b = _moba(proj, pos, rel_bias)
    x2 = _merge(o_a, o_b, proj, w_br_a[0].astype(BF16), w_br_b[0].astype(BF16), w_out[0].astype(BF16), x1)
    out = _ffn(x2, norm_ffn2[0], w_ffn2_in[0].astype(BF16), w_ffn2_out[0].astype(BF16), norm_final)
    return out[None]
```

```python
import functools
import math

import jax
import jax.numpy as jnp
from jax import lax
from jax.experimental import pallas as pl
from jax.experimental.pallas import tpu as pltpu

F32 = jnp.float32
BF16 = jnp.bfloat16

RMS_EPS = 1e-6
ROPE_THETA = 10000.0
LOG2E = math.log2(math.e)
MASKED = -1e30

MLA_HEADS = 16
MLA_Q_LORA = 1024
MLA_KV_LORA = 512
MLA_NOPE = 128
MLA_ROPE = 64
MLA_V = 128
MLA_QK_PAD = 256

MOBA_HEADS = 16
MOBA_DH = 128
MOBA_BLOCK = 256
MOBA_TOPK = 3
REL_BUCKETS = 32
REL_MAX_DIST = 128
BIAS_TABLE = 128

V7X_LANES = 128
V7X_VMEM_LIMIT = 56 * 1024 * 1024

COL_CQ = 0
COL_CKV = 1024
COL_KROPE = 1536
COL_MOBA_Q = 2048
COL_MOBA_K = 4096
COL_MOBA_V = 6144
COL_GATE_A = 8192
COL_GATE_B = 12288
PROJ_COLS = 16384

_NT = (((1,), (1,)), ((), ()))


def _params(sem):
    return pltpu.CompilerParams(dimension_semantics=sem, vmem_limit_bytes=V7X_VMEM_LIMIT)


def _row_resident_spec(tm, d):
    return pl.BlockSpec((tm, d), lambda i, j: (i, 0), pipeline_mode=pl.Buffered(1))


def _rms(xf, gain):
    ms = jnp.mean(xf * xf, axis=-1, keepdims=True)
    return xf * lax.rsqrt(ms + RMS_EPS) * gain


def _sigmoid(z):
    return 1.0 / (1.0 + jnp.exp(-z))


def _ffn_body(x_ref, g_ref, wg_ref, wu_ref, wo_ref, *rest, final_norm):
    if final_norm:
        gf_ref, o_ref, h_ref = rest
    else:
        o_ref, h_ref = rest
    j = pl.program_id(1)

    @pl.when(j == 0)
    def _():
        xf = x_ref[...]
        h_ref[...] = _rms(xf, g_ref[...]).astype(BF16)
        o_ref[...] = xf

    h = h_ref[...]
    g = jnp.dot(h, wg_ref[0], preferred_element_type=F32)
    u = jnp.dot(h, wu_ref[0], preferred_element_type=F32)
    a = (g * _sigmoid(g) * (0.5 * u)).astype(BF16)
    o_ref[...] += jnp.dot(a, wo_ref[...], preferred_element_type=F32)

    if final_norm:
        @pl.when(j == pl.num_programs(1) - 1)
        def _():
            o_ref[...] = _rms(o_ref[...], gf_ref[...])


def _tile_major(w, tn):
    k, n = w.shape
    return jnp.transpose(w.reshape(k, n // tn, tn), (1, 0, 2))


def _ffn(x, gain, w_in, w_out, final_gain=None, *, tm=512, tf=256):
    s, d = x.shape
    f = w_out.shape[0]
    nf = f // tf
    assert s % tm == 0 and f % tf == 0 and w_in.shape == (d, 2 * f)
    w_in = _tile_major(w_in, tf)
    in_specs = [
        _row_resident_spec(tm, d),
        pl.BlockSpec((1, d), lambda i, j: (0, 0)),
        pl.BlockSpec((1, d, tf), lambda i, j: (j, 0, 0)),
        pl.BlockSpec((1, d, tf), lambda i, j: (j + nf, 0, 0)),
        pl.BlockSpec((tf, d), lambda i, j: (j, 0)),
    ]
    args = [x, gain.reshape(1, d), w_in, w_in, w_out]
    if final_gain is not None:
        in_specs.append(pl.BlockSpec((1, d), lambda i, j: (0, 0)))
        args.append(final_gain.reshape(1, d))
    return pl.pallas_call(
        functools.partial(_ffn_body, final_norm=final_gain is not None),
        grid=(s // tm, nf),
        in_specs=in_specs,
        out_specs=pl.BlockSpec((tm, d), lambda i, j: (i, 0)),
        out_shape=jax.ShapeDtypeStruct((s, d), F32),
        scratch_shapes=[pltpu.VMEM((tm, d), BF16)],
        compiler_params=_params(("arbitrary", "arbitrary")),
        name="ffn",
    )(*args)


def _proj_body(x_ref, g_ref, w_ref, o_ref, h_ref):
    @pl.when(pl.program_id(1) == 0)
    def _():
        h_ref[...] = _rms(x_ref[...], g_ref[...]).astype(BF16)

    o_ref[...] = jnp.dot(h_ref[...], w_ref[...], preferred_element_type=F32).astype(o_ref.dtype)


def _proj(x, gain, w, *, tm=512, tn=1024):
    s, d = x.shape
    n = w.shape[1]
    assert s % tm == 0 and n % tn == 0
    return pl.pallas_call(
        _proj_body,
        grid=(s // tm, n // tn),
        in_specs=[
            _row_resident_spec(tm, d),
            pl.BlockSpec((1, d), lambda i, j: (0, 0)),
            pl.BlockSpec((d, tn), lambda i, j: (0, j)),
        ],
        out_specs=pl.BlockSpec((tm, tn), lambda i, j: (i, j)),
        out_shape=jax.ShapeDtypeStruct((s, n), BF16),
        scratch_shapes=[pltpu.VMEM((tm, d), BF16)],
        compiler_params=_params(("arbitrary", "arbitrary")),
        name="proj",
    )(x, gain.reshape(1, d), w)


def _rope_lanes(r, cf, s1, s2):
    return r * cf + pltpu.roll(r, 96, 1) * s1 + pltpu.roll(r, 32, 1) * s2


def _rope_tables(positions):
    half = MLA_ROPE // 2
    inv_freq = ROPE_THETA ** (-jnp.arange(0, MLA_ROPE, 2, dtype=F32) / MLA_ROPE)
    ang = positions[:, None].astype(F32) * inv_freq
    cos, sin = jnp.cos(ang), jnp.sin(ang)
    z = jnp.zeros_like(cos)
    pad = jnp.zeros((positions.shape[0], V7X_LANES - MLA_ROPE), F32)
    cf = jnp.concatenate([cos, cos, pad], axis=1)
    s1 = jnp.concatenate([-sin, z, pad], axis=1)
    s2 = jnp.concatenate([z, sin, pad], axis=1)
    return cf, s1, s2


def _qup_body(c_ref, g_ref, w_ref, cf_ref, s1_ref, s2_ref, q_ref, h_ref, *, scale):
    @pl.when(pl.program_id(1) == 0)
    def _():
        h_ref[...] = _rms(c_ref[...].astype(F32), g_ref[...]).astype(BF16)

    y = jnp.dot(h_ref[...], w_ref[...], preferred_element_type=F32)
    q_ref[0, :, :MLA_NOPE] = (y[:, :MLA_NOPE] * scale).astype(q_ref.dtype)
    r = _rope_lanes(y[:, MLA_NOPE:], cf_ref[...], s1_ref[...], s2_ref[...])
    q_ref[0, :, MLA_NOPE:] = (r * scale).astype(q_ref.dtype)


def _q_up(proj, gain, w_q, tables, *, tm=1024):
    s = proj.shape[0]
    cf, s1, s2 = tables
    scale = (MLA_NOPE + MLA_ROPE) ** -0.5 * LOG2E
    tab_spec = pl.BlockSpec((tm, V7X_LANES), lambda i, h: (i, 0))
    return pl.pallas_call(
        functools.partial(_qup_body, scale=scale),
        grid=(s // tm, MLA_HEADS),
        in_specs=[
            pl.BlockSpec((tm, MLA_Q_LORA), lambda i, h: (i, COL_CQ // MLA_Q_LORA)),
            pl.BlockSpec((1, MLA_Q_LORA), lambda i, h: (0, 0)),
            pl.BlockSpec((MLA_Q_LORA, MLA_QK_PAD), lambda i, h: (0, h)),
            tab_spec, tab_spec, tab_spec,
        ],
        out_specs=pl.BlockSpec((1, tm, MLA_QK_PAD), lambda i, h: (h, i, 0)),
        out_shape=jax.ShapeDtypeStruct((MLA_HEADS, s, MLA_QK_PAD), BF16),
        scratch_shapes=[pltpu.VMEM((tm, MLA_Q_LORA), BF16)],
        compiler_params=_params(("arbitrary", "arbitrary")),
        name="q_up",
    )(proj, gain.reshape(1, MLA_Q_LORA), w_q, cf, s1, s2)


def _kvup_body(c_ref, kr_ref, g_ref, w_ref, cf_ref, s1_ref, s2_ref, k_ref, v_ref, h_ref, r_ref):
    @pl.when(pl.program_id(1) == 0)
    def _():
        h_ref[...] = _rms(c_ref[...].astype(F32), g_ref[...]).astype(BF16)
        r_ref[...] = _rope_lanes(kr_ref[...].astype(F32), cf_ref[...], s1_ref[...], s2_ref[...]).astype(BF16)

    y = jnp.dot(h_ref[...], w_ref[...], preferred_element_type=F32)
    k_ref[0, :, :MLA_NOPE] = y[:, :MLA_NOPE].astype(k_ref.dtype)
    k_ref[0, :, MLA_NOPE:] = r_ref[...]
    v_ref[0] = y[:, MLA_NOPE:].astype(v_ref.dtype)


def _kv_up(proj, gain, w_ukv, tables, *, tm=1024):
    s = proj.shape[0]
    cf, s1, s2 = tables
    tab_spec = pl.BlockSpec((tm, V7X_LANES), lambda i, h: (i, 0))
    return pl.pallas_call(
        _kvup_body,
        grid=(s // tm, MLA_HEADS),
        in_specs=[
            pl.BlockSpec((tm, MLA_KV_LORA), lambda i, h: (i, COL_CKV // MLA_KV_LORA)),
            pl.BlockSpec((tm, V7X_LANES), lambda i, h: (i, COL_KROPE // V7X_LANES)),
            pl.BlockSpec((1, MLA_KV_LORA), lambda i, h: (0, 0)),
            pl.BlockSpec((MLA_KV_LORA, MLA_NOPE + MLA_V), lambda i, h: (0, h)),
            tab_spec, tab_spec, tab_spec,
        ],
        out_specs=[
            pl.BlockSpec((1, tm, MLA_QK_PAD), lambda i, h: (h, i, 0)),
            pl.BlockSpec((1, tm, MLA_V), lambda i, h: (h, i, 0)),
        ],
        out_shape=[
            jax.ShapeDtypeStruct((MLA_HEADS, s, MLA_QK_PAD), BF16),
            jax.ShapeDtypeStruct((MLA_HEADS, s, MLA_V), BF16),
        ],
        scratch_shapes=[pltpu.VMEM((tm, MLA_KV_LORA), BF16), pltpu.VMEM((tm, V7X_LANES), BF16)],
        compiler_params=_params(("arbitrary", "arbitrary")),
        name="kv_up",
    )(proj, proj, gain.reshape(1, MLA_KV_LORA), w_ukv, cf, s1, s2)


def _pv(values, p, keys_axis):
    if keys_axis == 0:
        return jnp.dot(values, p, preferred_element_type=F32)
    return jnp.dot(p, values, preferred_element_type=F32)


def _softmax_step(s, values, m, l, acc, keys_axis):
    m_new = jnp.maximum(m, jnp.max(s, axis=keys_axis, keepdims=True))
    alpha = jnp.exp2(m - m_new)
    p = jnp.exp2(s - m_new)
    l = alpha * l + jnp.sum(p, axis=keys_axis, keepdims=True)
    acc = alpha * acc + _pv(values, p.astype(values.dtype), keys_axis)
    return m_new, l, acc


def _softmax_chunks(scores_fn, values_fn, n, last_chunk, state, s_ref, p_ref, keys_axis):
    m_ref, l_ref, acc_ref, a_ref = state

    def flush(g, slot):
        acc_ref[...] = a_ref[...] * acc_ref[...] + _pv(values_fn(jnp.maximum(g, 0)), p_ref[slot], keys_axis)

    def step(g, cur):
        s_ref[1 - cur] = scores_fn(jnp.minimum(g + 1, last_chunk))
        flush(g - 1, 1 - cur)
        s = s_ref[cur]
        m = m_ref[...]
        m_new = jnp.maximum(m, jnp.max(s, axis=keys_axis, keepdims=True))
        alpha = jnp.exp2(m - m_new)
        p = jnp.exp2(s - m_new)
        l_ref[...] = alpha * l_ref[...] + jnp.sum(p, axis=keys_axis, keepdims=True)
        p_ref[cur] = p.astype(p_ref.dtype)
        m_ref[...] = m_new
        a_ref[...] = alpha

    s_ref[0] = scores_fn(0)
    p_ref[1] = jnp.zeros(p_ref.shape[1:], p_ref.dtype)
    a_ref[...] = jnp.ones(a_ref.shape, a_ref.dtype)

    def pair(k, c):
        step(2 * k, 0)
        step(2 * k + 1, 1)
        return c

    lax.fori_loop(0, n // 2, pair, 0)

    @pl.when(n % 2 == 1)
    def _():
        step(n - 1, 0)

    flush(n - 1, (n + 1) % 2)


def _softmax_scratch(dv, tq, span, keys_axis):
    stat, acc, tile = ((1, tq), (dv, tq), (span, tq)) if keys_axis == 0 else ((tq, 1), (tq, dv), (tq, span))
    return [pltpu.VMEM(stat, F32), pltpu.VMEM(stat, F32), pltpu.VMEM(acc, F32), pltpu.VMEM(stat, F32),
            pltpu.VMEM((2,) + tile, F32), pltpu.VMEM((2,) + tile, BF16)]


def _mla_body(q_ref, k_ref, v_ref, o_ref, m_ref, l_ref, acc_ref, a_ref, s_ref, p_ref, *, t):
    i = pl.program_id(1)
    q = q_ref[0]

    def scores(n):
        return lax.dot_general(q, k_ref[0, pl.ds(pl.multiple_of(n * t, t), t), :], _NT, preferred_element_type=F32)

    def values(n):
        return v_ref[0, pl.ds(pl.multiple_of(n * t, t), t), :]

    m_ref[...] = jnp.full(m_ref.shape, -jnp.inf, F32)
    l_ref[...] = jnp.zeros(l_ref.shape, F32)
    acc_ref[...] = jnp.zeros(acc_ref.shape, F32)
    _softmax_chunks(scores, values, i, i, (m_ref, l_ref, acc_ref, a_ref), s_ref, p_ref, keys_axis=1)
    row = lax.broadcasted_iota(jnp.int32, (t, t), 0)
    col = lax.broadcasted_iota(jnp.int32, (t, t), 1)
    m, l, acc = _softmax_step(jnp.where(col <= row, scores(i), -jnp.inf), values(i),
                              m_ref[...], l_ref[...], acc_ref[...], keys_axis=1)
    o_ref[...] = (acc / l).astype(o_ref.dtype)


def _mla(q, k, v, *, t=1024):
    h, s, _ = q.shape
    assert s % t == 0
    return pl.pallas_call(
        functools.partial(_mla_body, t=t),
        grid=(h, s // t),
        in_specs=[
            pl.BlockSpec((1, t, MLA_QK_PAD), lambda h, i: (h, i, 0)),
            pl.BlockSpec((1, s, MLA_QK_PAD), lambda h, i: (h, 0, 0)),
            pl.BlockSpec((1, s, MLA_V), lambda h, i: (h, 0, 0)),
        ],
        out_specs=pl.BlockSpec((t, MLA_V), lambda h, i: (i, h)),
        out_shape=jax.ShapeDtypeStruct((s, h * MLA_V), BF16),
        scratch_shapes=_softmax_scratch(MLA_V, t, t, keys_axis=1),
        compiler_params=_params(("arbitrary", "arbitrary")),
        name="mla",
    )(q, k, v)


def _t5_bucket(n):
    max_exact = REL_BUCKETS // 2
    n_f = jnp.maximum(n, max_exact).astype(F32)
    large = max_exact + (jnp.log(n_f / max_exact) / math.log(REL_MAX_DIST / max_exact)
                         * (REL_BUCKETS - max_exact)).astype(jnp.int32)
    large = jnp.minimum(large, REL_BUCKETS - 1)
    return jnp.where(n < max_exact, n, large)


def _moba_body(pmin_ref, pmax_ref, q_ref, k_ref, v_ref, pq_ref, pk_ref, pmaxl_ref, tbl_ref, o_ref,
               kmean_ref, kaug_ref, vt_ref, m_ref, l_ref, acc_ref, a_ref, s_ref, p_ref, *, nblk, scale, group):
    i = pl.program_id(1)
    blk = MOBA_BLOCK
    lane_id = lax.broadcasted_iota(jnp.int32, (blk, V7X_LANES), 1)

    @pl.when(i == 0)
    def _():
        kmean_ref[...] = jnp.zeros_like(kmean_ref)

        def fill(n, c):
            rows = pl.ds(pl.multiple_of(n * blk, blk), blk)
            kb = k_ref[rows, :]
            kmean_ref[pl.ds(n, 1), :] = jnp.sum(kb.astype(F32), axis=0, keepdims=True) * (1.0 / blk)
            kaug_ref[rows, :MOBA_DH] = kb
            kaug_ref[rows, MOBA_DH:] = jnp.where(lane_id == n, 1.0, 0.0).astype(kaug_ref.dtype)
            vt_ref[:, rows] = v_ref[rows, :].astype(F32).T.astype(vt_ref.dtype)
            return c

        lax.fori_loop(0, nblk, fill, 0)

    q = q_ref[...]
    qs = (q.astype(F32) * scale).astype(q.dtype)

    gate = lax.dot_general(q.astype(F32), kmean_ref[...], _NT, preferred_element_type=F32)
    lane_f = lane_id.astype(F32)
    g = jnp.where(lane_id < i, gate, -jnp.inf)
    picked = jnp.zeros(gate.shape, F32)
    for _ in range(min(MOBA_TOPK, nblk)):
        mx = jnp.max(g, axis=1, keepdims=True)
        idx = jnp.min(jnp.where(g == mx, lane_f, float(V7X_LANES)), axis=1, keepdims=True)
        hit = lane_f == jnp.where(mx > -jnp.inf, idx, -1.0)
        picked = jnp.where(hit, 1.0, picked)
        g = jnp.where(hit, -jnp.inf, g)

    far_lanes = (pmin_ref[i] - pmaxl_ref[...]) >= BIAS_TABLE - 1
    q_any = jnp.concatenate([qs, jnp.where(picked > 0.0, 0.0, MASKED).astype(qs.dtype)], axis=1)
    q_far = jnp.concatenate([qs, jnp.where(far_lanes, jnp.where(picked > 0.0, 0.0, MASKED), MASKED).astype(qs.dtype)],
                            axis=1)

    def is_far(n):
        return (pmin_ref[i] - pmax_ref[n]) >= BIAS_TABLE - 1

    def masked_scores(off, width, q_aug):
        return lax.dot_general(kaug_ref[pl.ds(off, width), :], q_aug, _NT, preferred_element_type=F32)

    def near_bias(off):
        d = jnp.clip(pq_ref[...] - pk_ref[:, pl.ds(off, blk)], 0, BIAS_TABLE - 1)
        tb = jnp.broadcast_to(tbl_ref[0], (blk, BIAS_TABLE))
        parts = [jnp.take_along_axis(tb, d[:, c * V7X_LANES:(c + 1) * V7X_LANES], axis=1)
                 for c in range(blk // V7X_LANES)]
        return jnp.concatenate(parts, axis=1).T

    off_i = pl.multiple_of(i * blk, blk)
    key = lax.broadcasted_iota(jnp.int32, (blk, blk), 0)
    qry = lax.broadcasted_iota(jnp.int32, (blk, blk), 1)
    s = lax.dot_general(k_ref[pl.ds(off_i, blk), :], qs, _NT, preferred_element_type=F32) + near_bias(off_i)
    s = jnp.where(key <= qry, s, -jnp.inf)
    m = jnp.max(s, axis=0, keepdims=True)
    p = jnp.exp2(s - m)
    m_ref[...] = m
    l_ref[...] = jnp.sum(p, axis=0, keepdims=True)
    acc_ref[...] = jnp.dot(vt_ref[:, pl.ds(off_i, blk)], p.astype(vt_ref.dtype), preferred_element_type=F32)

    span = group * blk

    def far_scores(g):
        return masked_scores(pl.multiple_of(g * span, span), span, q_far)

    def far_values_t(g):
        return vt_ref[:, pl.ds(pl.multiple_of(g * span, span), span)]

    _softmax_chunks(far_scores, far_values_t, (i + group - 1) // group, nblk // group - 1,
                    (m_ref, l_ref, acc_ref, a_ref), s_ref, p_ref, keys_axis=0)

    def near_block(n, c):
        @pl.when(jnp.logical_not(is_far(n)))
        def _():
            off = pl.multiple_of(n * blk, blk)
            s = masked_scores(off, blk, q_any) + near_bias(off)
            m_ref[...], l_ref[...], acc_ref[...] = _softmax_step(
                s, vt_ref[:, pl.ds(off, blk)], m_ref[...], l_ref[...], acc_ref[...], keys_axis=0)

        return c

    lax.fori_loop(0, i, near_block, 0)
    o_ref[...] = (acc_ref[...] / l_ref[...]).T.astype(o_ref.dtype)


def _moba(proj, positions, rel_bias, *, group=4):
    s = proj.shape[0]
    blk = MOBA_BLOCK
    nblk = s // blk
    assert s % blk == 0 and nblk <= V7X_LANES and nblk % group == 0
    pos_blocks = positions.reshape(nblk, blk)
    pmin = jnp.min(pos_blocks, axis=1)
    pmax = jnp.max(pos_blocks, axis=1)
    table = rel_bias[_t5_bucket(jnp.arange(BIAS_TABLE, dtype=jnp.int32))]
    table = jnp.transpose(table).reshape(MOBA_HEADS, 1, BIAS_TABLE).astype(F32)
    table = (table - table[:, :, BIAS_TABLE - 1:]) * LOG2E
    pmax_lanes = jnp.pad(pmax, (0, V7X_LANES - nblk)).reshape(1, V7X_LANES)
    qc, kc, vc = (c // MOBA_DH for c in (COL_MOBA_Q, COL_MOBA_K, COL_MOBA_V))
    grid_spec = pltpu.PrefetchScalarGridSpec(
        num_scalar_prefetch=2,
        grid=(MOBA_HEADS, nblk),
        in_specs=[
            pl.BlockSpec((blk, MOBA_DH), lambda h, i, *_: (i, qc + h)),
            pl.BlockSpec((s, MOBA_DH), lambda h, i, *_: (0, kc + h)),
            pl.BlockSpec((s, MOBA_DH), lambda h, i, *_: (0, vc + h)),
            pl.BlockSpec((blk, 1), lambda h, i, *_: (i, 0)),
            pl.BlockSpec((1, s), lambda h, i, *_: (0, 0)),
            pl.BlockSpec((1, V7X_LANES), lambda h, i, *_: (0, 0)),
            pl.BlockSpec((1, 1, BIAS_TABLE), lambda h, i, *_: (h, 0, 0)),
        ],
        out_specs=pl.BlockSpec((blk, MOBA_DH), lambda h, i, *_: (i, h)),
        scratch_shapes=[
            pltpu.VMEM((V7X_LANES, MOBA_DH), F32),
            pltpu.VMEM((s, 2 * MOBA_DH), BF16),
            pltpu.VMEM((MOBA_DH, s), BF16),
        ] + _softmax_scratch(MOBA_DH, blk, group * blk, keys_axis=0),
    )
    return pl.pallas_call(
        functools.partial(_moba_body, nblk=nblk, scale=MOBA_DH ** -0.5 * LOG2E, group=group),
        grid_spec=grid_spec,
        out_shape=jax.ShapeDtypeStruct((s, MOBA_HEADS * MOBA_DH), BF16),
        compiler_params=_params(("arbitrary", "arbitrary")),
        name="moba",
    )(pmin, pmax, proj, proj, proj, positions.reshape(s, 1), positions.reshape(1, s), pmax_lanes, table)


def _merge_body(oa_ref, ob_ref, ga_ref, gb_ref, wa_ref, wb_ref, wo_ref, x_ref, o_ref):
    @pl.when(pl.program_id(1) == 0)
    def _():
        o_ref[...] = x_ref[...]

    ma = jnp.dot(oa_ref[...], wa_ref[0], preferred_element_type=F32)
    mb = jnp.dot(ob_ref[...], wb_ref[0], preferred_element_type=F32)
    mg = _sigmoid(ga_ref[...].astype(F32)) * ma + _sigmoid(gb_ref[...].astype(F32)) * mb
    o_ref[...] += jnp.dot(mg.astype(BF16), wo_ref[...], preferred_element_type=F32)


def _merge(o_a, o_b, proj, w_a, w_b, w_out, x, *, tm=512, tn=512):
    s, d = x.shape
    ka, kb = o_a.shape[1], o_b.shape[1]
    ga, gb = COL_GATE_A // tn, COL_GATE_B // tn
    w_a, w_b = _tile_major(w_a, tn), _tile_major(w_b, tn)
    return pl.pallas_call(
        _merge_body,
        grid=(s // tm, d // tn),
        in_specs=[
            pl.BlockSpec((tm, ka), lambda i, j: (i, 0)),
            pl.BlockSpec((tm, kb), lambda i, j: (i, 0)),
            pl.BlockSpec((tm, tn), lambda i, j: (i, ga + j)),
            pl.BlockSpec((tm, tn), lambda i, j: (i, gb + j)),
            pl.BlockSpec((1, ka, tn), lambda i, j: (j, 0, 0)),
            pl.BlockSpec((1, kb, tn), lambda i, j: (j, 0, 0)),
            pl.BlockSpec((tn, d), lambda i, j: (j, 0)),
            _row_resident_spec(tm, d),
        ],
        out_specs=pl.BlockSpec((tm, d), lambda i, j: (i, 0)),
        out_shape=jax.ShapeDtypeStruct((s, d), F32),
        compiler_params=_params(("arbitrary", "arbitrary")),
        name="merge",
    )(o_a, o_b, proj, proj, w_a, w_b, w_out, x)


def _proj_layout(w_in):
    d = w_in.shape[0]
    lat = MLA_Q_LORA + MLA_KV_LORA + MLA_ROPE
    pad = jnp.zeros((d, COL_MOBA_Q - lat), BF16)
    return jnp.concatenate([w_in[:, :lat].astype(BF16), pad, w_in[:, lat:].astype(BF16)], axis=1)


def _q_layout(w_uq):
    r = w_uq.shape[0]
    w = w_uq.reshape(r, MLA_HEADS, MLA_NOPE + MLA_ROPE)
    w = jnp.pad(w, ((0, 0), (0, 0), (0, MLA_QK_PAD - MLA_NOPE - MLA_ROPE)))
    return w.reshape(r, MLA_HEADS * MLA_QK_PAD).astype(BF16)


def kernel(x, positions, rel_bias, norm_ffn1, w_ffn1_in, w_ffn1_out, norm_mix, w_in, norm_cq, w_uq, norm_ckv, w_ukv, w_br_a, w_br_b, w_out, norm_ffn2, w_ffn2_in, w_ffn2_out, norm_final):
    b, s, d = x.shape
    assert b == 1 and norm_ffn1.shape[0] == 1
    pos = positions[0]
    x0 = x[0]

    x1 = _ffn(x0, norm_ffn1[0], w_ffn1_in[0].astype(BF16), w_ffn1_out[0].astype(BF16))
    proj = _proj(x1, norm_mix[0], _proj_layout(w_in[0]))
    tables = _rope_tables(pos)
    q = _q_up(proj, norm_cq[0], _q_layout(w_uq[0]), tables)
    k, v = _kv_up(proj, norm_ckv[0], w_ukv[0].astype(BF16), tables)
    o_a = _mla(q, k, v)
    o_b = _moba(proj, pos, rel_bias)
    x2 = _merge(o_a, o_b, proj, w_br_a[0].astype(BF16), w_br_b[0].astype(BF16), w_out[0].astype(BF16), x1)
    out = _ffn(x2, norm_ffn2[0], w_ffn2_in[0].astype(BF16), w_ffn2_out[0].astype(BF16), norm_final)
    return out[None]
```

```python
import functools
import math

import jax
import jax.numpy as jnp
from jax import lax
from jax.experimental import pallas as pl
from jax.experimental.pallas import tpu as pltpu

F32 = jnp.float32
BF16 = jnp.bfloat16

RMS_EPS = 1e-6
ROPE_THETA = 10000.0
LOG2E = math.log2(math.e)
MASKED = -1e30

MLA_HEADS = 16
MLA_Q_LORA = 1024
MLA_KV_LORA = 512
MLA_NOPE = 128
MLA_ROPE = 64
MLA_V = 128
MLA_QK_PAD = 256

MOBA_HEADS = 16
MOBA_DH = 128
MOBA_BLOCK = 256
MOBA_TOPK = 3
REL_BUCKETS = 32
REL_MAX_DIST = 128
BIAS_TABLE = 128

V7X_LANES = 128
V7X_VMEM_LIMIT = 56 * 1024 * 1024

LATENT_REAL = MLA_Q_LORA + MLA_KV_LORA + MLA_ROPE
LATENT_COLS = 2048
COL_CQ = 0
COL_CKV = 1024
COL_KROPE = 1536
COL_MOBA_Q = 0
COL_MOBA_K = 2048
COL_MOBA_V = 4096
COL_GATE_A = 6144
COL_GATE_B = 10240

_NT = (((1,), (1,)), ((), ()))


def _params(sem):
    return pltpu.CompilerParams(dimension_semantics=sem, vmem_limit_bytes=V7X_VMEM_LIMIT)


def _row_resident_spec(tm, d):
    return pl.BlockSpec((tm, d), lambda i, j: (i, 0), pipeline_mode=pl.Buffered(1))


def _rms(xf, gain):
    ms = jnp.mean(xf * xf, axis=-1, keepdims=True)
    return xf * lax.rsqrt(ms + RMS_EPS) * gain


def _sigmoid(z):
    return 1.0 / (1.0 + jnp.exp(-z))


def _ffn_body(x_ref, g_ref, wg_ref, wu_ref, wo_ref, *rest, final_norm):
    if final_norm:
        gf_ref, o_ref, h_ref = rest
    else:
        o_ref, h_ref = rest
    j = pl.program_id(1)

    @pl.when(j == 0)
    def _():
        xf = x_ref[...]
        h_ref[...] = _rms(xf, g_ref[...]).astype(BF16)
        o_ref[...] = xf

    h = h_ref[...]
    g = jnp.dot(h, wg_ref[...], preferred_element_type=F32)
    u = jnp.dot(h, wu_ref[...], preferred_element_type=F32)
    a = (g * _sigmoid(g) * (0.5 * u)).astype(BF16)
    o_ref[...] += jnp.dot(a, wo_ref[...], preferred_element_type=F32)

    if final_norm:
        @pl.when(j == pl.num_programs(1) - 1)
        def _():
            o_ref[...] = _rms(o_ref[...], gf_ref[...])


def _ffn(x, gain, w_in, w_out, final_gain=None, *, tm=512, tf=256):
    s, d = x.shape
    f = w_out.shape[0]
    nf = f // tf
    assert s % tm == 0 and f % tf == 0 and w_in.shape == (d, 2 * f)
    in_specs = [
        _row_resident_spec(tm, d),
        pl.BlockSpec((1, d), lambda i, j: (0, 0)),
        pl.BlockSpec((d, tf), lambda i, j: (0, j)),
        pl.BlockSpec((d, tf), lambda i, j: (0, j + nf)),
        pl.BlockSpec((tf, d), lambda i, j: (j, 0)),
    ]
    args = [x, gain.reshape(1, d), w_in, w_in, w_out]
    if final_gain is not None:
        in_specs.append(pl.BlockSpec((1, d), lambda i, j: (0, 0)))
        args.append(final_gain.reshape(1, d))
    return pl.pallas_call(
        functools.partial(_ffn_body, final_norm=final_gain is not None),
        grid=(s // tm, nf),
        in_specs=in_specs,
        out_specs=pl.BlockSpec((tm, d), lambda i, j: (i, 0)),
        out_shape=jax.ShapeDtypeStruct((s, d), F32),
        scratch_shapes=[pltpu.VMEM((tm, d), BF16)],
        compiler_params=_params(("arbitrary", "arbitrary")),
        name="ffn",
    )(*args)


def _proj_body(x_ref, g_ref, w_ref, o_ref, h_ref):
    @pl.when(pl.program_id(1) == 0)
    def _():
        h_ref[...] = _rms(x_ref[...], g_ref[...]).astype(BF16)

    o_ref[...] = jnp.dot(h_ref[...], w_ref[...], preferred_element_type=F32).astype(o_ref.dtype)


def _proj(x, gain, w, *, tm=512, tn=1024):
    s, d = x.shape
    n = w.shape[1]
    assert s % tm == 0 and n % tn == 0
    return pl.pallas_call(
        _proj_body,
        grid=(s // tm, n // tn),
        in_specs=[
            _row_resident_spec(tm, d),
            pl.BlockSpec((1, d), lambda i, j: (0, 0)),
            pl.BlockSpec((d, tn), lambda i, j: (0, j)),
        ],
        out_specs=pl.BlockSpec((tm, tn), lambda i, j: (i, j)),
        out_shape=jax.ShapeDtypeStruct((s, n), BF16),
        scratch_shapes=[pltpu.VMEM((tm, d), BF16)],
        compiler_params=_params(("arbitrary", "arbitrary")),
        name="proj",
    )(x, gain.reshape(1, d), w)


def _rope_lanes(r, cf, s1, s2):
    return r * cf + pltpu.roll(r, 96, 1) * s1 + pltpu.roll(r, 32, 1) * s2


def _rope_tables(positions):
    half = MLA_ROPE // 2
    inv_freq = ROPE_THETA ** (-jnp.arange(0, MLA_ROPE, 2, dtype=F32) / MLA_ROPE)
    ang = positions[:, None].astype(F32) * inv_freq
    cos, sin = jnp.cos(ang), jnp.sin(ang)
    z = jnp.zeros_like(cos)
    pad = jnp.zeros((positions.shape[0], V7X_LANES - MLA_ROPE), F32)
    cf = jnp.concatenate([cos, cos, pad], axis=1)
    s1 = jnp.concatenate([-sin, z, pad], axis=1)
    s2 = jnp.concatenate([z, sin, pad], axis=1)
    return cf, s1, s2


def _qup_body(c_ref, g_ref, w_ref, cf_ref, s1_ref, s2_ref, q_ref, h_ref, *, scale):
    @pl.when(pl.program_id(1) == 0)
    def _():
        h_ref[...] = _rms(c_ref[...].astype(F32), g_ref[...]).astype(BF16)

    y = jnp.dot(h_ref[...], w_ref[...], preferred_element_type=F32)
    q_ref[0, :, :MLA_NOPE] = (y[:, :MLA_NOPE] * scale).astype(q_ref.dtype)
    r = _rope_lanes(y[:, MLA_NOPE:], cf_ref[...], s1_ref[...], s2_ref[...])
    q_ref[0, :, MLA_NOPE:] = (r * scale).astype(q_ref.dtype)


def _q_up(proj, gain, w_q, tables, *, tm=1024):
    s = proj.shape[0]
    cf, s1, s2 = tables
    scale = (MLA_NOPE + MLA_ROPE) ** -0.5 * LOG2E
    tab_spec = pl.BlockSpec((tm, V7X_LANES), lambda i, h: (i, 0))
    return pl.pallas_call(
        functools.partial(_qup_body, scale=scale),
        grid=(s // tm, MLA_HEADS),
        in_specs=[
            pl.BlockSpec((tm, MLA_Q_LORA), lambda i, h: (i, COL_CQ // MLA_Q_LORA)),
            pl.BlockSpec((1, MLA_Q_LORA), lambda i, h: (0, 0)),
            pl.BlockSpec((MLA_Q_LORA, MLA_QK_PAD), lambda i, h: (0, h)),
            tab_spec, tab_spec, tab_spec,
        ],
        out_specs=pl.BlockSpec((1, tm, MLA_QK_PAD), lambda i, h: (h, i, 0)),
        out_shape=jax.ShapeDtypeStruct((MLA_HEADS, s, MLA_QK_PAD), BF16),
        scratch_shapes=[pltpu.VMEM((tm, MLA_Q_LORA), BF16)],
        compiler_params=_params(("arbitrary", "arbitrary")),
        name="q_up",
    )(proj, gain.reshape(1, MLA_Q_LORA), w_q, cf, s1, s2)


def _kvup_body(c_ref, kr_ref, g_ref, w_ref, cf_ref, s1_ref, s2_ref, k_ref, v_ref, h_ref, r_ref):
    @pl.when(pl.program_id(1) == 0)
    def _():
        h_ref[...] = _rms(c_ref[...].astype(F32), g_ref[...]).astype(BF16)
        r_ref[...] = _rope_lanes(kr_ref[...].astype(F32), cf_ref[...], s1_ref[...], s2_ref[...]).astype(BF16)

    y = jnp.dot(h_ref[...], w_ref[...], preferred_element_type=F32)
    k_ref[0, :, :MLA_NOPE] = y[:, :MLA_NOPE].astype(k_ref.dtype)
    k_ref[0, :, MLA_NOPE:] = r_ref[...]
    v_ref[0] = y[:, MLA_NOPE:].astype(v_ref.dtype)


def _kv_up(proj, gain, w_ukv, tables, *, tm=1024):
    s = proj.shape[0]
    cf, s1, s2 = tables
    tab_spec = pl.BlockSpec((tm, V7X_LANES), lambda i, h: (i, 0))
    return pl.pallas_call(
        _kvup_body,
        grid=(s // tm, MLA_HEADS),
        in_specs=[
            pl.BlockSpec((tm, MLA_KV_LORA), lambda i, h: (i, COL_CKV // MLA_KV_LORA)),
            pl.BlockSpec((tm, V7X_LANES), lambda i, h: (i, COL_KROPE // V7X_LANES)),
            pl.BlockSpec((1, MLA_KV_LORA), lambda i, h: (0, 0)),
            pl.BlockSpec((MLA_KV_LORA, MLA_NOPE + MLA_V), lambda i, h: (0, h)),
            tab_spec, tab_spec, tab_spec,
        ],
        out_specs=[
            pl.BlockSpec((1, tm, MLA_QK_PAD), lambda i, h: (h, i, 0)),
            pl.BlockSpec((1, tm, MLA_V), lambda i, h: (h, i, 0)),
        ],
        out_shape=[
            jax.ShapeDtypeStruct((MLA_HEADS, s, MLA_QK_PAD), BF16),
            jax.ShapeDtypeStruct((MLA_HEADS, s, MLA_V), BF16),
        ],
        scratch_shapes=[pltpu.VMEM((tm, MLA_KV_LORA), BF16), pltpu.VMEM((tm, V7X_LANES), BF16)],
        compiler_params=_params(("arbitrary", "arbitrary")),
        name="kv_up",
    )(proj, proj, gain.reshape(1, MLA_KV_LORA), w_ukv, cf, s1, s2)


def _pv(values, p, keys_axis):
    if keys_axis == 0:
        return jnp.dot(values, p, preferred_element_type=F32)
    return jnp.dot(p, values, preferred_element_type=F32)


def _softmax_step(s, values, m, l, acc, keys_axis):
    m_new = jnp.maximum(m, jnp.max(s, axis=keys_axis, keepdims=True))
    alpha = jnp.exp2(m - m_new)
    p = jnp.exp2(s - m_new)
    l = alpha * l + jnp.sum(p, axis=keys_axis, keepdims=True)
    acc = alpha * acc + _pv(values, p.astype(values.dtype), keys_axis)
    return m_new, l, acc


def _softmax_chunks(scores_fn, values_fn, n, last_chunk, state, s_ref, p_ref, keys_axis):
    m_ref, l_ref, acc_ref, a_ref = state

    def flush(g, slot):
        acc_ref[...] = a_ref[...] * acc_ref[...] + _pv(values_fn(jnp.maximum(g, 0)), p_ref[slot], keys_axis)

    def step(g, cur):
        s_ref[1 - cur] = scores_fn(jnp.minimum(g + 1, last_chunk))
        flush(g - 1, 1 - cur)
        s = s_ref[cur]
        m = m_ref[...]
        m_new = jnp.maximum(m, jnp.max(s, axis=keys_axis, keepdims=True))
        alpha = jnp.exp2(m - m_new)
        p = jnp.exp2(s - m_new)
        l_ref[...] = alpha * l_ref[...] + jnp.sum(p, axis=keys_axis, keepdims=True)
        p_ref[cur] = p.astype(p_ref.dtype)
        m_ref[...] = m_new
        a_ref[...] = alpha

    s_ref[0] = scores_fn(0)
    p_ref[1] = jnp.zeros(p_ref.shape[1:], p_ref.dtype)
    a_ref[...] = jnp.ones(a_ref.shape, a_ref.dtype)

    def pair(k, c):
        step(2 * k, 0)
        step(2 * k + 1, 1)
        return c

    lax.fori_loop(0, n // 2, pair, 0)

    @pl.when(n % 2 == 1)
    def _():
        step(n - 1, 0)

    flush(n - 1, (n + 1) % 2)


def _softmax_scratch(dv, tq, span, keys_axis):
    stat, acc, tile = ((1, tq), (dv, tq), (span, tq)) if keys_axis == 0 else ((tq, 1), (tq, dv), (tq, span))
    return [pltpu.VMEM(stat, F32), pltpu.VMEM(stat, F32), pltpu.VMEM(acc, F32), pltpu.VMEM(stat, F32),
            pltpu.VMEM((2,) + tile, F32), pltpu.VMEM((2,) + tile, BF16)]


def _mla_body(q_ref, k_ref, v_ref, o_ref, *, t):
    i = pl.program_id(1)
    q = q_ref[0]

    def scores(n):
        return lax.dot_general(q, k_ref[0, pl.ds(pl.multiple_of(n * t, t), t), :], _NT, preferred_element_type=F32)

    def values(n):
        return v_ref[0, pl.ds(pl.multiple_of(n * t, t), t), :]

    def body(n, c):
        return _softmax_step(scores(n), values(n), *c, keys_axis=1)

    init = (jnp.full((t, 1), -jnp.inf, F32), jnp.zeros((t, 1), F32), jnp.zeros((t, MLA_V), F32))
    m, l, acc = lax.fori_loop(0, i, body, init)
    row = lax.broadcasted_iota(jnp.int32, (t, t), 0)
    col = lax.broadcasted_iota(jnp.int32, (t, t), 1)
    m, l, acc = _softmax_step(jnp.where(col <= row, scores(i), -jnp.inf), values(i), m, l, acc, keys_axis=1)
    o_ref[...] = (acc / l).astype(o_ref.dtype)


def _mla(q, k, v, *, t=1024):
    h, s, _ = q.shape
    assert s % t == 0
    return pl.pallas_call(
        functools.partial(_mla_body, t=t),
        grid=(h, s // t),
        in_specs=[
            pl.BlockSpec((1, t, MLA_QK_PAD), lambda h, i: (h, i, 0)),
            pl.BlockSpec((1, s, MLA_QK_PAD), lambda h, i: (h, 0, 0)),
            pl.BlockSpec((1, s, MLA_V), lambda h, i: (h, 0, 0)),
        ],
        out_specs=pl.BlockSpec((t, MLA_V), lambda h, i: (i, h)),
        out_shape=jax.ShapeDtypeStruct((s, h * MLA_V), BF16),
        compiler_params=_params(("arbitrary", "arbitrary")),
        name="mla",
    )(q, k, v)


def _t5_bucket(n):
    max_exact = REL_BUCKETS // 2
    n_f = jnp.maximum(n, max_exact).astype(F32)
    large = max_exact + (jnp.log(n_f / max_exact) / math.log(REL_MAX_DIST / max_exact)
                         * (REL_BUCKETS - max_exact)).astype(jnp.int32)
    large = jnp.minimum(large, REL_BUCKETS - 1)
    return jnp.where(n < max_exact, n, large)


def _moba_body(pmin_ref, pmax_ref, q_ref, k_ref, v_ref, pq_ref, pk_ref, pmaxl_ref, tbl_ref, o_ref,
               kmean_ref, kaug_ref, vt_ref, m_ref, l_ref, acc_ref, a_ref, s_ref, p_ref, *, nblk, scale, group):
    i = pl.program_id(1)
    blk = MOBA_BLOCK
    lane_id = lax.broadcasted_iota(jnp.int32, (blk, V7X_LANES), 1)

    @pl.when(i == 0)
    def _():
        kmean_ref[...] = jnp.zeros_like(kmean_ref)

        def fill(n, c):
            rows = pl.ds(pl.multiple_of(n * blk, blk), blk)
            kb = k_ref[rows, :]
            kmean_ref[pl.ds(n, 1), :] = jnp.sum(kb.astype(F32), axis=0, keepdims=True) * (1.0 / blk)
            kaug_ref[rows, :MOBA_DH] = kb
            kaug_ref[rows, MOBA_DH:] = jnp.where(lane_id == n, 1.0, 0.0).astype(kaug_ref.dtype)
            vt_ref[:, rows] = v_ref[rows, :].astype(F32).T.astype(vt_ref.dtype)
            return c

        lax.fori_loop(0, nblk, fill, 0)

    q = q_ref[...]
    qs = (q.astype(F32) * scale).astype(q.dtype)

    gate = lax.dot_general(q.astype(F32), kmean_ref[...], _NT, preferred_element_type=F32)
    lane_f = lane_id.astype(F32)
    g = jnp.where(lane_id < i, gate, -jnp.inf)
    picked = jnp.zeros(gate.shape, F32)
    for _ in range(min(MOBA_TOPK, nblk)):
        mx = jnp.max(g, axis=1, keepdims=True)
        idx = jnp.min(jnp.where(g == mx, lane_f, float(V7X_LANES)), axis=1, keepdims=True)
        hit = lane_f == jnp.where(mx > -jnp.inf, idx, -1.0)
        picked = jnp.where(hit, 1.0, picked)
        g = jnp.where(hit, -jnp.inf, g)

    far_lanes = (pmin_ref[i] - pmaxl_ref[...]) >= BIAS_TABLE - 1
    q_any = jnp.concatenate([qs, jnp.where(picked > 0.0, 0.0, MASKED).astype(qs.dtype)], axis=1)
    q_far = jnp.concatenate([qs, jnp.where(far_lanes, jnp.where(picked > 0.0, 0.0, MASKED), MASKED).astype(qs.dtype)],
                            axis=1)

    def is_far(n):
        return (pmin_ref[i] - pmax_ref[n]) >= BIAS_TABLE - 1

    def masked_scores(off, width, q_aug):
        return lax.dot_general(kaug_ref[pl.ds(off, width), :], q_aug, _NT, preferred_element_type=F32)

    def near_bias(off):
        d = jnp.clip(pq_ref[...] - pk_ref[:, pl.ds(off, blk)], 0, BIAS_TABLE - 1)
        tb = jnp.broadcast_to(tbl_ref[0], (blk, BIAS_TABLE))
        parts = [jnp.take_along_axis(tb, d[:, c * V7X_LANES:(c + 1) * V7X_LANES], axis=1)
                 for c in range(blk // V7X_LANES)]
        return jnp.concatenate(parts, axis=1).T

    off_i = pl.multiple_of(i * blk, blk)
    key = lax.broadcasted_iota(jnp.int32, (blk, blk), 0)
    qry = lax.broadcasted_iota(jnp.int32, (blk, blk), 1)
    s = lax.dot_general(k_ref[pl.ds(off_i, blk), :], qs, _NT, preferred_element_type=F32) + near_bias(off_i)
    s = jnp.where(key <= qry, s, -jnp.inf)
    m = jnp.max(s, axis=0, keepdims=True)
    p = jnp.exp2(s - m)
    m_ref[...] = m
    l_ref[...] = jnp.sum(p, axis=0, keepdims=True)
    acc_ref[...] = jnp.dot(vt_ref[:, pl.ds(off_i, blk)], p.astype(vt_ref.dtype), preferred_element_type=F32)

    span = group * blk

    def far_scores(g):
        return masked_scores(pl.multiple_of(g * span, span), span, q_far)

    def far_values_t(g):
        return vt_ref[:, pl.ds(pl.multiple_of(g * span, span), span)]

    _softmax_chunks(far_scores, far_values_t, (i + group - 1) // group, nblk // group - 1,
                    (m_ref, l_ref, acc_ref, a_ref), s_ref, p_ref, keys_axis=0)

    def near_block(n, c):
        @pl.when(jnp.logical_not(is_far(n)))
        def _():
            off = pl.multiple_of(n * blk, blk)
            s = masked_scores(off, blk, q_any) + near_bias(off)
            m_ref[...], l_ref[...], acc_ref[...] = _softmax_step(
                s, vt_ref[:, pl.ds(off, blk)], m_ref[...], l_ref[...], acc_ref[...], keys_axis=0)

        return c

    lax.fori_loop(0, i, near_block, 0)
    o_ref[...] = (acc_ref[...] / l_ref[...]).T.astype(o_ref.dtype)


def _moba(proj, positions, rel_bias, *, group=4):
    s = proj.shape[0]
    blk = MOBA_BLOCK
    nblk = s // blk
    assert s % blk == 0 and nblk <= V7X_LANES and nblk % group == 0
    pos_blocks = positions.reshape(nblk, blk)
    pmin = jnp.min(pos_blocks, axis=1)
    pmax = jnp.max(pos_blocks, axis=1)
    table = rel_bias[_t5_bucket(jnp.arange(BIAS_TABLE, dtype=jnp.int32))]
    table = jnp.transpose(table).reshape(MOBA_HEADS, 1, BIAS_TABLE).astype(F32)
    table = (table - table[:, :, BIAS_TABLE - 1:]) * LOG2E
    pmax_lanes = jnp.pad(pmax, (0, V7X_LANES - nblk)).reshape(1, V7X_LANES)
    qc, kc, vc = (c // MOBA_DH for c in (COL_MOBA_Q, COL_MOBA_K, COL_MOBA_V))
    grid_spec = pltpu.PrefetchScalarGridSpec(
        num_scalar_prefetch=2,
        grid=(MOBA_HEADS, nblk),
        in_specs=[
            pl.BlockSpec((blk, MOBA_DH), lambda h, i, *_: (i, qc + h)),
            pl.BlockSpec((s, MOBA_DH), lambda h, i, *_: (0, kc + h)),
            pl.BlockSpec((s, MOBA_DH), lambda h, i, *_: (0, vc + h)),
            pl.BlockSpec((blk, 1), lambda h, i, *_: (i, 0)),
            pl.BlockSpec((1, s), lambda h, i, *_: (0, 0)),
            pl.BlockSpec((1, V7X_LANES), lambda h, i, *_: (0, 0)),
            pl.BlockSpec((1, 1, BIAS_TABLE), lambda h, i, *_: (h, 0, 0)),
        ],
        out_specs=pl.BlockSpec((blk, MOBA_DH), lambda h, i, *_: (i, h)),
        scratch_shapes=[
            pltpu.VMEM((V7X_LANES, MOBA_DH), F32),
            pltpu.VMEM((s, 2 * MOBA_DH), BF16),
            pltpu.VMEM((MOBA_DH, s), BF16),
        ] + _softmax_scratch(MOBA_DH, blk, group * blk, keys_axis=0),
    )
    return pl.pallas_call(
        functools.partial(_moba_body, nblk=nblk, scale=MOBA_DH ** -0.5 * LOG2E, group=group),
        grid_spec=grid_spec,
        out_shape=jax.ShapeDtypeStruct((s, MOBA_HEADS * MOBA_DH), BF16),
        compiler_params=_params(("arbitrary", "arbitrary")),
        name="moba",
    )(pmin, pmax, proj, proj, proj, positions.reshape(s, 1), positions.reshape(1, s), pmax_lanes, table)


def _merge_body(oa_ref, ob_ref, ga_ref, gb_ref, wa_ref, wb_ref, wo_ref, x_ref, o_ref):
    @pl.when(pl.program_id(1) == 0)
    def _():
        o_ref[...] = x_ref[...]

    ma = jnp.dot(oa_ref[...], wa_ref[...], preferred_element_type=F32)
    mb = jnp.dot(ob_ref[...], wb_ref[...], preferred_element_type=F32)
    mg = _sigmoid(ga_ref[...].astype(F32)) * ma + _sigmoid(gb_ref[...].astype(F32)) * mb
    o_ref[...] += jnp.dot(mg.astype(BF16), wo_ref[...], preferred_element_type=F32)


def _merge(o_a, o_b, proj, w_a, w_b, w_out, x, *, tm=512, tn=512):
    s, d = x.shape
    ka, kb = o_a.shape[1], o_b.shape[1]
    ga, gb = COL_GATE_A // tn, COL_GATE_B // tn
    return pl.pallas_call(
        _merge_body,
        grid=(s // tm, d // tn),
        in_specs=[
            pl.BlockSpec((tm, ka), lambda i, j: (i, 0)),
            pl.BlockSpec((tm, kb), lambda i, j: (i, 0)),
            pl.BlockSpec((tm, tn), lambda i, j: (i, ga + j)),
            pl.BlockSpec((tm, tn), lambda i, j: (i, gb + j)),
            pl.BlockSpec((ka, tn), lambda i, j: (0, j)),
            pl.BlockSpec((kb, tn), lambda i, j: (0, j)),
            pl.BlockSpec((tn, d), lambda i, j: (j, 0)),
            _row_resident_spec(tm, d),
        ],
        out_specs=pl.BlockSpec((tm, d), lambda i, j: (i, 0)),
        out_shape=jax.ShapeDtypeStruct((s, d), F32),
        compiler_params=_params(("arbitrary", "arbitrary")),
        name="merge",
    )(o_a, o_b, proj, proj, w_a, w_b, w_out, x)


def _proj_layout(w_in):
    w_lat = jnp.pad(w_in[:, :LATENT_REAL].astype(BF16), ((0, 0), (0, LATENT_COLS - LATENT_REAL)))
    return w_lat, w_in[:, LATENT_REAL:].astype(BF16)


def _q_layout(w_uq):
    r = w_uq.shape[0]
    w = w_uq.reshape(r, MLA_HEADS, MLA_NOPE + MLA_ROPE)
    w = jnp.pad(w, ((0, 0), (0, 0), (0, MLA_QK_PAD - MLA_NOPE - MLA_ROPE)))
    return w.reshape(r, MLA_HEADS * MLA_QK_PAD).astype(BF16)


def kernel(x, positions, rel_bias, norm_ffn1, w_ffn1_in, w_ffn1_out, norm_mix, w_in, norm_cq, w_uq, norm_ckv, w_ukv, w_br_a, w_br_b, w_out, norm_ffn2, w_ffn2_in, w_ffn2_out, norm_final):
    b, s, d = x.shape
    assert b == 1 and norm_ffn1.shape[0] == 1
    pos = positions[0]
    x0 = x[0]

    x1 = _ffn(x0, norm_ffn1[0], w_ffn1_in[0].astype(BF16), w_ffn1_out[0].astype(BF16))
    w_lat, w_mix = _proj_layout(w_in[0])
    latent = _proj(x1, norm_mix[0], w_lat)
    mixer = _proj(x1, norm_mix[0], w_mix)
    tables = _rope_tables(pos)
    q = _q_up(latent, norm_cq[0], _q_layout(w_uq[0]), tables)
    k, v = _kv_up(latent, norm_ckv[0], w_ukv[0].astype(BF16), tables)
    o_a = _mla(q, k, v)
    o_b = _moba(mixer, pos, rel_bias)
    x2 = _merge(o_a, o_b, mixer, w_br_a[0].astype(BF16), w_br_b[0].astype(BF16), w_out[0].astype(BF16), x1)
    out = _ffn(x2, norm_ffn2[0], w_ffn2_in[0].astype(BF16), w_ffn2_out[0].astype(BF16), norm_final)
    return out[None]
```

```python
import functools
import math

import jax
import jax.numpy as jnp
from jax import lax
from jax.experimental import pallas as pl
from jax.experimental.pallas import tpu as pltpu

F32 = jnp.float32
BF16 = jnp.bfloat16

RMS_EPS = 1e-6
ROPE_THETA = 10000.0
LOG2E = math.log2(math.e)
MASKED = -1e30

MLA_HEADS = 16
MLA_Q_LORA = 1024
MLA_KV_LORA = 512
MLA_NOPE = 128
MLA_ROPE = 64
MLA_V = 128
MLA_QK_PAD = 256

MOBA_HEADS = 16
MOBA_DH = 128
MOBA_BLOCK = 256
MOBA_TOPK = 3
REL_BUCKETS = 32
REL_MAX_DIST = 128
BIAS_TABLE = 128

V7X_LANES = 128
V7X_VMEM_LIMIT = 56 * 1024 * 1024

LATENT_REAL = MLA_Q_LORA + MLA_KV_LORA + MLA_ROPE
LATENT_COLS = 2048
COL_CQ = 0
COL_CKV = 1024
COL_KROPE = 1536
COL_MOBA_Q = 0
COL_MOBA_K = 2048
COL_MOBA_V = 4096
COL_GATE_A = 6144
COL_GATE_B = 10240

_NT = (((1,), (1,)), ((), ()))


def _params(sem):
    return pltpu.CompilerParams(dimension_semantics=sem, vmem_limit_bytes=V7X_VMEM_LIMIT)


def _row_resident_spec(tm, d):
    return pl.BlockSpec((tm, d), lambda i, j: (i, 0), pipeline_mode=pl.Buffered(1))


def _rms(xf, gain):
    ms = jnp.mean(xf * xf, axis=-1, keepdims=True)
    return xf * lax.rsqrt(ms + RMS_EPS) * gain


def _sigmoid(z):
    return 1.0 / (1.0 + jnp.exp(-z))


def _ffn_body(x_ref, g_ref, wg_ref, wu_ref, wo_ref, *rest, final_norm):
    if final_norm:
        gf_ref, o_ref, h_ref = rest
    else:
        o_ref, h_ref = rest
    j = pl.program_id(1)

    @pl.when(j == 0)
    def _():
        xf = x_ref[...]
        h_ref[...] = _rms(xf, g_ref[...]).astype(BF16)
        o_ref[...] = xf

    h = h_ref[...]
    g = jnp.dot(h, wg_ref[...], preferred_element_type=F32)
    u = jnp.dot(h, wu_ref[...], preferred_element_type=F32)
    a = (g * _sigmoid(g) * (0.5 * u)).astype(BF16)
    o_ref[...] += jnp.dot(a, wo_ref[...], preferred_element_type=F32)

    if final_norm:
        @pl.when(j == pl.num_programs(1) - 1)
        def _():
            o_ref[...] = _rms(o_ref[...], gf_ref[...])


def _ffn(x, gain, w_in, w_out, final_gain=None, *, tm=512, tf=256):
    s, d = x.shape
    f = w_out.shape[0]
    nf = f // tf
    assert s % tm == 0 and f % tf == 0 and w_in.shape == (d, 2 * f)
    in_specs = [
        _row_resident_spec(tm, d),
        pl.BlockSpec((1, d), lambda i, j: (0, 0)),
        pl.BlockSpec((d, tf), lambda i, j: (0, j)),
        pl.BlockSpec((d, tf), lambda i, j: (0, j + nf)),
        pl.BlockSpec((tf, d), lambda i, j: (j, 0)),
    ]
    args = [x, gain.reshape(1, d), w_in, w_in, w_out]
    if final_gain is not None:
        in_specs.append(pl.BlockSpec((1, d), lambda i, j: (0, 0)))
        args.append(final_gain.reshape(1, d))
    return pl.pallas_call(
        functools.partial(_ffn_body, final_norm=final_gain is not None),
        grid=(s // tm, nf),
        in_specs=in_specs,
        out_specs=pl.BlockSpec((tm, d), lambda i, j: (i, 0)),
        out_shape=jax.ShapeDtypeStruct((s, d), F32),
        scratch_shapes=[pltpu.VMEM((tm, d), BF16)],
        compiler_params=_params(("arbitrary", "arbitrary")),
        name="ffn",
    )(*args)


def _proj_body(x_ref, g_ref, w_ref, o_ref, h_ref):
    @pl.when(pl.program_id(1) == 0)
    def _():
        h_ref[...] = _rms(x_ref[...], g_ref[...]).astype(BF16)

    o_ref[...] = jnp.dot(h_ref[...], w_ref[...], preferred_element_type=F32).astype(o_ref.dtype)


def _proj(x, gain, w, *, tm=512, tn=1024):
    s, d = x.shape
    n = w.shape[1]
    assert s % tm == 0 and n % tn == 0
    return pl.pallas_call(
        _proj_body,
        grid=(s // tm, n // tn),
        in_specs=[
            _row_resident_spec(tm, d),
            pl.BlockSpec((1, d), lambda i, j: (0, 0)),
            pl.BlockSpec((d, tn), lambda i, j: (0, j)),
        ],
        out_specs=pl.BlockSpec((tm, tn), lambda i, j: (i, j)),
        out_shape=jax.ShapeDtypeStruct((s, n), BF16),
        scratch_shapes=[pltpu.VMEM((tm, d), BF16)],
        compiler_params=_params(("arbitrary", "arbitrary")),
        name="proj",
    )(x, gain.reshape(1, d), w)


def _rope_lanes(r, cf, s1, s2):
    return r * cf + pltpu.roll(r, 96, 1) * s1 + pltpu.roll(r, 32, 1) * s2


def _rope_tables(positions):
    half = MLA_ROPE // 2
    inv_freq = ROPE_THETA ** (-jnp.arange(0, MLA_ROPE, 2, dtype=F32) / MLA_ROPE)
    ang = positions[:, None].astype(F32) * inv_freq
    cos, sin = jnp.cos(ang), jnp.sin(ang)
    z = jnp.zeros_like(cos)
    pad = jnp.zeros((positions.shape[0], V7X_LANES - MLA_ROPE), F32)
    cf = jnp.concatenate([cos, cos, pad], axis=1)
    s1 = jnp.concatenate([-sin, z, pad], axis=1)
    s2 = jnp.concatenate([z, sin, pad], axis=1)
    return cf, s1, s2


def _qup_body(c_ref, g_ref, w_ref, cf_ref, s1_ref, s2_ref, q_ref, h_ref, *, scale):
    @pl.when(pl.program_id(1) == 0)
    def _():
        h_ref[...] = _rms(c_ref[...].astype(F32), g_ref[...]).astype(BF16)

    y = jnp.dot(h_ref[...], w_ref[...], preferred_element_type=F32)
    for hh in range(q_ref.shape[0]):
        c0 = hh * MLA_QK_PAD
        q_ref[hh, :, :MLA_NOPE] = (y[:, c0:c0 + MLA_NOPE] * scale).astype(q_ref.dtype)
        r = _rope_lanes(y[:, c0 + MLA_NOPE:c0 + MLA_QK_PAD], cf_ref[...], s1_ref[...], s2_ref[...])
        q_ref[hh, :, MLA_NOPE:] = (r * scale).astype(q_ref.dtype)


def _q_up(proj, gain, w_q, tables, *, tm=1024, hg=4):
    s = proj.shape[0]
    cf, s1, s2 = tables
    scale = (MLA_NOPE + MLA_ROPE) ** -0.5 * LOG2E
    tab_spec = pl.BlockSpec((tm, V7X_LANES), lambda i, h: (i, 0))
    return pl.pallas_call(
        functools.partial(_qup_body, scale=scale),
        grid=(s // tm, MLA_HEADS // hg),
        in_specs=[
            pl.BlockSpec((tm, MLA_Q_LORA), lambda i, h: (i, COL_CQ // MLA_Q_LORA)),
            pl.BlockSpec((1, MLA_Q_LORA), lambda i, h: (0, 0)),
            pl.BlockSpec((MLA_Q_LORA, hg * MLA_QK_PAD), lambda i, h: (0, h)),
            tab_spec, tab_spec, tab_spec,
        ],
        out_specs=pl.BlockSpec((hg, tm, MLA_QK_PAD), lambda i, h: (h, i, 0)),
        out_shape=jax.ShapeDtypeStruct((MLA_HEADS, s, MLA_QK_PAD), BF16),
        scratch_shapes=[pltpu.VMEM((tm, MLA_Q_LORA), BF16)],
        compiler_params=_params(("arbitrary", "arbitrary")),
        name="q_up",
    )(proj, gain.reshape(1, MLA_Q_LORA), w_q, cf, s1, s2)


def _kvup_body(c_ref, kr_ref, g_ref, w_ref, cf_ref, s1_ref, s2_ref, k_ref, v_ref, h_ref, r_ref):
    @pl.when(pl.program_id(1) == 0)
    def _():
        h_ref[...] = _rms(c_ref[...].astype(F32), g_ref[...]).astype(BF16)
        r_ref[...] = _rope_lanes(kr_ref[...].astype(F32), cf_ref[...], s1_ref[...], s2_ref[...]).astype(BF16)

    y = jnp.dot(h_ref[...], w_ref[...], preferred_element_type=F32)
    for hh in range(k_ref.shape[0]):
        c0 = hh * (MLA_NOPE + MLA_V)
        k_ref[hh, :, :MLA_NOPE] = y[:, c0:c0 + MLA_NOPE].astype(k_ref.dtype)
        k_ref[hh, :, MLA_NOPE:] = r_ref[...]
        v_ref[hh] = y[:, c0 + MLA_NOPE:c0 + MLA_NOPE + MLA_V].astype(v_ref.dtype)


def _kv_up(proj, gain, w_ukv, tables, *, tm=1024, hg=4):
    s = proj.shape[0]
    cf, s1, s2 = tables
    tab_spec = pl.BlockSpec((tm, V7X_LANES), lambda i, h: (i, 0))
    return pl.pallas_call(
        _kvup_body,
        grid=(s // tm, MLA_HEADS // hg),
        in_specs=[
            pl.BlockSpec((tm, MLA_KV_LORA), lambda i, h: (i, COL_CKV // MLA_KV_LORA)),
            pl.BlockSpec((tm, V7X_LANES), lambda i, h: (i, COL_KROPE // V7X_LANES)),
            pl.BlockSpec((1, MLA_KV_LORA), lambda i, h: (0, 0)),
            pl.BlockSpec((MLA_KV_LORA, hg * (MLA_NOPE + MLA_V)), lambda i, h: (0, h)),
            tab_spec, tab_spec, tab_spec,
        ],
        out_specs=[
            pl.BlockSpec((hg, tm, MLA_QK_PAD), lambda i, h: (h, i, 0)),
            pl.BlockSpec((hg, tm, MLA_V), lambda i, h: (h, i, 0)),
        ],
        out_shape=[
            jax.ShapeDtypeStruct((MLA_HEADS, s, MLA_QK_PAD), BF16),
            jax.ShapeDtypeStruct((MLA_HEADS, s, MLA_V), BF16),
        ],
        scratch_shapes=[pltpu.VMEM((tm, MLA_KV_LORA), BF16), pltpu.VMEM((tm, V7X_LANES), BF16)],
        compiler_params=_params(("arbitrary", "arbitrary")),
        name="kv_up",
    )(proj, proj, gain.reshape(1, MLA_KV_LORA), w_ukv, cf, s1, s2)


def _pv(values, p, keys_axis):
    if keys_axis == 0:
        return jnp.dot(values, p, preferred_element_type=F32)
    return jnp.dot(p, values, preferred_element_type=F32)


def _softmax_step(s, values, m, l, acc, keys_axis):
    m_new = jnp.maximum(m, jnp.max(s, axis=keys_axis, keepdims=True))
    alpha = jnp.exp2(m - m_new)
    p = jnp.exp2(s - m_new)
    l = alpha * l + jnp.sum(p, axis=keys_axis, keepdims=True)
    acc = alpha * acc + _pv(values, p.astype(values.dtype), keys_axis)
    return m_new, l, acc


def _softmax_chunks(scores_fn, values_fn, n, last_chunk, state, s_ref, p_ref, keys_axis):
    m_ref, l_ref, acc_ref, a_ref = state

    def flush(g, slot):
        acc_ref[...] = a_ref[...] * acc_ref[...] + _pv(values_fn(jnp.maximum(g, 0)), p_ref[slot], keys_axis)

    def step(g, cur):
        s_ref[1 - cur] = scores_fn(jnp.minimum(g + 1, last_chunk))
        flush(g - 1, 1 - cur)
        s = s_ref[cur]
        m = m_ref[...]
        m_new = jnp.maximum(m, jnp.max(s, axis=keys_axis, keepdims=True))
        alpha = jnp.exp2(m - m_new)
        p = jnp.exp2(s - m_new)
        l_ref[...] = alpha * l_ref[...] + jnp.sum(p, axis=keys_axis, keepdims=True)
        p_ref[cur] = p.astype(p_ref.dtype)
        m_ref[...] = m_new
        a_ref[...] = alpha

    s_ref[0] = scores_fn(0)
    p_ref[1] = jnp.zeros(p_ref.shape[1:], p_ref.dtype)
    a_ref[...] = jnp.ones(a_ref.shape, a_ref.dtype)

    def pair(k, c):
        step(2 * k, 0)
        step(2 * k + 1, 1)
        return c

    lax.fori_loop(0, n // 2, pair, 0)

    @pl.when(n % 2 == 1)
    def _():
        step(n - 1, 0)

    flush(n - 1, (n + 1) % 2)


def _softmax_scratch(dv, tq, span, keys_axis):
    stat, acc, tile = ((1, tq), (dv, tq), (span, tq)) if keys_axis == 0 else ((tq, 1), (tq, dv), (tq, span))
    return [pltpu.VMEM(stat, F32), pltpu.VMEM(stat, F32), pltpu.VMEM(acc, F32), pltpu.VMEM(stat, F32),
            pltpu.VMEM((2,) + tile, F32), pltpu.VMEM((2,) + tile, BF16)]


def _mla_body(q_ref, k_ref, v_ref, o_ref, *, t):
    i = pl.program_id(1)
    q = q_ref[0]

    def scores(n):
        return lax.dot_general(q, k_ref[0, pl.ds(pl.multiple_of(n * t, t), t), :], _NT, preferred_element_type=F32)

    def values(n):
        return v_ref[0, pl.ds(pl.multiple_of(n * t, t), t), :]

    def body(n, c):
        return _softmax_step(scores(n), values(n), *c, keys_axis=1)

    init = (jnp.full((t, 1), -jnp.inf, F32), jnp.zeros((t, 1), F32), jnp.zeros((t, MLA_V), F32))
    m, l, acc = lax.fori_loop(0, i, body, init)
    row = lax.broadcasted_iota(jnp.int32, (t, t), 0)
    col = lax.broadcasted_iota(jnp.int32, (t, t), 1)
    m, l, acc = _softmax_step(jnp.where(col <= row, scores(i), -jnp.inf), values(i), m, l, acc, keys_axis=1)
    o_ref[...] = (acc / l).astype(o_ref.dtype)


def _mla(q, k, v, *, t=1024):
    h, s, _ = q.shape
    assert s % t == 0
    return pl.pallas_call(
        functools.partial(_mla_body, t=t),
        grid=(h, s // t),
        in_specs=[
            pl.BlockSpec((1, t, MLA_QK_PAD), lambda h, i: (h, i, 0)),
            pl.BlockSpec((1, s, MLA_QK_PAD), lambda h, i: (h, 0, 0)),
            pl.BlockSpec((1, s, MLA_V), lambda h, i: (h, 0, 0)),
        ],
        out_specs=pl.BlockSpec((t, MLA_V), lambda h, i: (i, h)),
        out_shape=jax.ShapeDtypeStruct((s, h * MLA_V), BF16),
        compiler_params=_params(("arbitrary", "arbitrary")),
        name="mla",
    )(q, k, v)


def _t5_bucket(n):
    max_exact = REL_BUCKETS // 2
    n_f = jnp.maximum(n, max_exact).astype(F32)
    large = max_exact + (jnp.log(n_f / max_exact) / math.log(REL_MAX_DIST / max_exact)
                         * (REL_BUCKETS - max_exact)).astype(jnp.int32)
    large = jnp.minimum(large, REL_BUCKETS - 1)
    return jnp.where(n < max_exact, n, large)


def _moba_body(pmin_ref, pmax_ref, q_ref, k_ref, v_ref, pq_ref, pk_ref, pmaxl_ref, tbl_ref, o_ref,
               kmean_ref, kaug_ref, vt_ref, m_ref, l_ref, acc_ref, a_ref, s_ref, p_ref, *, nblk, scale, group):
    i = pl.program_id(1)
    blk = MOBA_BLOCK
    lane_id = lax.broadcasted_iota(jnp.int32, (blk, V7X_LANES), 1)

    @pl.when(i == 0)
    def _():
        kmean_ref[...] = jnp.zeros_like(kmean_ref)

        def fill(n, c):
            rows = pl.ds(pl.multiple_of(n * blk, blk), blk)
            kb = k_ref[rows, :]
            kmean_ref[pl.ds(n, 1), :] = jnp.sum(kb.astype(F32), axis=0, keepdims=True) * (1.0 / blk)
            kaug_ref[rows, :MOBA_DH] = kb
            kaug_ref[rows, MOBA_DH:] = jnp.where(lane_id == n, 1.0, 0.0).astype(kaug_ref.dtype)
            vt_ref[:, rows] = v_ref[rows, :].astype(F32).T.astype(vt_ref.dtype)
            return c

        lax.fori_loop(0, nblk, fill, 0)

    q = q_ref[...]
    qs = (q.astype(F32) * scale).astype(q.dtype)

    gate = lax.dot_general(kmean_ref[...], q.astype(F32), _NT, preferred_element_type=F32)
    bid = lax.broadcasted_iota(jnp.int32, gate.shape, 0)
    bid_f = bid.astype(F32)
    g = jnp.where(bid < i, gate, -jnp.inf)
    picked_t = jnp.zeros(gate.shape, F32)
    for _ in range(min(MOBA_TOPK, nblk)):
        mx = jnp.max(g, axis=0, keepdims=True)
        idx = jnp.min(jnp.where(g == mx, bid_f, float(V7X_LANES)), axis=0, keepdims=True)
        hit = bid_f == jnp.where(mx > -jnp.inf, idx, -1.0)
        picked_t = jnp.where(hit, 1.0, picked_t)
        g = jnp.where(hit, -jnp.inf, g)
    picked = picked_t.T

    prev = jnp.maximum(i - 1, 0)
    far_lanes = jnp.logical_and((pmin_ref[i] - pmaxl_ref[...]) >= BIAS_TABLE - 1, lane_id[:1] != prev)
    m_any = jnp.where(picked > 0.0, 0.0, MASKED)
    q_any = jnp.concatenate([qs, m_any.astype(qs.dtype)], axis=1)
    q_far = jnp.concatenate([qs, jnp.where(far_lanes, m_any, MASKED).astype(qs.dtype)], axis=1)
    q_diag = jnp.concatenate([qs, jnp.where(lane_id == i, 0.0, m_any).astype(qs.dtype)], axis=1)

    def is_far(n):
        return (pmin_ref[i] - pmax_ref[n]) >= BIAS_TABLE - 1

    def masked_scores(off, width, q_aug):
        return lax.dot_general(kaug_ref[pl.ds(off, width), :], q_aug, _NT, preferred_element_type=F32)

    def near_bias(off):
        off = pl.multiple_of(off, blk)
        d = jnp.clip(pq_ref[...] - pk_ref[:, pl.ds(off, blk)], 0, BIAS_TABLE - 1)
        tb = jnp.broadcast_to(tbl_ref[0], (blk, BIAS_TABLE))
        parts = [jnp.take_along_axis(tb, d[:, c * V7X_LANES:(c + 1) * V7X_LANES], axis=1)
                 for c in range(blk // V7X_LANES)]
        return jnp.concatenate(parts, axis=1).T

    off_d = pl.multiple_of(prev * blk, blk)
    key = prev * blk + lax.broadcasted_iota(jnp.int32, (2 * blk, blk), 0)
    qry = i * blk + lax.broadcasted_iota(jnp.int32, (2 * blk, blk), 1)
    s = masked_scores(off_d, 2 * blk, q_diag) + jnp.concatenate([near_bias(off_d), near_bias(off_d + blk)], axis=0)
    s = jnp.where(key <= qry, s, -jnp.inf)
    m = jnp.max(s, axis=0, keepdims=True)
    p = jnp.exp2(s - m)
    m_ref[...] = m
    l_ref[...] = jnp.sum(p, axis=0, keepdims=True)
    acc_ref[...] = jnp.dot(vt_ref[:, pl.ds(off_d, 2 * blk)], p.astype(vt_ref.dtype), preferred_element_type=F32)

    span = group * blk

    def far_scores(g):
        return masked_scores(pl.multiple_of(g * span, span), span, q_far)

    def far_values_t(g):
        return vt_ref[:, pl.ds(pl.multiple_of(g * span, span), span)]

    _softmax_chunks(far_scores, far_values_t, (i + group - 1) // group, nblk // group - 1,
                    (m_ref, l_ref, acc_ref, a_ref), s_ref, p_ref, keys_axis=0)

    def near_block(n, c):
        @pl.when(jnp.logical_not(is_far(n)))
        def _():
            off = pl.multiple_of(n * blk, blk)
            s = masked_scores(off, blk, q_any) + near_bias(off)
            m_ref[...], l_ref[...], acc_ref[...] = _softmax_step(
                s, vt_ref[:, pl.ds(off, blk)], m_ref[...], l_ref[...], acc_ref[...], keys_axis=0)

        return c

    lax.fori_loop(0, prev, near_block, 0)
    o_ref[...] = (acc_ref[...] / l_ref[...]).T.astype(o_ref.dtype)


def _moba(proj, positions, rel_bias, *, group=4):
    s = proj.shape[0]
    blk = MOBA_BLOCK
    nblk = s // blk
    assert s % blk == 0 and 2 <= nblk <= V7X_LANES and nblk % group == 0
    pos_blocks = positions.reshape(nblk, blk)
    pmin = jnp.min(pos_blocks, axis=1)
    pmax = jnp.max(pos_blocks, axis=1)
    table = rel_bias[_t5_bucket(jnp.arange(BIAS_TABLE, dtype=jnp.int32))]
    table = jnp.transpose(table).reshape(MOBA_HEADS, 1, BIAS_TABLE).astype(F32)
    table = (table - table[:, :, BIAS_TABLE - 1:]) * LOG2E
    pmax_lanes = jnp.pad(pmax, (0, V7X_LANES - nblk)).reshape(1, V7X_LANES)
    qc, kc, vc = (c // MOBA_DH for c in (COL_MOBA_Q, COL_MOBA_K, COL_MOBA_V))
    grid_spec = pltpu.PrefetchScalarGridSpec(
        num_scalar_prefetch=2,
        grid=(MOBA_HEADS, nblk),
        in_specs=[
            pl.BlockSpec((blk, MOBA_DH), lambda h, i, *_: (i, qc + h)),
            pl.BlockSpec((s, MOBA_DH), lambda h, i, *_: (0, kc + h)),
            pl.BlockSpec((s, MOBA_DH), lambda h, i, *_: (0, vc + h)),
            pl.BlockSpec((blk, 1), lambda h, i, *_: (i, 0)),
            pl.BlockSpec((1, s), lambda h, i, *_: (0, 0)),
            pl.BlockSpec((1, V7X_LANES), lambda h, i, *_: (0, 0)),
            pl.BlockSpec((1, 1, BIAS_TABLE), lambda h, i, *_: (h, 0, 0)),
        ],
        out_specs=pl.BlockSpec((blk, MOBA_DH), lambda h, i, *_: (i, h)),
        scratch_shapes=[
            pltpu.VMEM((V7X_LANES, MOBA_DH), F32),
            pltpu.VMEM((s, 2 * MOBA_DH), BF16),
            pltpu.VMEM((MOBA_DH, s), BF16),
        ] + _softmax_scratch(MOBA_DH, blk, group * blk, keys_axis=0),
    )
    return pl.pallas_call(
        functools.partial(_moba_body, nblk=nblk, scale=MOBA_DH ** -0.5 * LOG2E, group=group),
        grid_spec=grid_spec,
        out_shape=jax.ShapeDtypeStruct((s, MOBA_HEADS * MOBA_DH), BF16),
        compiler_params=_params(("arbitrary", "arbitrary")),
        name="moba",
    )(pmin, pmax, proj, proj, proj, positions.reshape(s, 1), positions.reshape(1, s), pmax_lanes, table)


def _merge_body(oa_ref, ob_ref, ga_ref, gb_ref, wa_ref, wb_ref, wo_ref, x_ref, o_ref):
    @pl.when(pl.program_id(1) == 0)
    def _():
        o_ref[...] = x_ref[...]

    ma = jnp.dot(oa_ref[...], wa_ref[...], preferred_element_type=F32)
    mb = jnp.dot(ob_ref[...], wb_ref[...], preferred_element_type=F32)
    mg = _sigmoid(ga_ref[...].astype(F32)) * ma + _sigmoid(gb_ref[...].astype(F32)) * mb
    o_ref[...] += jnp.dot(mg.astype(BF16), wo_ref[...], preferred_element_type=F32)


def _merge(o_a, o_b, proj, w_a, w_b, w_out, x, *, tm=512, tn=512):
    s, d = x.shape
    ka, kb = o_a.shape[1], o_b.shape[1]
    ga, gb = COL_GATE_A // tn, COL_GATE_B // tn
    return pl.pallas_call(
        _merge_body,
        grid=(s // tm, d // tn),
        in_specs=[
            pl.BlockSpec((tm, ka), lambda i, j: (i, 0)),
            pl.BlockSpec((tm, kb), lambda i, j: (i, 0)),
            pl.BlockSpec((tm, tn), lambda i, j: (i, ga + j)),
            pl.BlockSpec((tm, tn), lambda i, j: (i, gb + j)),
            pl.BlockSpec((ka, tn), lambda i, j: (0, j)),
            pl.BlockSpec((kb, tn), lambda i, j: (0, j)),
            pl.BlockSpec((tn, d), lambda i, j: (j, 0)),
            _row_resident_spec(tm, d),
        ],
        out_specs=pl.BlockSpec((tm, d), lambda i, j: (i, 0)),
        out_shape=jax.ShapeDtypeStruct((s, d), F32),
        compiler_params=_params(("arbitrary", "arbitrary")),
        name="merge",
    )(o_a, o_b, proj, proj, w_a, w_b, w_out, x)


def _proj_layout(w_in):
    w_lat = jnp.pad(w_in[:, :LATENT_REAL].astype(BF16), ((0, 0), (0, LATENT_COLS - LATENT_REAL)))
    return w_lat, w_in[:, LATENT_REAL:].astype(BF16)


def _q_layout(w_uq):
    r = w_uq.shape[0]
    w = w_uq.reshape(r, MLA_HEADS, MLA_NOPE + MLA_ROPE)
    w = jnp.pad(w, ((0, 0), (0, 0), (0, MLA_QK_PAD - MLA_NOPE - MLA_ROPE)))
    return w.reshape(r, MLA_HEADS * MLA_QK_PAD).astype(BF16)


def kernel(x, positions, rel_bias, norm_ffn1, w_ffn1_in, w_ffn1_out, norm_mix, w_in, norm_cq, w_uq, norm_ckv, w_ukv, w_br_a, w_br_b, w_out, norm_ffn2, w_ffn2_in, w_ffn2_out, norm_final):
    b, s, d = x.shape
    assert b == 1 and norm_ffn1.shape[0] == 1
    pos = positions[0]
    x0 = x[0]

    x1 = _ffn(x0, norm_ffn1[0], w_ffn1_in[0].astype(BF16), w_ffn1_out[0].astype(BF16))
    w_lat, w_mix = _proj_layout(w_in[0])
    latent = _proj(x1, norm_mix[0], w_lat)
    mixer = _proj(x1, norm_mix[0], w_mix)
    tables = _rope_tables(pos)
    q = _q_up(latent, norm_cq[0], _q_layout(w_uq[0]), tables)
    k, v = _kv_up(latent, norm_ckv[0], w_ukv[0].astype(BF16), tables)
    o_a = _mla(q, k, v)
    o_b = _moba(mixer, pos, rel_bias)
    x2 = _merge(o_a, o_b, mixer, w_br_a[0].astype(BF16), w_br_b[0].astype(BF16), w_out[0].astype(BF16), x1)
    out = _ffn(x2, norm_ffn2[0], w_ffn2_in[0].astype(BF16), w_ffn2_out[0].astype(BF16), norm_final)
    return out[None]
```

```python
import functools
import math

import jax
import jax.numpy as jnp
from jax import lax
from jax.experimental import pallas as pl
from jax.experimental.pallas import tpu as pltpu

F32 = jnp.float32
BF16 = jnp.bfloat16

RMS_EPS = 1e-6
ROPE_THETA = 10000.0
LOG2E = math.log2(math.e)
MASKED = -1e30

MLA_HEADS = 16
MLA_Q_LORA = 1024
MLA_KV_LORA = 512
MLA_NOPE = 128
MLA_ROPE = 64
MLA_V = 128
MLA_QK_PAD = 256

MOBA_HEADS = 16
MOBA_DH = 128
MOBA_BLOCK = 256
MOBA_TOPK = 3
REL_BUCKETS = 32
REL_MAX_DIST = 128
BIAS_TABLE = 128

V7X_LANES = 128
V7X_VMEM_LIMIT = 56 * 1024 * 1024

LATENT_REAL = MLA_Q_LORA + MLA_KV_LORA + MLA_ROPE
LATENT_COLS = 2048
COL_CQ = 0
COL_CKV = 1024
COL_KROPE = 1536
COL_MOBA_Q = 0
COL_MOBA_K = 2048
COL_MOBA_V = 4096
COL_GATE_A = 6144
COL_GATE_B = 10240

_NT = (((1,), (1,)), ((), ()))


def _params(sem):
    return pltpu.CompilerParams(dimension_semantics=sem, vmem_limit_bytes=V7X_VMEM_LIMIT)


def _row_resident_spec(tm, d):
    return pl.BlockSpec((tm, d), lambda i, j: (i, 0), pipeline_mode=pl.Buffered(1))


def _rms(xf, gain):
    ms = jnp.mean(xf * xf, axis=-1, keepdims=True)
    return xf * lax.rsqrt(ms + RMS_EPS) * gain


def _sigmoid(z):
    return 1.0 / (1.0 + jnp.exp(-z))


def _ffn_body(x_ref, g_ref, wg_ref, wu_ref, wo_ref, *rest, final_norm):
    if final_norm:
        gf_ref, o_ref, h_ref = rest
    else:
        o_ref, h_ref = rest
    j = pl.program_id(1)

    @pl.when(j == 0)
    def _():
        xf = x_ref[...]
        h_ref[...] = _rms(xf, g_ref[...]).astype(BF16)
        o_ref[...] = xf

    h = h_ref[...]
    g = jnp.dot(h, wg_ref[...], preferred_element_type=F32)
    u = jnp.dot(h, wu_ref[...], preferred_element_type=F32)
    a = (g * _sigmoid(g) * (0.5 * u)).astype(BF16)
    o_ref[...] += jnp.dot(a, wo_ref[...], preferred_element_type=F32)

    if final_norm:
        @pl.when(j == pl.num_programs(1) - 1)
        def _():
            o_ref[...] = _rms(o_ref[...], gf_ref[...])


def _ffn(x, gain, w_in, w_out, final_gain=None, *, tm=512, tf=256):
    s, d = x.shape
    f = w_out.shape[0]
    nf = f // tf
    assert s % tm == 0 and f % tf == 0 and w_in.shape == (d, 2 * f)
    in_specs = [
        _row_resident_spec(tm, d),
        pl.BlockSpec((1, d), lambda i, j: (0, 0)),
        pl.BlockSpec((d, tf), lambda i, j: (0, j)),
        pl.BlockSpec((d, tf), lambda i, j: (0, j + nf)),
        pl.BlockSpec((tf, d), lambda i, j: (j, 0)),
    ]
    args = [x, gain.reshape(1, d), w_in, w_in, w_out]
    if final_gain is not None:
        in_specs.append(pl.BlockSpec((1, d), lambda i, j: (0, 0)))
        args.append(final_gain.reshape(1, d))
    return pl.pallas_call(
        functools.partial(_ffn_body, final_norm=final_gain is not None),
        grid=(s // tm, nf),
        in_specs=in_specs,
        out_specs=pl.BlockSpec((tm, d), lambda i, j: (i, 0)),
        out_shape=jax.ShapeDtypeStruct((s, d), F32),
        scratch_shapes=[pltpu.VMEM((tm, d), BF16)],
        compiler_params=_params(("arbitrary", "arbitrary")),
        name="ffn",
    )(*args)


def _proj_body(x_ref, g_ref, w_ref, o_ref, h_ref):
    @pl.when(pl.program_id(1) == 0)
    def _():
        h_ref[...] = _rms(x_ref[...], g_ref[...]).astype(BF16)

    o_ref[...] = jnp.dot(h_ref[...], w_ref[...], preferred_element_type=F32).astype(o_ref.dtype)


def _proj(x, gain, w, *, tm=512, tn=1024):
    s, d = x.shape
    n = w.shape[1]
    assert s % tm == 0 and n % tn == 0
    return pl.pallas_call(
        _proj_body,
        grid=(s // tm, n // tn),
        in_specs=[
            _row_resident_spec(tm, d),
            pl.BlockSpec((1, d), lambda i, j: (0, 0)),
            pl.BlockSpec((d, tn), lambda i, j: (0, j)),
        ],
        out_specs=pl.BlockSpec((tm, tn), lambda i, j: (i, j)),
        out_shape=jax.ShapeDtypeStruct((s, n), BF16),
        scratch_shapes=[pltpu.VMEM((tm, d), BF16)],
        compiler_params=_params(("arbitrary", "arbitrary")),
        name="proj",
    )(x, gain.reshape(1, d), w)


def _rope_lanes(r, cf, s1, s2):
    return r * cf + pltpu.roll(r, 96, 1) * s1 + pltpu.roll(r, 32, 1) * s2


def _rope_tables(positions):
    half = MLA_ROPE // 2
    inv_freq = ROPE_THETA ** (-jnp.arange(0, MLA_ROPE, 2, dtype=F32) / MLA_ROPE)
    ang = positions[:, None].astype(F32) * inv_freq
    cos, sin = jnp.cos(ang), jnp.sin(ang)
    z = jnp.zeros_like(cos)
    pad = jnp.zeros((positions.shape[0], V7X_LANES - MLA_ROPE), F32)
    cf = jnp.concatenate([cos, cos, pad], axis=1)
    s1 = jnp.concatenate([-sin, z, pad], axis=1)
    s2 = jnp.concatenate([z, sin, pad], axis=1)
    return cf, s1, s2


def _qup_body(c_ref, g_ref, w_ref, cf_ref, s1_ref, s2_ref, q_ref, h_ref, *, scale):
    @pl.when(pl.program_id(1) == 0)
    def _():
        h_ref[...] = _rms(c_ref[...].astype(F32), g_ref[...]).astype(BF16)

    y = jnp.dot(h_ref[...], w_ref[...], preferred_element_type=F32)
    for hh in range(q_ref.shape[0]):
        c0 = hh * MLA_QK_PAD
        q_ref[hh, :, :MLA_NOPE] = (y[:, c0:c0 + MLA_NOPE] * scale).astype(q_ref.dtype)
        r = _rope_lanes(y[:, c0 + MLA_NOPE:c0 + MLA_QK_PAD], cf_ref[...], s1_ref[...], s2_ref[...])
        q_ref[hh, :, MLA_NOPE:] = (r * scale).astype(q_ref.dtype)


def _q_up(proj, gain, w_q, tables, *, tm=1024, hg=4):
    s = proj.shape[0]
    cf, s1, s2 = tables
    scale = (MLA_NOPE + MLA_ROPE) ** -0.5 * LOG2E
    tab_spec = pl.BlockSpec((tm, V7X_LANES), lambda i, h: (i, 0))
    return pl.pallas_call(
        functools.partial(_qup_body, scale=scale),
        grid=(s // tm, MLA_HEADS // hg),
        in_specs=[
            pl.BlockSpec((tm, MLA_Q_LORA), lambda i, h: (i, COL_CQ // MLA_Q_LORA)),
            pl.BlockSpec((1, MLA_Q_LORA), lambda i, h: (0, 0)),
            pl.BlockSpec((MLA_Q_LORA, hg * MLA_QK_PAD), lambda i, h: (0, h)),
            tab_spec, tab_spec, tab_spec,
        ],
        out_specs=pl.BlockSpec((hg, tm, MLA_QK_PAD), lambda i, h: (h, i, 0)),
        out_shape=jax.ShapeDtypeStruct((MLA_HEADS, s, MLA_QK_PAD), BF16),
        scratch_shapes=[pltpu.VMEM((tm, MLA_Q_LORA), BF16)],
        compiler_params=_params(("arbitrary", "arbitrary")),
        name="q_up",
    )(proj, gain.reshape(1, MLA_Q_LORA), w_q, cf, s1, s2)


def _kvup_body(c_ref, kr_ref, g_ref, w_ref, cf_ref, s1_ref, s2_ref, k_ref, v_ref, h_ref, r_ref):
    @pl.when(pl.program_id(1) == 0)
    def _():
        h_ref[...] = _rms(c_ref[...].astype(F32), g_ref[...]).astype(BF16)
        r_ref[...] = _rope_lanes(kr_ref[...].astype(F32), cf_ref[...], s1_ref[...], s2_ref[...]).astype(BF16)

    y = jnp.dot(h_ref[...], w_ref[...], preferred_element_type=F32)
    for hh in range(k_ref.shape[0]):
        c0 = hh * (MLA_NOPE + MLA_V)
        k_ref[hh, :, :MLA_NOPE] = y[:, c0:c0 + MLA_NOPE].astype(k_ref.dtype)
        k_ref[hh, :, MLA_NOPE:] = r_ref[...]
        v_ref[hh] = y[:, c0 + MLA_NOPE:c0 + MLA_NOPE + MLA_V].astype(v_ref.dtype)


def _kv_up(proj, gain, w_ukv, tables, *, tm=1024, hg=4):
    s = proj.shape[0]
    cf, s1, s2 = tables
    tab_spec = pl.BlockSpec((tm, V7X_LANES), lambda i, h: (i, 0))
    return pl.pallas_call(
        _kvup_body,
        grid=(s // tm, MLA_HEADS // hg),
        in_specs=[
            pl.BlockSpec((tm, MLA_KV_LORA), lambda i, h: (i, COL_CKV // MLA_KV_LORA)),
            pl.BlockSpec((tm, V7X_LANES), lambda i, h: (i, COL_KROPE // V7X_LANES)),
            pl.BlockSpec((1, MLA_KV_LORA), lambda i, h: (0, 0)),
            pl.BlockSpec((MLA_KV_LORA, hg * (MLA_NOPE + MLA_V)), lambda i, h: (0, h)),
            tab_spec, tab_spec, tab_spec,
        ],
        out_specs=[
            pl.BlockSpec((hg, tm, MLA_QK_PAD), lambda i, h: (h, i, 0)),
            pl.BlockSpec((hg, tm, MLA_V), lambda i, h: (h, i, 0)),
        ],
        out_shape=[
            jax.ShapeDtypeStruct((MLA_HEADS, s, MLA_QK_PAD), BF16),
            jax.ShapeDtypeStruct((MLA_HEADS, s, MLA_V), BF16),
        ],
        scratch_shapes=[pltpu.VMEM((tm, MLA_KV_LORA), BF16), pltpu.VMEM((tm, V7X_LANES), BF16)],
        compiler_params=_params(("arbitrary", "arbitrary")),
        name="kv_up",
    )(proj, proj, gain.reshape(1, MLA_KV_LORA), w_ukv, cf, s1, s2)


def _pv(values, p, keys_axis):
    if keys_axis == 0:
        return jnp.dot(values, p, preferred_element_type=F32)
    return jnp.dot(p, values, preferred_element_type=F32)


def _softmax_step(s, values, m, l, acc, keys_axis):
    m_new = jnp.maximum(m, jnp.max(s, axis=keys_axis, keepdims=True))
    alpha = jnp.exp2(m - m_new)
    p = jnp.exp2(s - m_new)
    l = alpha * l + jnp.sum(p, axis=keys_axis, keepdims=True)
    acc = alpha * acc + _pv(values, p.astype(values.dtype), keys_axis)
    return m_new, l, acc


def _softmax_chunks(scores_fn, values_fn, last_chunk, state, s_ref, p_ref, keys_axis):
    m_ref, l_ref, acc_ref, a_ref, cmax_ref = state

    def flush(g, slot):
        acc_ref[...] = a_ref[...] * acc_ref[...] + _pv(values_fn(jnp.maximum(g, 0)), p_ref[slot], keys_axis)

    def produce(g, slot):
        s = scores_fn(jnp.minimum(g, last_chunk))
        s_ref[slot] = s
        cmax_ref[slot] = jnp.max(s, axis=keys_axis, keepdims=True)

    def step(g, cur, produce_next=True):
        if produce_next:
            produce(g + 1, 1 - cur)
        flush(g - 1, 1 - cur)
        m = m_ref[...]
        m_new = jnp.maximum(m, cmax_ref[cur])
        alpha = jnp.exp2(m - m_new)
        p = jnp.exp2(s_ref[cur] - m_new)
        l_ref[...] = alpha * l_ref[...] + jnp.sum(p, axis=keys_axis, keepdims=True)
        p_ref[cur] = p.astype(p_ref.dtype)
        m_ref[...] = m_new
        a_ref[...] = alpha

    def begin():
        produce(0, 0)
        p_ref[1] = jnp.zeros(p_ref.shape[1:], p_ref.dtype)
        a_ref[...] = jnp.ones(a_ref.shape, a_ref.dtype)

    def run(n):
        def pair(k, c):
            step(2 * k, 0)
            step(2 * k + 1, 1)
            return c

        lax.fori_loop(0, n // 2, pair, 0)

        @pl.when(n % 2 == 1)
        def _():
            step(n - 1, 0, produce_next=False)

        flush(n - 1, (n + 1) % 2)

    return begin, run


def _softmax_scratch(dv, tq, span, keys_axis):
    stat, acc, tile = ((1, tq), (dv, tq), (span, tq)) if keys_axis == 0 else ((tq, 1), (tq, dv), (tq, span))
    return [pltpu.VMEM(stat, F32), pltpu.VMEM(stat, F32), pltpu.VMEM(acc, F32), pltpu.VMEM(stat, F32),
            pltpu.VMEM((2,) + stat, F32), pltpu.VMEM((2,) + tile, F32), pltpu.VMEM((2,) + tile, BF16)]


def _mla_body(q_ref, k_ref, v_ref, o_ref, *, t):
    i = pl.program_id(1)
    q = q_ref[0]

    def scores(n):
        return lax.dot_general(q, k_ref[0, pl.ds(pl.multiple_of(n * t, t), t), :], _NT, preferred_element_type=F32)

    def values(n):
        return v_ref[0, pl.ds(pl.multiple_of(n * t, t), t), :]

    def body(n, c):
        return _softmax_step(scores(n), values(n), *c, keys_axis=1)

    init = (jnp.full((t, 1), -jnp.inf, F32), jnp.zeros((t, 1), F32), jnp.zeros((t, MLA_V), F32))
    m, l, acc = lax.fori_loop(0, i, body, init)
    row = lax.broadcasted_iota(jnp.int32, (t, t), 0)
    col = lax.broadcasted_iota(jnp.int32, (t, t), 1)
    m, l, acc = _softmax_step(jnp.where(col <= row, scores(i), -jnp.inf), values(i), m, l, acc, keys_axis=1)
    o_ref[...] = (acc / l).astype(o_ref.dtype)


def _mla(q, k, v, *, t=1024):
    h, s, _ = q.shape
    assert s % t == 0
    return pl.pallas_call(
        functools.partial(_mla_body, t=t),
        grid=(h, s // t),
        in_specs=[
            pl.BlockSpec((1, t, MLA_QK_PAD), lambda h, i: (h, i, 0)),
            pl.BlockSpec((1, s, MLA_QK_PAD), lambda h, i: (h, 0, 0)),
            pl.BlockSpec((1, s, MLA_V), lambda h, i: (h, 0, 0)),
        ],
        out_specs=pl.BlockSpec((t, MLA_V), lambda h, i: (i, h)),
        out_shape=jax.ShapeDtypeStruct((s, h * MLA_V), BF16),
        compiler_params=_params(("arbitrary", "arbitrary")),
        name="mla",
    )(q, k, v)


def _t5_bucket(n):
    max_exact = REL_BUCKETS // 2
    n_f = jnp.maximum(n, max_exact).astype(F32)
    large = max_exact + (jnp.log(n_f / max_exact) / math.log(REL_MAX_DIST / max_exact)
                         * (REL_BUCKETS - max_exact)).astype(jnp.int32)
    large = jnp.minimum(large, REL_BUCKETS - 1)
    return jnp.where(n < max_exact, n, large)


def _moba_body(pmin_ref, pmax_ref, q_ref, k_ref, v_ref, pq_ref, pk_ref, pmaxl_ref, tbl_ref, o_ref,
               kmean_ref, kaug_ref, vt_ref, m_ref, l_ref, acc_ref, a_ref, cmax_ref, s_ref, p_ref,
               *, nblk, scale, group):
    i = pl.program_id(1)
    blk = MOBA_BLOCK
    lane_id = lax.broadcasted_iota(jnp.int32, (blk, V7X_LANES), 1)

    @pl.when(i == 0)
    def _():
        kmean_ref[...] = jnp.zeros_like(kmean_ref)

        def fill(n, c):
            rows = pl.ds(pl.multiple_of(n * blk, blk), blk)
            kb = k_ref[rows, :]
            kmean_ref[pl.ds(n, 1), :] = jnp.sum(kb.astype(F32), axis=0, keepdims=True) * (1.0 / blk)
            kaug_ref[rows, :MOBA_DH] = kb
            kaug_ref[rows, MOBA_DH:] = jnp.where(lane_id == n, 1.0, 0.0).astype(kaug_ref.dtype)
            vt_ref[:, rows] = v_ref[rows, :].astype(F32).T.astype(vt_ref.dtype)
            return c

        lax.fori_loop(0, nblk, fill, 0)

    q = q_ref[...]
    qs = (q.astype(F32) * scale).astype(q.dtype)

    gate = lax.dot_general(kmean_ref[...], q.astype(F32), _NT, preferred_element_type=F32)
    bid = lax.broadcasted_iota(jnp.int32, gate.shape, 0)
    bid_f = bid.astype(F32)
    g = jnp.where(bid < i, gate, -jnp.inf)
    picked_t = jnp.zeros(gate.shape, F32)
    for _ in range(min(MOBA_TOPK, nblk)):
        mx = jnp.max(g, axis=0, keepdims=True)
        idx = jnp.min(jnp.where(g == mx, bid_f, float(V7X_LANES)), axis=0, keepdims=True)
        hit = bid_f == jnp.where(mx > -jnp.inf, idx, -1.0)
        picked_t = jnp.where(hit, 1.0, picked_t)
        g = jnp.where(hit, -jnp.inf, g)
    picked = picked_t.T

    prev = jnp.maximum(i - 1, 0)
    far_lanes = jnp.logical_and((pmin_ref[i] - pmaxl_ref[...]) >= BIAS_TABLE - 1, lane_id[:1] != prev)
    m_any = jnp.where(picked > 0.0, 0.0, MASKED)
    q_any = jnp.concatenate([qs, m_any.astype(qs.dtype)], axis=1)
    q_far = jnp.concatenate([qs, jnp.where(far_lanes, m_any, MASKED).astype(qs.dtype)], axis=1)
    q_diag = jnp.concatenate([qs, jnp.where(lane_id == i, 0.0, m_any).astype(qs.dtype)], axis=1)

    def is_far(n):
        return (pmin_ref[i] - pmax_ref[n]) >= BIAS_TABLE - 1

    def masked_scores(off, width, q_aug):
        return lax.dot_general(kaug_ref[pl.ds(off, width), :], q_aug, _NT, preferred_element_type=F32)

    def near_bias(off):
        off = pl.multiple_of(off, blk)
        d = jnp.clip(pq_ref[...] - pk_ref[:, pl.ds(off, blk)], 0, BIAS_TABLE - 1)
        tb = jnp.broadcast_to(tbl_ref[0], (blk, BIAS_TABLE))
        parts = [jnp.take_along_axis(tb, d[:, c * V7X_LANES:(c + 1) * V7X_LANES], axis=1)
                 for c in range(blk // V7X_LANES)]
        return jnp.concatenate(parts, axis=1).T

    span = group * blk

    def far_scores(g):
        return masked_scores(pl.multiple_of(g * span, span), span, q_far)

    def far_values_t(g):
        return vt_ref[:, pl.ds(pl.multiple_of(g * span, span), span)]

    far_begin, far_run = _softmax_chunks(far_scores, far_values_t, nblk // group - 1,
                                         (m_ref, l_ref, acc_ref, a_ref, cmax_ref), s_ref, p_ref, keys_axis=0)
    off_d = pl.multiple_of(prev * blk, blk)
    key = prev * blk + lax.broadcasted_iota(jnp.int32, (2 * blk, blk), 0)
    qry = i * blk + lax.broadcasted_iota(jnp.int32, (2 * blk, blk), 1)
    s = masked_scores(off_d, 2 * blk, q_diag)
    far_begin()
    s = s + jnp.concatenate([near_bias(off_d), near_bias(off_d + blk)], axis=0)
    s = jnp.where(key <= qry, s, -jnp.inf)
    m = jnp.max(s, axis=0, keepdims=True)
    p = jnp.exp2(s - m)
    m_ref[...] = m
    l_ref[...] = jnp.sum(p, axis=0, keepdims=True)
    acc_ref[...] = jnp.dot(vt_ref[:, pl.ds(off_d, 2 * blk)], p.astype(vt_ref.dtype), preferred_element_type=F32)
    far_run((i + group - 1) // group)

    def near_block(n, c):
        @pl.when(jnp.logical_not(is_far(n)))
        def _():
            off = pl.multiple_of(n * blk, blk)
            s = masked_scores(off, blk, q_any) + near_bias(off)
            m_ref[...], l_ref[...], acc_ref[...] = _softmax_step(
                s, vt_ref[:, pl.ds(off, blk)], m_ref[...], l_ref[...], acc_ref[...], keys_axis=0)

        return c

    lax.fori_loop(0, prev, near_block, 0)
    o_ref[...] = (acc_ref[...] / l_ref[...]).T.astype(o_ref.dtype)


def _moba(proj, positions, rel_bias, *, group=4):
    s = proj.shape[0]
    blk = MOBA_BLOCK
    nblk = s // blk
    assert s % blk == 0 and 2 <= nblk <= V7X_LANES and nblk % group == 0
    pos_blocks = positions.reshape(nblk, blk)
    pmin = jnp.min(pos_blocks, axis=1)
    pmax = jnp.max(pos_blocks, axis=1)
    table = rel_bias[_t5_bucket(jnp.arange(BIAS_TABLE, dtype=jnp.int32))]
    table = jnp.transpose(table).reshape(MOBA_HEADS, 1, BIAS_TABLE).astype(F32)
    table = (table - table[:, :, BIAS_TABLE - 1:]) * LOG2E
    pmax_lanes = jnp.pad(pmax, (0, V7X_LANES - nblk)).reshape(1, V7X_LANES)
    qc, kc, vc = (c // MOBA_DH for c in (COL_MOBA_Q, COL_MOBA_K, COL_MOBA_V))
    grid_spec = pltpu.PrefetchScalarGridSpec(
        num_scalar_prefetch=2,
        grid=(MOBA_HEADS, nblk),
        in_specs=[
            pl.BlockSpec((blk, MOBA_DH), lambda h, i, *_: (i, qc + h)),
            pl.BlockSpec((s, MOBA_DH), lambda h, i, *_: (0, kc + h)),
            pl.BlockSpec((s, MOBA_DH), lambda h, i, *_: (0, vc + h)),
            pl.BlockSpec((blk, 1), lambda h, i, *_: (i, 0)),
            pl.BlockSpec((1, s), lambda h, i, *_: (0, 0)),
            pl.BlockSpec((1, V7X_LANES), lambda h, i, *_: (0, 0)),
            pl.BlockSpec((1, 1, BIAS_TABLE), lambda h, i, *_: (h, 0, 0)),
        ],
        out_specs=pl.BlockSpec((blk, MOBA_DH), lambda h, i, *_: (i, h)),
        scratch_shapes=[
            pltpu.VMEM((V7X_LANES, MOBA_DH), F32),
            pltpu.VMEM((s, 2 * MOBA_DH), BF16),
            pltpu.VMEM((MOBA_DH, s), BF16),
        ] + _softmax_scratch(MOBA_DH, blk, group * blk, keys_axis=0),
    )
    return pl.pallas_call(
        functools.partial(_moba_body, nblk=nblk, scale=MOBA_DH ** -0.5 * LOG2E, group=group),
        grid_spec=grid_spec,
        out_shape=jax.ShapeDtypeStruct((s, MOBA_HEADS * MOBA_DH), BF16),
        compiler_params=_params(("arbitrary", "arbitrary")),
        name="moba",
    )(pmin, pmax, proj, proj, proj, positions.reshape(s, 1), positions.reshape(1, s), pmax_lanes, table)


def _merge_body(oa_ref, ob_ref, ga_ref, gb_ref, wa_ref, wb_ref, wo_ref, x_ref, o_ref):
    @pl.when(pl.program_id(1) == 0)
    def _():
        o_ref[...] = x_ref[...]

    ma = jnp.dot(oa_ref[...], wa_ref[...], preferred_element_type=F32)
    mb = jnp.dot(ob_ref[...], wb_ref[...], preferred_element_type=F32)
    mg = _sigmoid(ga_ref[...].astype(F32)) * ma + _sigmoid(gb_ref[...].astype(F32)) * mb
    o_ref[...] += jnp.dot(mg.astype(BF16), wo_ref[...], preferred_element_type=F32)


def _merge(o_a, o_b, proj, w_a, w_b, w_out, x, *, tm=512, tn=512):
    s, d = x.shape
    ka, kb = o_a.shape[1], o_b.shape[1]
    ga, gb = COL_GATE_A // tn, COL_GATE_B // tn
    return pl.pallas_call(
        _merge_body,
        grid=(s // tm, d // tn),
        in_specs=[
            pl.BlockSpec((tm, ka), lambda i, j: (i, 0)),
            pl.BlockSpec((tm, kb), lambda i, j: (i, 0)),
            pl.BlockSpec((tm, tn), lambda i, j: (i, ga + j)),
            pl.BlockSpec((tm, tn), lambda i, j: (i, gb + j)),
            pl.BlockSpec((ka, tn), lambda i, j: (0, j)),
            pl.BlockSpec((kb, tn), lambda i, j: (0, j)),
            pl.BlockSpec((tn, d), lambda i, j: (j, 0)),
            _row_resident_spec(tm, d),
        ],
        out_specs=pl.BlockSpec((tm, d), lambda i, j: (i, 0)),
        out_shape=jax.ShapeDtypeStruct((s, d), F32),
        compiler_params=_params(("arbitrary", "arbitrary")),
        name="merge",
    )(o_a, o_b, proj, proj, w_a, w_b, w_out, x)


def _proj_layout(w_in):
    w_lat = jnp.pad(w_in[:, :LATENT_REAL].astype(BF16), ((0, 0), (0, LATENT_COLS - LATENT_REAL)))
    return w_lat, w_in[:, LATENT_REAL:].astype(BF16)


def _q_layout(w_uq):
    r = w_uq.shape[0]
    w = w_uq.reshape(r, MLA_HEADS, MLA_NOPE + MLA_ROPE)
    w = jnp.pad(w, ((0, 0), (0, 0), (0, MLA_QK_PAD - MLA_NOPE - MLA_ROPE)))
    return w.reshape(r, MLA_HEADS * MLA_QK_PAD).astype(BF16)


def kernel(x, positions, rel_bias, norm_ffn1, w_ffn1_in, w_ffn1_out, norm_mix, w_in, norm_cq, w_uq, norm_ckv, w_ukv, w_br_a, w_br_b, w_out, norm_ffn2, w_ffn2_in, w_ffn2_out, norm_final):
    b, s, d = x.shape
    assert b == 1 and norm_ffn1.shape[0] == 1
    pos = positions[0]
    x0 = x[0]

    x1 = _ffn(x0, norm_ffn1[0], w_ffn1_in[0].astype(BF16), w_ffn1_out[0].astype(BF16))
    w_lat, w_mix = _proj_layout(w_in[0])
    latent = _proj(x1, norm_mix[0], w_lat)
    mixer = _proj(x1, norm_mix[0], w_mix)
    tables = _rope_tables(pos)
    q = _q_up(latent, norm_cq[0], _q_layout(w_uq[0]), tables)
    k, v = _kv_up(latent, norm_ckv[0], w_ukv[0].astype(BF16), tables)
    o_a = _mla(q, k, v)
    o_b = _moba(mixer, pos, rel_bias)
    x2 = _merge(o_a, o_b, mixer, w_br_a[0].astype(BF16), w_br_b[0].astype(BF16), w_out[0].astype(BF16), x1)
    out = _ffn(x2, norm_ffn2[0], w_ffn2_in[0].astype(BF16), w_ffn2_out[0].astype(BF16), norm_final)
    return out[None]
```

```python
import functools
import math

import jax
import jax.numpy as jnp
from jax import lax
from jax.experimental import pallas as pl
from jax.experimental.pallas import tpu as pltpu

F32 = jnp.float32
BF16 = jnp.bfloat16

RMS_EPS = 1e-6
ROPE_THETA = 10000.0
LOG2E = math.log2(math.e)
MASKED = -1e30

MLA_HEADS = 16
MLA_Q_LORA = 1024
MLA_KV_LORA = 512
MLA_NOPE = 128
MLA_ROPE = 64
MLA_V = 128
MLA_QK_PAD = 256

MOBA_HEADS = 16
MOBA_DH = 128
MOBA_BLOCK = 256
MOBA_TOPK = 3
REL_BUCKETS = 32
REL_MAX_DIST = 128
BIAS_TABLE = 128

V7X_LANES = 128
V7X_VMEM_LIMIT = 56 * 1024 * 1024

LATENT_REAL = MLA_Q_LORA + MLA_KV_LORA + MLA_ROPE
LATENT_COLS = 2048
COL_CQ = 0
COL_CKV = 1024
COL_KROPE = 1536
COL_MOBA_Q = 0
COL_MOBA_K = 2048
COL_MOBA_V = 4096
COL_GATE_A = 6144
COL_GATE_B = 10240

_NT = (((1,), (1,)), ((), ()))


def _params(sem):
    return pltpu.CompilerParams(dimension_semantics=sem, vmem_limit_bytes=V7X_VMEM_LIMIT)


def _row_resident_spec(tm, d):
    return pl.BlockSpec((tm, d), lambda i, j: (i, 0), pipeline_mode=pl.Buffered(1))


def _rms(xf, gain):
    ms = jnp.mean(xf * xf, axis=-1, keepdims=True)
    return xf * lax.rsqrt(ms + RMS_EPS) * gain


def _sigmoid(z):
    return 1.0 / (1.0 + jnp.exp(-z))


def _ffn_body(x_ref, g_ref, wg_ref, wu_ref, wo_ref, *rest, final_norm):
    if final_norm:
        gf_ref, o_ref, h_ref = rest
    else:
        o_ref, h_ref = rest
    j = pl.program_id(1)

    @pl.when(j == 0)
    def _():
        xf = x_ref[...]
        h_ref[...] = _rms(xf, g_ref[...]).astype(BF16)
        o_ref[...] = xf

    h = h_ref[...]
    g = jnp.dot(h, wg_ref[...], preferred_element_type=F32)
    u = jnp.dot(h, wu_ref[...], preferred_element_type=F32)
    a = (g * _sigmoid(g) * (0.5 * u)).astype(BF16)
    o_ref[...] += jnp.dot(a, wo_ref[...], preferred_element_type=F32)

    if final_norm:
        @pl.when(j == pl.num_programs(1) - 1)
        def _():
            o_ref[...] = _rms(o_ref[...], gf_ref[...])


def _ffn(x, gain, w_in, w_out, final_gain=None, *, tm=512, tf=256):
    s, d = x.shape
    f = w_out.shape[0]
    nf = f // tf
    assert s % tm == 0 and f % tf == 0 and w_in.shape == (d, 2 * f)
    in_specs = [
        _row_resident_spec(tm, d),
        pl.BlockSpec((1, d), lambda i, j: (0, 0)),
        pl.BlockSpec((d, tf), lambda i, j: (0, j)),
        pl.BlockSpec((d, tf), lambda i, j: (0, j + nf)),
        pl.BlockSpec((tf, d), lambda i, j: (j, 0)),
    ]
    args = [x, gain.reshape(1, d), w_in, w_in, w_out]
    if final_gain is not None:
        in_specs.append(pl.BlockSpec((1, d), lambda i, j: (0, 0)))
        args.append(final_gain.reshape(1, d))
    return pl.pallas_call(
        functools.partial(_ffn_body, final_norm=final_gain is not None),
        grid=(s // tm, nf),
        in_specs=in_specs,
        out_specs=pl.BlockSpec((tm, d), lambda i, j: (i, 0)),
        out_shape=jax.ShapeDtypeStruct((s, d), F32),
        scratch_shapes=[pltpu.VMEM((tm, d), BF16)],
        compiler_params=_params(("arbitrary", "arbitrary")),
        name="ffn",
    )(*args)


def _proj_body(x_ref, g_ref, w_ref, o_ref, h_ref):
    @pl.when(pl.program_id(1) == 0)
    def _():
        h_ref[...] = _rms(x_ref[...], g_ref[...]).astype(BF16)

    o_ref[...] = jnp.dot(h_ref[...], w_ref[...], preferred_element_type=F32).astype(o_ref.dtype)


def _proj(x, gain, w, *, tm=512, tn=1024):
    s, d = x.shape
    n = w.shape[1]
    assert s % tm == 0 and n % tn == 0
    return pl.pallas_call(
        _proj_body,
        grid=(s // tm, n // tn),
        in_specs=[
            _row_resident_spec(tm, d),
            pl.BlockSpec((1, d), lambda i, j: (0, 0)),
            pl.BlockSpec((d, tn), lambda i, j: (0, j)),
        ],
        out_specs=pl.BlockSpec((tm, tn), lambda i, j: (i, j)),
        out_shape=jax.ShapeDtypeStruct((s, n), BF16),
        scratch_shapes=[pltpu.VMEM((tm, d), BF16)],
        compiler_params=_params(("arbitrary", "arbitrary")),
        name="proj",
    )(x, gain.reshape(1, d), w)


def _rope_lanes(r, cf, s1, s2):
    return r * cf + pltpu.roll(r, 96, 1) * s1 + pltpu.roll(r, 32, 1) * s2


def _rope_tables(positions):
    half = MLA_ROPE // 2
    inv_freq = ROPE_THETA ** (-jnp.arange(0, MLA_ROPE, 2, dtype=F32) / MLA_ROPE)
    ang = positions[:, None].astype(F32) * inv_freq
    cos, sin = jnp.cos(ang), jnp.sin(ang)
    z = jnp.zeros_like(cos)
    pad = jnp.zeros((positions.shape[0], V7X_LANES - MLA_ROPE), F32)
    cf = jnp.concatenate([cos, cos, pad], axis=1)
    s1 = jnp.concatenate([-sin, z, pad], axis=1)
    s2 = jnp.concatenate([z, sin, pad], axis=1)
    return cf, s1, s2


def _qup_body(c_ref, g_ref, w_ref, cf_ref, s1_ref, s2_ref, q_ref, h_ref, *, scale):
    @pl.when(pl.program_id(1) == 0)
    def _():
        h_ref[...] = _rms(c_ref[...].astype(F32), g_ref[...]).astype(BF16)

    y = jnp.dot(h_ref[...], w_ref[...], preferred_element_type=F32)
    for hh in range(q_ref.shape[0]):
        c0 = hh * MLA_QK_PAD
        q_ref[hh, :, :MLA_NOPE] = (y[:, c0:c0 + MLA_NOPE] * scale).astype(q_ref.dtype)
        r = _rope_lanes(y[:, c0 + MLA_NOPE:c0 + MLA_QK_PAD], cf_ref[...], s1_ref[...], s2_ref[...])
        q_ref[hh, :, MLA_NOPE:] = (r * scale).astype(q_ref.dtype)


def _q_up(proj, gain, w_q, tables, *, tm=1024, hg=4):
    s = proj.shape[0]
    cf, s1, s2 = tables
    scale = (MLA_NOPE + MLA_ROPE) ** -0.5 * LOG2E
    tab_spec = pl.BlockSpec((tm, V7X_LANES), lambda i, h: (i, 0))
    return pl.pallas_call(
        functools.partial(_qup_body, scale=scale),
        grid=(s // tm, MLA_HEADS // hg),
        in_specs=[
            pl.BlockSpec((tm, MLA_Q_LORA), lambda i, h: (i, COL_CQ // MLA_Q_LORA)),
            pl.BlockSpec((1, MLA_Q_LORA), lambda i, h: (0, 0)),
            pl.BlockSpec((MLA_Q_LORA, hg * MLA_QK_PAD), lambda i, h: (0, h)),
            tab_spec, tab_spec, tab_spec,
        ],
        out_specs=pl.BlockSpec((hg, tm, MLA_QK_PAD), lambda i, h: (h, i, 0)),
        out_shape=jax.ShapeDtypeStruct((MLA_HEADS, s, MLA_QK_PAD), BF16),
        scratch_shapes=[pltpu.VMEM((tm, MLA_Q_LORA), BF16)],
        compiler_params=_params(("arbitrary", "arbitrary")),
        name="q_up",
    )(proj, gain.reshape(1, MLA_Q_LORA), w_q, cf, s1, s2)


def _kvup_body(c_ref, kr_ref, g_ref, w_ref, cf_ref, s1_ref, s2_ref, k_ref, v_ref, h_ref, r_ref):
    @pl.when(pl.program_id(1) == 0)
    def _():
        h_ref[...] = _rms(c_ref[...].astype(F32), g_ref[...]).astype(BF16)
        r_ref[...] = _rope_lanes(kr_ref[...].astype(F32), cf_ref[...], s1_ref[...], s2_ref[...]).astype(BF16)

    y = jnp.dot(h_ref[...], w_ref[...], preferred_element_type=F32)
    for hh in range(k_ref.shape[0]):
        c0 = hh * (MLA_NOPE + MLA_V)
        k_ref[hh, :, :MLA_NOPE] = y[:, c0:c0 + MLA_NOPE].astype(k_ref.dtype)
        k_ref[hh, :, MLA_NOPE:] = r_ref[...]
        v_ref[hh] = y[:, c0 + MLA_NOPE:c0 + MLA_NOPE + MLA_V].astype(v_ref.dtype)


def _kv_up(proj, gain, w_ukv, tables, *, tm=1024, hg=4):
    s = proj.shape[0]
    cf, s1, s2 = tables
    tab_spec = pl.BlockSpec((tm, V7X_LANES), lambda i, h: (i, 0))
    return pl.pallas_call(
        _kvup_body,
        grid=(s // tm, MLA_HEADS // hg),
        in_specs=[
            pl.BlockSpec((tm, MLA_KV_LORA), lambda i, h: (i, COL_CKV // MLA_KV_LORA)),
            pl.BlockSpec((tm, V7X_LANES), lambda i, h: (i, COL_KROPE // V7X_LANES)),
            pl.BlockSpec((1, MLA_KV_LORA), lambda i, h: (0, 0)),
            pl.BlockSpec((MLA_KV_LORA, hg * (MLA_NOPE + MLA_V)), lambda i, h: (0, h)),
            tab_spec, tab_spec, tab_spec,
        ],
        out_specs=[
            pl.BlockSpec((hg, tm, MLA_QK_PAD), lambda i, h: (h, i, 0)),
            pl.BlockSpec((hg, tm, MLA_V), lambda i, h: (h, i, 0)),
        ],
        out_shape=[
            jax.ShapeDtypeStruct((MLA_HEADS, s, MLA_QK_PAD), BF16),
            jax.ShapeDtypeStruct((MLA_HEADS, s, MLA_V), BF16),
        ],
        scratch_shapes=[pltpu.VMEM((tm, MLA_KV_LORA), BF16), pltpu.VMEM((tm, V7X_LANES), BF16)],
        compiler_params=_params(("arbitrary", "arbitrary")),
        name="kv_up",
    )(proj, proj, gain.reshape(1, MLA_KV_LORA), w_ukv, cf, s1, s2)


def _pv(values, p, keys_axis):
    if keys_axis == 0:
        return jnp.dot(values, p, preferred_element_type=F32)
    return jnp.dot(p, values, preferred_element_type=F32)


def _softmax_step(s, values, m, l, acc, keys_axis):
    m_new = jnp.maximum(m, jnp.max(s, axis=keys_axis, keepdims=True))
    alpha = jnp.exp2(m - m_new)
    p = jnp.exp2(s - m_new)
    l = alpha * l + jnp.sum(p, axis=keys_axis, keepdims=True)
    acc = alpha * acc + _pv(values, p.astype(values.dtype), keys_axis)
    return m_new, l, acc


def _softmax_chunks(scores_fn, values_fn, last_chunk, state, s_ref, p_ref, keys_axis):
    m_ref, l_ref, acc_ref, a_ref, cmax_ref = state

    def flush(g, slot):
        pv = _pv(values_fn(jnp.maximum(g, 0)), p_ref[slot], keys_axis)
        acc_ref[slot] = a_ref[slot] * acc_ref[slot] + pv

    def produce(g, slot):
        s = scores_fn(jnp.minimum(g, last_chunk))
        s_ref[slot] = s
        cmax_ref[slot] = jnp.max(s, axis=keys_axis, keepdims=True)

    def step(g, cur, produce_next=True):
        m = m_ref[cur]
        m_new = jnp.maximum(m, cmax_ref[cur])
        alpha = jnp.exp2(m - m_new)
        p = jnp.exp2(s_ref[cur] - m_new)
        l_ref[cur] = alpha * l_ref[cur] + jnp.sum(p, axis=keys_axis, keepdims=True)
        p_ref[cur] = p.astype(p_ref.dtype)
        m_ref[cur] = m_new
        a_ref[cur] = alpha
        if produce_next:
            produce(g + 1, 1 - cur)
        flush(g - 1, 1 - cur)

    def begin():
        produce(0, 0)
        p_ref[1] = jnp.zeros(p_ref.shape[1:], p_ref.dtype)
        a_ref[0] = jnp.ones(a_ref.shape[1:], a_ref.dtype)
        a_ref[1] = jnp.ones(a_ref.shape[1:], a_ref.dtype)
        m_ref[1] = jnp.full(m_ref.shape[1:], -jnp.inf, F32)
        l_ref[1] = jnp.zeros(l_ref.shape[1:], F32)
        acc_ref[1] = jnp.zeros(acc_ref.shape[1:], F32)

    def run(n):
        def pair(k, c):
            step(2 * k, 0)
            step(2 * k + 1, 1)
            return c

        lax.fori_loop(0, n // 2, pair, 0)

        @pl.when(n % 2 == 1)
        def _():
            step(n - 1, 0, produce_next=False)
            flush(n - 1, 0)

        @pl.when(n % 2 == 0)
        def _():
            flush(n - 1, 1)

        m = jnp.maximum(m_ref[0], m_ref[1])
        w0, w1 = jnp.exp2(m_ref[0] - m), jnp.exp2(m_ref[1] - m)
        return m, w0 * l_ref[0] + w1 * l_ref[1], w0 * acc_ref[0] + w1 * acc_ref[1]

    return begin, run


class _Slots:
    def __init__(self, ref0, ref1):
        self.refs = (ref0, ref1)
        self.shape = (2,) + tuple(ref0.shape)
        self.dtype = ref0.dtype

    def __getitem__(self, slot):
        return self.refs[slot][...]

    def __setitem__(self, slot, value):
        self.refs[slot][...] = value


def _softmax_scratch(dv, tq, span, keys_axis):
    stat, acc, tile = ((1, tq), (dv, tq), (span, tq)) if keys_axis == 0 else ((tq, 1), (tq, dv), (tq, span))
    kinds = ((stat, F32), (stat, F32), (acc, F32), (stat, F32), (stat, F32), (tile, F32), (tile, BF16))
    return [pltpu.VMEM(shape, dtype) for shape, dtype in kinds for _ in range(2)]


def _softmax_slots(refs):
    assert len(refs) == 14
    return [_Slots(refs[2 * k], refs[2 * k + 1]) for k in range(7)]


def _mla_body(q_ref, k_ref, v_ref, o_ref, *, t, parts):
    i = pl.program_id(1)
    rows = t // parts
    qs = [q_ref[0, r * rows:(r + 1) * rows, :] for r in range(parts)]

    def keys(n):
        return k_ref[0, pl.ds(pl.multiple_of(n * t, t), t), :]

    def values(n):
        return v_ref[0, pl.ds(pl.multiple_of(n * t, t), t), :]

    def body(n, c):
        k, v = keys(n), values(n)
        return tuple(
            _softmax_step(lax.dot_general(qs[r], k, _NT, preferred_element_type=F32), v, *c[r], keys_axis=1)
            for r in range(parts))

    init = tuple((jnp.full((rows, 1), -jnp.inf, F32), jnp.zeros((rows, 1), F32), jnp.zeros((rows, MLA_V), F32))
                 for _ in range(parts))
    state = lax.fori_loop(0, i, body, init)
    k, v = keys(i), values(i)
    col = lax.broadcasted_iota(jnp.int32, (rows, t), 1)
    for r in range(parts):
        row = r * rows + lax.broadcasted_iota(jnp.int32, (rows, t), 0)
        s = jnp.where(col <= row, lax.dot_general(qs[r], k, _NT, preferred_element_type=F32), -jnp.inf)
        m, l, acc = _softmax_step(s, v, *state[r], keys_axis=1)
        o_ref[r * rows:(r + 1) * rows, :] = (acc / l).astype(o_ref.dtype)


def _mla(q, k, v, *, t=1024, parts=2):
    h, s, _ = q.shape
    assert s % t == 0 and t % parts == 0
    return pl.pallas_call(
        functools.partial(_mla_body, t=t, parts=parts),
        grid=(h, s // t),
        in_specs=[
            pl.BlockSpec((1, t, MLA_QK_PAD), lambda h, i: (h, i, 0)),
            pl.BlockSpec((1, s, MLA_QK_PAD), lambda h, i: (h, 0, 0)),
            pl.BlockSpec((1, s, MLA_V), lambda h, i: (h, 0, 0)),
        ],
        out_specs=pl.BlockSpec((t, MLA_V), lambda h, i: (i, h)),
        out_shape=jax.ShapeDtypeStruct((s, h * MLA_V), BF16),
        compiler_params=_params(("arbitrary", "arbitrary")),
        name="mla",
    )(q, k, v)


def _t5_bucket(n):
    max_exact = REL_BUCKETS // 2
    n_f = jnp.maximum(n, max_exact).astype(F32)
    large = max_exact + (jnp.log(n_f / max_exact) / math.log(REL_MAX_DIST / max_exact)
                         * (REL_BUCKETS - max_exact)).astype(jnp.int32)
    large = jnp.minimum(large, REL_BUCKETS - 1)
    return jnp.where(n < max_exact, n, large)


def _moba_body(pmin_ref, pmax_ref, q_ref, k_ref, v_ref, pq_ref, pk_ref, pmaxl_ref, tbl_ref, o_ref,
               kmean_ref, kaug_ref, vt_ref, *softmax_refs, nblk, scale, group):
    i = pl.program_id(1)
    blk = MOBA_BLOCK
    lane_id = lax.broadcasted_iota(jnp.int32, (blk, V7X_LANES), 1)
    m_ref, l_ref, acc_ref, a_ref, cmax_ref, s_ref, p_ref = _softmax_slots(softmax_refs)

    @pl.when(i == 0)
    def _():
        kmean_ref[...] = jnp.zeros_like(kmean_ref)

        def fill(n, c):
            rows = pl.ds(pl.multiple_of(n * blk, blk), blk)
            kb = k_ref[rows, :]
            kmean_ref[pl.ds(n, 1), :] = jnp.sum(kb.astype(F32), axis=0, keepdims=True) * (1.0 / blk)
            kaug_ref[rows, :MOBA_DH] = kb
            kaug_ref[rows, MOBA_DH:] = jnp.where(lane_id == n, 1.0, 0.0).astype(kaug_ref.dtype)
            vt_ref[:, rows] = v_ref[rows, :].astype(F32).T.astype(vt_ref.dtype)
            return c

        lax.fori_loop(0, nblk, fill, 0)

    q = q_ref[...]
    qs = (q.astype(F32) * scale).astype(q.dtype)

    gate = lax.dot_general(kmean_ref[...], q.astype(F32), _NT, preferred_element_type=F32)
    bid = lax.broadcasted_iota(jnp.int32, gate.shape, 0)
    bid_f = bid.astype(F32)
    g = jnp.where(bid < i, gate, -jnp.inf)
    picked_t = jnp.zeros(gate.shape, F32)
    for _ in range(min(MOBA_TOPK, nblk)):
        mx = jnp.max(g, axis=0, keepdims=True)
        idx = jnp.min(jnp.where(g == mx, bid_f, float(V7X_LANES)), axis=0, keepdims=True)
        hit = bid_f == jnp.where(mx > -jnp.inf, idx, -1.0)
        picked_t = jnp.where(hit, 1.0, picked_t)
        g = jnp.where(hit, -jnp.inf, g)
    picked = picked_t.T

    prev = jnp.maximum(i - 1, 0)
    far_lanes = jnp.logical_and((pmin_ref[i] - pmaxl_ref[...]) >= BIAS_TABLE - 1, lane_id[:1] != prev)
    m_any = jnp.where(picked > 0.0, 0.0, MASKED)
    q_any = jnp.concatenate([qs, m_any.astype(qs.dtype)], axis=1)
    q_far = jnp.concatenate([qs, jnp.where(far_lanes, m_any, MASKED).astype(qs.dtype)], axis=1)
    q_diag = jnp.concatenate([qs, jnp.where(lane_id == i, 0.0, m_any).astype(qs.dtype)], axis=1)

    def is_far(n):
        return (pmin_ref[i] - pmax_ref[n]) >= BIAS_TABLE - 1

    def masked_scores(off, width, q_aug):
        return lax.dot_general(kaug_ref[pl.ds(off, width), :], q_aug, _NT, preferred_element_type=F32)

    def near_bias(off):
        off = pl.multiple_of(off, blk)
        d = jnp.clip(pq_ref[...] - pk_ref[:, pl.ds(off, blk)], 0, BIAS_TABLE - 1)
        tb = jnp.broadcast_to(tbl_ref[0], (blk, BIAS_TABLE))
        parts = [jnp.take_along_axis(tb, d[:, c * V7X_LANES:(c + 1) * V7X_LANES], axis=1)
                 for c in range(blk // V7X_LANES)]
        return jnp.concatenate(parts, axis=1).T

    span = group * blk

    def far_scores(g):
        return masked_scores(pl.multiple_of(g * span, span), span, q_far)

    def far_values_t(g):
        return vt_ref[:, pl.ds(pl.multiple_of(g * span, span), span)]

    far_begin, far_run = _softmax_chunks(far_scores, far_values_t, nblk // group - 1,
                                         (m_ref, l_ref, acc_ref, a_ref, cmax_ref), s_ref, p_ref, keys_axis=0)
    off_d = pl.multiple_of(prev * blk, blk)
    key = prev * blk + lax.broadcasted_iota(jnp.int32, (2 * blk, blk), 0)
    qry = i * blk + lax.broadcasted_iota(jnp.int32, (2 * blk, blk), 1)
    s = masked_scores(off_d, 2 * blk, q_diag)
    far_begin()
    s = s + jnp.concatenate([near_bias(off_d), near_bias(off_d + blk)], axis=0)
    s = jnp.where(key <= qry, s, -jnp.inf)
    m = jnp.max(s, axis=0, keepdims=True)
    p = jnp.exp2(s - m)
    m_ref[0] = m
    l_ref[0] = jnp.sum(p, axis=0, keepdims=True)
    acc_ref[0] = jnp.dot(vt_ref[:, pl.ds(off_d, 2 * blk)], p.astype(vt_ref.dtype), preferred_element_type=F32)
    m_ref[0], l_ref[0], acc_ref[0] = far_run((i + group - 1) // group)

    def near_block(n, c):
        @pl.when(jnp.logical_not(is_far(n)))
        def _():
            off = pl.multiple_of(n * blk, blk)
            s = masked_scores(off, blk, q_any) + near_bias(off)
            m_ref[0], l_ref[0], acc_ref[0] = _softmax_step(
                s, vt_ref[:, pl.ds(off, blk)], m_ref[0], l_ref[0], acc_ref[0], keys_axis=0)

        return c

    lax.fori_loop(0, prev, near_block, 0)
    o_ref[...] = (acc_ref[0] / l_ref[0]).T.astype(o_ref.dtype)


def _moba(proj, positions, rel_bias, *, group=4):
    s = proj.shape[0]
    blk = MOBA_BLOCK
    nblk = s // blk
    assert s % blk == 0 and 2 <= nblk <= V7X_LANES and nblk % group == 0
    pos_blocks = positions.reshape(nblk, blk)
    pmin = jnp.min(pos_blocks, axis=1)
    pmax = jnp.max(pos_blocks, axis=1)
    table = rel_bias[_t5_bucket(jnp.arange(BIAS_TABLE, dtype=jnp.int32))]
    table = jnp.transpose(table).reshape(MOBA_HEADS, 1, BIAS_TABLE).astype(F32)
    table = (table - table[:, :, BIAS_TABLE - 1:]) * LOG2E
    pmax_lanes = jnp.pad(pmax, (0, V7X_LANES - nblk)).reshape(1, V7X_LANES)
    qc, kc, vc = (c // MOBA_DH for c in (COL_MOBA_Q, COL_MOBA_K, COL_MOBA_V))
    grid_spec = pltpu.PrefetchScalarGridSpec(
        num_scalar_prefetch=2,
        grid=(MOBA_HEADS, nblk),
        in_specs=[
            pl.BlockSpec((blk, MOBA_DH), lambda h, i, *_: (i, qc + h)),
            pl.BlockSpec((s, MOBA_DH), lambda h, i, *_: (0, kc + h)),
            pl.BlockSpec((s, MOBA_DH), lambda h, i, *_: (0, vc + h)),
            pl.BlockSpec((blk, 1), lambda h, i, *_: (i, 0)),
            pl.BlockSpec((1, s), lambda h, i, *_: (0, 0)),
            pl.BlockSpec((1, V7X_LANES), lambda h, i, *_: (0, 0)),
            pl.BlockSpec((1, 1, BIAS_TABLE), lambda h, i, *_: (h, 0, 0)),
        ],
        out_specs=pl.BlockSpec((blk, MOBA_DH), lambda h, i, *_: (i, h)),
        scratch_shapes=[
            pltpu.VMEM((V7X_LANES, MOBA_DH), F32),
            pltpu.VMEM((s, 2 * MOBA_DH), BF16),
            pltpu.VMEM((MOBA_DH, s), BF16),
        ] + _softmax_scratch(MOBA_DH, blk, group * blk, keys_axis=0),
    )
    return pl.pallas_call(
        functools.partial(_moba_body, nblk=nblk, scale=MOBA_DH ** -0.5 * LOG2E, group=group),
        grid_spec=grid_spec,
        out_shape=jax.ShapeDtypeStruct((s, MOBA_HEADS * MOBA_DH), BF16),
        compiler_params=_params(("arbitrary", "arbitrary")),
        name="moba",
    )(pmin, pmax, proj, proj, proj, positions.reshape(s, 1), positions.reshape(1, s), pmax_lanes, table)


def _merge_body(oa_ref, ob_ref, ga_ref, gb_ref, wa_ref, wb_ref, wo_ref, x_ref, o_ref):
    @pl.when(pl.program_id(1) == 0)
    def _():
        o_ref[...] = x_ref[...]

    ma = jnp.dot(oa_ref[...], wa_ref[...], preferred_element_type=F32)
    mb = jnp.dot(ob_ref[...], wb_ref[...], preferred_element_type=F32)
    mg = _sigmoid(ga_ref[...].astype(F32)) * ma + _sigmoid(gb_ref[...].astype(F32)) * mb
    o_ref[...] += jnp.dot(mg.astype(BF16), wo_ref[...], preferred_element_type=F32)


def _merge(o_a, o_b, proj, w_a, w_b, w_out, x, *, tm=512, tn=512):
    s, d = x.shape
    ka, kb = o_a.shape[1], o_b.shape[1]
    ga, gb = COL_GATE_A // tn, COL_GATE_B // tn
    return pl.pallas_call(
        _merge_body,
        grid=(s // tm, d // tn),
        in_specs=[
            pl.BlockSpec((tm, ka), lambda i, j: (i, 0)),
            pl.BlockSpec((tm, kb), lambda i, j: (i, 0)),
            pl.BlockSpec((tm, tn), lambda i, j: (i, ga + j)),
            pl.BlockSpec((tm, tn), lambda i, j: (i, gb + j)),
            pl.BlockSpec((ka, tn), lambda i, j: (0, j)),
            pl.BlockSpec((kb, tn), lambda i, j: (0, j)),
            pl.BlockSpec((tn, d), lambda i, j: (j, 0)),
            _row_resident_spec(tm, d),
        ],
        out_specs=pl.BlockSpec((tm, d), lambda i, j: (i, 0)),
        out_shape=jax.ShapeDtypeStruct((s, d), F32),
        compiler_params=_params(("arbitrary", "arbitrary")),
        name="merge",
    )(o_a, o_b, proj, proj, w_a, w_b, w_out, x)


def _proj_layout(w_in):
    w_lat = jnp.pad(w_in[:, :LATENT_REAL].astype(BF16), ((0, 0), (0, LATENT_COLS - LATENT_REAL)))
    return w_lat, w_in[:, LATENT_REAL:].astype(BF16)


def _q_layout(w_uq):
    r = w_uq.shape[0]
    w = w_uq.reshape(r, MLA_HEADS, MLA_NOPE + MLA_ROPE)
    w = jnp.pad(w, ((0, 0), (0, 0), (0, MLA_QK_PAD - MLA_NOPE - MLA_ROPE)))
    return w.reshape(r, MLA_HEADS * MLA_QK_PAD).astype(BF16)


def kernel(x, positions, rel_bias, norm_ffn1, w_ffn1_in, w_ffn1_out, norm_mix, w_in, norm_cq, w_uq, norm_ckv, w_ukv, w_br_a, w_br_b, w_out, norm_ffn2, w_ffn2_in, w_ffn2_out, norm_final):
    b, s, d = x.shape
    assert b == 1 and norm_ffn1.shape[0] == 1
    pos = positions[0]
    x0 = x[0]

    x1 = _ffn(x0, norm_ffn1[0], w_ffn1_in[0].astype(BF16), w_ffn1_out[0].astype(BF16))
    w_lat, w_mix = _proj_layout(w_in[0])
    latent = _proj(x1, norm_mix[0], w_lat)
    mixer = _proj(x1, norm_mix[0], w_mix)
    tables = _rope_tables(pos)
    q = _q_up(latent, norm_cq[0], _q_layout(w_uq[0]), tables)
    k, v = _kv_up(latent, norm_ckv[0], w_ukv[0].astype(BF16), tables)
    o_a = _mla(q, k, v)
    o_b = _moba(mixer, pos, rel_bias)
    x2 = _merge(o_a, o_b, mixer, w_br_a[0].astype(BF16), w_br_b[0].astype(BF16), w_out[0].astype(BF16), x1)
    out = _ffn(x2, norm_ffn2[0], w_ffn2_in[0].astype(BF16), w_ffn2_out[0].astype(BF16), norm_final)
    return out[None]
```

```python
import functools
import math

import jax
import jax.numpy as jnp
from jax import lax
from jax.experimental import pallas as pl
from jax.experimental.pallas import tpu as pltpu

F32 = jnp.float32
BF16 = jnp.bfloat16

RMS_EPS = 1e-6
ROPE_THETA = 10000.0
LOG2E = math.log2(math.e)
MASKED = -1e30

MLA_HEADS = 16
MLA_Q_LORA = 1024
MLA_KV_LORA = 512
MLA_NOPE = 128
MLA_ROPE = 64
MLA_V = 128
MLA_QK_PAD = 256

MOBA_HEADS = 16
MOBA_DH = 128
MOBA_BLOCK = 256
MOBA_TOPK = 3
REL_BUCKETS = 32
REL_MAX_DIST = 128
BIAS_TABLE = 128

V7X_LANES = 128
V7X_VMEM_LIMIT = 56 * 1024 * 1024

LATENT_REAL = MLA_Q_LORA + MLA_KV_LORA + MLA_ROPE
LATENT_COLS = 2048
COL_CQ = 0
COL_CKV = 1024
COL_KROPE = 1536
COL_MOBA_Q = 0
COL_MOBA_K = 2048
COL_MOBA_V = 4096
COL_GATE_A = 6144
COL_GATE_B = 10240

_NT = (((1,), (1,)), ((), ()))


def _params(sem):
    return pltpu.CompilerParams(dimension_semantics=sem, vmem_limit_bytes=V7X_VMEM_LIMIT)


def _row_resident_spec(tm, d):
    return pl.BlockSpec((tm, d), lambda i, j: (i, 0), pipeline_mode=pl.Buffered(1))


def _rms(xf, gain):
    ms = jnp.mean(xf * xf, axis=-1, keepdims=True)
    return xf * lax.rsqrt(ms + RMS_EPS) * gain


def _sigmoid(z):
    return 1.0 / (1.0 + jnp.exp(-z))


def _ffn_body(x_ref, g_ref, wg_ref, wu_ref, wo_ref, *rest, final_norm):
    if final_norm:
        gf_ref, o_ref, h_ref = rest
    else:
        o_ref, h_ref = rest
    j = pl.program_id(1)

    @pl.when(j == 0)
    def _():
        xf = x_ref[...]
        h_ref[...] = _rms(xf, g_ref[...]).astype(BF16)
        o_ref[...] = xf

    h = h_ref[...]
    g = jnp.dot(h, wg_ref[...], preferred_element_type=F32)
    u = jnp.dot(h, wu_ref[...], preferred_element_type=F32)
    a = (g * _sigmoid(g) * (0.5 * u)).astype(BF16)
    o_ref[...] += jnp.dot(a, wo_ref[...], preferred_element_type=F32)

    if final_norm:
        @pl.when(j == pl.num_programs(1) - 1)
        def _():
            o_ref[...] = _rms(o_ref[...], gf_ref[...])


def _ffn(x, gain, w_in, w_out, final_gain=None, *, tm=512, tf=256):
    s, d = x.shape
    f = w_out.shape[0]
    nf = f // tf
    assert s % tm == 0 and f % tf == 0 and w_in.shape == (d, 2 * f)
    in_specs = [
        _row_resident_spec(tm, d),
        pl.BlockSpec((1, d), lambda i, j: (0, 0)),
        pl.BlockSpec((d, tf), lambda i, j: (0, j)),
        pl.BlockSpec((d, tf), lambda i, j: (0, j + nf)),
        pl.BlockSpec((tf, d), lambda i, j: (j, 0)),
    ]
    args = [x, gain.reshape(1, d), w_in, w_in, w_out]
    if final_gain is not None:
        in_specs.append(pl.BlockSpec((1, d), lambda i, j: (0, 0)))
        args.append(final_gain.reshape(1, d))
    return pl.pallas_call(
        functools.partial(_ffn_body, final_norm=final_gain is not None),
        grid=(s // tm, nf),
        in_specs=in_specs,
        out_specs=pl.BlockSpec((tm, d), lambda i, j: (i, 0)),
        out_shape=jax.ShapeDtypeStruct((s, d), F32),
        scratch_shapes=[pltpu.VMEM((tm, d), BF16)],
        compiler_params=_params(("arbitrary", "arbitrary")),
        name="ffn",
    )(*args)


def _proj_body(x_ref, g_ref, wa_ref, wb_ref, oa_ref, ob_ref, h_ref, *, na):
    j = pl.program_id(1)

    @pl.when(j == 0)
    def _():
        h_ref[...] = _rms(x_ref[...], g_ref[...]).astype(BF16)

    @pl.when(j < na)
    def _():
        oa_ref[...] = jnp.dot(h_ref[...], wa_ref[...], preferred_element_type=F32).astype(oa_ref.dtype)

    @pl.when(j >= na)
    def _():
        ob_ref[...] = jnp.dot(h_ref[...], wb_ref[...], preferred_element_type=F32).astype(ob_ref.dtype)


def _proj(x, gain, w_a, w_b, *, tm=512, tn=1024):
    s, d = x.shape
    na, nb = w_a.shape[1] // tn, w_b.shape[1] // tn
    assert s % tm == 0 and w_a.shape[1] % tn == 0 and w_b.shape[1] % tn == 0
    col_a = lambda j: jnp.minimum(j, na - 1)
    col_b = lambda j: jnp.maximum(j - na, 0)
    return pl.pallas_call(
        functools.partial(_proj_body, na=na),
        grid=(s // tm, na + nb),
        in_specs=[
            _row_resident_spec(tm, d),
            pl.BlockSpec((1, d), lambda i, j: (0, 0)),
            pl.BlockSpec((d, tn), lambda i, j: (0, col_a(j))),
            pl.BlockSpec((d, tn), lambda i, j: (0, col_b(j))),
        ],
        out_specs=[
            pl.BlockSpec((tm, tn), lambda i, j: (i, col_a(j))),
            pl.BlockSpec((tm, tn), lambda i, j: (i, col_b(j))),
        ],
        out_shape=[jax.ShapeDtypeStruct((s, w_a.shape[1]), BF16), jax.ShapeDtypeStruct((s, w_b.shape[1]), BF16)],
        scratch_shapes=[pltpu.VMEM((tm, d), BF16)],
        compiler_params=_params(("arbitrary", "arbitrary")),
        name="proj",
    )(x, gain.reshape(1, d), w_a, w_b)


def _rope_lanes(r, cf, s1, s2):
    return r * cf + pltpu.roll(r, 96, 1) * s1 + pltpu.roll(r, 32, 1) * s2


def _rope_tables(positions):
    half = MLA_ROPE // 2
    inv_freq = ROPE_THETA ** (-jnp.arange(0, MLA_ROPE, 2, dtype=F32) / MLA_ROPE)
    ang = positions[:, None].astype(F32) * inv_freq
    cos, sin = jnp.cos(ang), jnp.sin(ang)
    z = jnp.zeros_like(cos)
    pad = jnp.zeros((positions.shape[0], V7X_LANES - MLA_ROPE), F32)
    cf = jnp.concatenate([cos, cos, pad], axis=1)
    s1 = jnp.concatenate([-sin, z, pad], axis=1)
    s2 = jnp.concatenate([z, sin, pad], axis=1)
    return cf, s1, s2


def _qup_body(c_ref, g_ref, w_ref, cf_ref, s1_ref, s2_ref, q_ref, h_ref, *, scale):
    @pl.when(pl.program_id(1) == 0)
    def _():
        h_ref[...] = _rms(c_ref[...].astype(F32), g_ref[...]).astype(BF16)

    y = jnp.dot(h_ref[...], w_ref[...], preferred_element_type=F32)
    for hh in range(q_ref.shape[0]):
        c0 = hh * MLA_QK_PAD
        q_ref[hh, :, :MLA_NOPE] = (y[:, c0:c0 + MLA_NOPE] * scale).astype(q_ref.dtype)
        r = _rope_lanes(y[:, c0 + MLA_NOPE:c0 + MLA_QK_PAD], cf_ref[...], s1_ref[...], s2_ref[...])
        q_ref[hh, :, MLA_NOPE:] = (r * scale).astype(q_ref.dtype)


def _q_up(proj, gain, w_q, tables, *, tm=1024, hg=4):
    s = proj.shape[0]
    cf, s1, s2 = tables
    scale = (MLA_NOPE + MLA_ROPE) ** -0.5 * LOG2E
    tab_spec = pl.BlockSpec((tm, V7X_LANES), lambda i, h: (i, 0))
    return pl.pallas_call(
        functools.partial(_qup_body, scale=scale),
        grid=(s // tm, MLA_HEADS // hg),
        in_specs=[
            pl.BlockSpec((tm, MLA_Q_LORA), lambda i, h: (i, COL_CQ // MLA_Q_LORA)),
            pl.BlockSpec((1, MLA_Q_LORA), lambda i, h: (0, 0)),
            pl.BlockSpec((MLA_Q_LORA, hg * MLA_QK_PAD), lambda i, h: (0, h)),
            tab_spec, tab_spec, tab_spec,
        ],
        out_specs=pl.BlockSpec((hg, tm, MLA_QK_PAD), lambda i, h: (h, i, 0)),
        out_shape=jax.ShapeDtypeStruct((MLA_HEADS, s, MLA_QK_PAD), BF16),
        scratch_shapes=[pltpu.VMEM((tm, MLA_Q_LORA), BF16)],
        compiler_params=_params(("arbitrary", "arbitrary")),
        name="q_up",
    )(proj, gain.reshape(1, MLA_Q_LORA), w_q, cf, s1, s2)


def _kvup_body(c_ref, kr_ref, g_ref, w_ref, cf_ref, s1_ref, s2_ref, k_ref, v_ref, h_ref, r_ref):
    @pl.when(pl.program_id(1) == 0)
    def _():
        h_ref[...] = _rms(c_ref[...].astype(F32), g_ref[...]).astype(BF16)
        r_ref[...] = _rope_lanes(kr_ref[...].astype(F32), cf_ref[...], s1_ref[...], s2_ref[...]).astype(BF16)

    y = jnp.dot(h_ref[...], w_ref[...], preferred_element_type=F32)
    for hh in range(k_ref.shape[0]):
        c0 = hh * (MLA_NOPE + MLA_V)
        k_ref[hh, :, :MLA_NOPE] = y[:, c0:c0 + MLA_NOPE].astype(k_ref.dtype)
        k_ref[hh, :, MLA_NOPE:] = r_ref[...]
        v_ref[hh] = y[:, c0 + MLA_NOPE:c0 + MLA_NOPE + MLA_V].astype(v_ref.dtype)


def _kv_up(proj, gain, w_ukv, tables, *, tm=1024, hg=4):
    s = proj.shape[0]
    cf, s1, s2 = tables
    tab_spec = pl.BlockSpec((tm, V7X_LANES), lambda i, h: (i, 0))
    return pl.pallas_call(
        _kvup_body,
        grid=(s // tm, MLA_HEADS // hg),
        in_specs=[
            pl.BlockSpec((tm, MLA_KV_LORA), lambda i, h: (i, COL_CKV // MLA_KV_LORA)),
            pl.BlockSpec((tm, V7X_LANES), lambda i, h: (i, COL_KROPE // V7X_LANES)),
            pl.BlockSpec((1, MLA_KV_LORA), lambda i, h: (0, 0)),
            pl.BlockSpec((MLA_KV_LORA, hg * (MLA_NOPE + MLA_V)), lambda i, h: (0, h)),
            tab_spec, tab_spec, tab_spec,
        ],
        out_specs=[
            pl.BlockSpec((hg, tm, MLA_QK_PAD), lambda i, h: (h, i, 0)),
            pl.BlockSpec((hg, tm, MLA_V), lambda i, h: (h, i, 0)),
        ],
        out_shape=[
            jax.ShapeDtypeStruct((MLA_HEADS, s, MLA_QK_PAD), BF16),
            jax.ShapeDtypeStruct((MLA_HEADS, s, MLA_V), BF16),
        ],
        scratch_shapes=[pltpu.VMEM((tm, MLA_KV_LORA), BF16), pltpu.VMEM((tm, V7X_LANES), BF16)],
        compiler_params=_params(("arbitrary", "arbitrary")),
        name="kv_up",
    )(proj, proj, gain.reshape(1, MLA_KV_LORA), w_ukv, cf, s1, s2)


def _pv(values, p, keys_axis):
    if keys_axis == 0:
        return jnp.dot(values, p, preferred_element_type=F32)
    return jnp.dot(p, values, preferred_element_type=F32)


def _softmax_step(s, values, m, l, acc, keys_axis):
    m_new = jnp.maximum(m, jnp.max(s, axis=keys_axis, keepdims=True))
    alpha = jnp.exp2(m - m_new)
    p = jnp.exp2(s - m_new)
    l = alpha * l + jnp.sum(p, axis=keys_axis, keepdims=True)
    acc = alpha * acc + _pv(values, p.astype(values.dtype), keys_axis)
    return m_new, l, acc


def _softmax_chunks(scores_fn, values_fn, last_chunk, state, s_ref, p_ref, keys_axis):
    m_ref, l_ref, acc_ref, a_ref, cmax_ref = state

    def flush(g, slot):
        pv = _pv(values_fn(jnp.maximum(g, 0)), p_ref[slot], keys_axis)
        acc_ref[slot] = a_ref[slot] * acc_ref[slot] + pv

    def produce(g, slot):
        s = scores_fn(jnp.minimum(g, last_chunk))
        s_ref[slot] = s
        cmax_ref[slot] = jnp.max(s, axis=keys_axis, keepdims=True)

    def step(g, cur, produce_next=True):
        m = m_ref[cur]
        m_new = jnp.maximum(m, cmax_ref[cur])
        alpha = jnp.exp2(m - m_new)
        p = jnp.exp2(s_ref[cur] - m_new)
        l_ref[cur] = alpha * l_ref[cur] + jnp.sum(p, axis=keys_axis, keepdims=True)
        p_ref[cur] = p.astype(p_ref.dtype)
        m_ref[cur] = m_new
        a_ref[cur] = alpha
        if produce_next:
            produce(g + 1, 1 - cur)
        flush(g - 1, 1 - cur)

    def begin():
        produce(0, 0)
        p_ref[1] = jnp.zeros(p_ref.shape[1:], p_ref.dtype)
        a_ref[0] = jnp.ones(a_ref.shape[1:], a_ref.dtype)
        a_ref[1] = jnp.ones(a_ref.shape[1:], a_ref.dtype)
        m_ref[1] = jnp.full(m_ref.shape[1:], -jnp.inf, F32)
        l_ref[1] = jnp.zeros(l_ref.shape[1:], F32)
        acc_ref[1] = jnp.zeros(acc_ref.shape[1:], F32)

    def run(n):
        def pair(k, c):
            step(2 * k, 0)
            step(2 * k + 1, 1)
            return c

        lax.fori_loop(0, n // 2, pair, 0)

        @pl.when(n % 2 == 1)
        def _():
            step(n - 1, 0, produce_next=False)
            flush(n - 1, 0)

        @pl.when(n % 2 == 0)
        def _():
            flush(n - 1, 1)

        m = jnp.maximum(m_ref[0], m_ref[1])
        w0, w1 = jnp.exp2(m_ref[0] - m), jnp.exp2(m_ref[1] - m)
        return m, w0 * l_ref[0] + w1 * l_ref[1], w0 * acc_ref[0] + w1 * acc_ref[1]

    return begin, run


class _Slots:
    def __init__(self, ref0, ref1):
        self.refs = (ref0, ref1)
        self.shape = (2,) + tuple(ref0.shape)
        self.dtype = ref0.dtype

    def __getitem__(self, slot):
        return self.refs[slot][...]

    def __setitem__(self, slot, value):
        self.refs[slot][...] = value


def _softmax_scratch(dv, tq, span, keys_axis):
    stat, acc, tile = ((1, tq), (dv, tq), (span, tq)) if keys_axis == 0 else ((tq, 1), (tq, dv), (tq, span))
    kinds = ((stat, F32), (stat, F32), (acc, F32), (stat, F32), (stat, F32), (tile, F32), (tile, BF16))
    return [pltpu.VMEM(shape, dtype) for shape, dtype in kinds for _ in range(2)]


def _softmax_slots(refs):
    assert len(refs) == 14
    return [_Slots(refs[2 * k], refs[2 * k + 1]) for k in range(7)]


def _mla_body(q_ref, k_ref, v_ref, o_ref, *, t):
    i = pl.program_id(1)
    q = q_ref[0]

    def scores(n):
        return lax.dot_general(q, k_ref[0, pl.ds(pl.multiple_of(n * t, t), t), :], _NT, preferred_element_type=F32)

    def values(n):
        return v_ref[0, pl.ds(pl.multiple_of(n * t, t), t), :]

    def body(n, c):
        return _softmax_step(scores(n), values(n), *c, keys_axis=1)

    def body2(k, c):
        return body(2 * k + 1, body(2 * k, c))

    init = (jnp.full((t, 1), -jnp.inf, F32), jnp.zeros((t, 1), F32), jnp.zeros((t, MLA_V), F32))
    state = lax.fori_loop(0, i // 2, body2, init)
    m, l, acc = lax.cond(i % 2 == 1, lambda c: body(i - 1, c), lambda c: c, state)
    row = lax.broadcasted_iota(jnp.int32, (t, t), 0)
    col = lax.broadcasted_iota(jnp.int32, (t, t), 1)
    m, l, acc = _softmax_step(jnp.where(col <= row, scores(i), -jnp.inf), values(i), m, l, acc, keys_axis=1)
    o_ref[...] = (acc / l).astype(o_ref.dtype)


def _mla(q, k, v, *, t=1024):
    h, s, _ = q.shape
    assert s % t == 0
    return pl.pallas_call(
        functools.partial(_mla_body, t=t),
        grid=(h, s // t),
        in_specs=[
            pl.BlockSpec((1, t, MLA_QK_PAD), lambda h, i: (h, i, 0)),
            pl.BlockSpec((1, s, MLA_QK_PAD), lambda h, i: (h, 0, 0)),
            pl.BlockSpec((1, s, MLA_V), lambda h, i: (h, 0, 0)),
        ],
        out_specs=pl.BlockSpec((t, MLA_V), lambda h, i: (i, h)),
        out_shape=jax.ShapeDtypeStruct((s, h * MLA_V), BF16),
        compiler_params=_params(("arbitrary", "arbitrary")),
        name="mla",
    )(q, k, v)


def _t5_bucket(n):
    max_exact = REL_BUCKETS // 2
    n_f = jnp.maximum(n, max_exact).astype(F32)
    large = max_exact + (jnp.log(n_f / max_exact) / math.log(REL_MAX_DIST / max_exact)
                         * (REL_BUCKETS - max_exact)).astype(jnp.int32)
    large = jnp.minimum(large, REL_BUCKETS - 1)
    return jnp.where(n < max_exact, n, large)


def _moba_body(pmin_ref, pmax_ref, first_ref, q_ref, k_ref, v_ref, pq_ref, pk_ref, pmaxl_ref, tbl_ref, o_ref,
               kmean_ref, kaug_ref, vt_ref, *softmax_refs, nblk, scale, group):
    i = pl.program_id(1)
    blk = MOBA_BLOCK
    lane_id = lax.broadcasted_iota(jnp.int32, (blk, V7X_LANES), 1)
    m_ref, l_ref, acc_ref, a_ref, cmax_ref, s_ref, p_ref = _softmax_slots(softmax_refs)

    @pl.when(i == 0)
    def _():
        kmean_ref[...] = jnp.zeros_like(kmean_ref)

        def fill(n, c):
            rows = pl.ds(pl.multiple_of(n * blk, blk), blk)
            kb = k_ref[rows, :]
            kmean_ref[pl.ds(n, 1), :] = jnp.sum(kb.astype(F32), axis=0, keepdims=True) * (1.0 / blk)
            kaug_ref[rows, :MOBA_DH] = kb
            kaug_ref[rows, MOBA_DH:] = jnp.where(lane_id == n, 1.0, 0.0).astype(kaug_ref.dtype)
            vt_ref[:, rows] = v_ref[rows, :].astype(F32).T.astype(vt_ref.dtype)
            return c

        lax.fori_loop(0, nblk, fill, 0)

    q = q_ref[...]
    qs = (q.astype(F32) * scale).astype(q.dtype)

    gate = lax.dot_general(kmean_ref[...], q.astype(F32), _NT, preferred_element_type=F32)
    bid = lax.broadcasted_iota(jnp.int32, gate.shape, 0)
    bid_f = bid.astype(F32)
    g = jnp.where(bid < i, gate, -jnp.inf)
    picked_t = jnp.zeros(gate.shape, F32)
    for _ in range(min(MOBA_TOPK, nblk)):
        mx = jnp.max(g, axis=0, keepdims=True)
        idx = jnp.min(jnp.where(g == mx, bid_f, float(V7X_LANES)), axis=0, keepdims=True)
        hit = bid_f == jnp.where(mx > -jnp.inf, idx, -1.0)
        picked_t = jnp.where(hit, 1.0, picked_t)
        g = jnp.where(hit, -jnp.inf, g)
    picked = picked_t.T

    prev = jnp.maximum(i - 1, 0)
    far_lanes = jnp.logical_and((pmin_ref[i] - pmaxl_ref[...]) >= BIAS_TABLE - 1, lane_id[:1] != prev)
    m_any = jnp.where(picked > 0.0, 0.0, MASKED)
    q_any = jnp.concatenate([qs, m_any.astype(qs.dtype)], axis=1)
    q_far = jnp.concatenate([qs, jnp.where(far_lanes, m_any, MASKED).astype(qs.dtype)], axis=1)
    q_diag = jnp.concatenate([qs, jnp.where(lane_id == i, 0.0, m_any).astype(qs.dtype)], axis=1)

    def is_far(n):
        return (pmin_ref[i] - pmax_ref[n]) >= BIAS_TABLE - 1

    def masked_scores(off, width, q_aug):
        return lax.dot_general(kaug_ref[pl.ds(off, width), :], q_aug, _NT, preferred_element_type=F32)

    def near_bias(off):
        off = pl.multiple_of(off, blk)
        d = jnp.clip(pq_ref[...] - pk_ref[:, pl.ds(off, blk)], 0, BIAS_TABLE - 1)
        tb = jnp.broadcast_to(tbl_ref[0], (blk, BIAS_TABLE))
        parts = [jnp.take_along_axis(tb, d[:, c * V7X_LANES:(c + 1) * V7X_LANES], axis=1)
                 for c in range(blk // V7X_LANES)]
        return jnp.concatenate(parts, axis=1).T

    span = group * blk

    def far_scores(g):
        return masked_scores(pl.multiple_of(g * span, span), span, q_far)

    def far_values_t(g):
        return vt_ref[:, pl.ds(pl.multiple_of(g * span, span), span)]

    far_begin, far_run = _softmax_chunks(far_scores, far_values_t, nblk // group - 1,
                                         (m_ref, l_ref, acc_ref, a_ref, cmax_ref), s_ref, p_ref, keys_axis=0)
    off_d = pl.multiple_of(prev * blk, blk)
    key = prev * blk + lax.broadcasted_iota(jnp.int32, (2 * blk, blk), 0)
    qry = i * blk + lax.broadcasted_iota(jnp.int32, (2 * blk, blk), 1)
    s = masked_scores(off_d, 2 * blk, q_diag)
    far_begin()
    s = s + jnp.concatenate([near_bias(off_d), near_bias(off_d + blk)], axis=0)
    s = jnp.where(key <= qry, s, -jnp.inf)
    m = jnp.max(s, axis=0, keepdims=True)
    p = jnp.exp2(s - m)
    m_ref[0] = m
    l_ref[0] = jnp.sum(p, axis=0, keepdims=True)
    acc_ref[0] = jnp.dot(vt_ref[:, pl.ds(off_d, 2 * blk)], p.astype(vt_ref.dtype), preferred_element_type=F32)
    m_ref[0], l_ref[0], acc_ref[0] = far_run((i + group - 1) // group)

    def near_block(n, c):
        @pl.when(jnp.logical_not(is_far(n)))
        def _():
            off = pl.multiple_of(n * blk, blk)
            s = masked_scores(off, blk, q_any) + near_bias(off)
            m_ref[0], l_ref[0], acc_ref[0] = _softmax_step(
                s, vt_ref[:, pl.ds(off, blk)], m_ref[0], l_ref[0], acc_ref[0], keys_axis=0)

        return c

    lax.fori_loop(first_ref[i], prev, near_block, 0)
    o_ref[...] = (acc_ref[0] / l_ref[0]).T.astype(o_ref.dtype)


def _moba(proj, positions, rel_bias, *, group=4):
    s = proj.shape[0]
    blk = MOBA_BLOCK
    nblk = s // blk
    assert s % blk == 0 and 2 <= nblk <= V7X_LANES and nblk % group == 0
    pos_blocks = positions.reshape(nblk, blk)
    pmin = jnp.min(pos_blocks, axis=1)
    pmax = jnp.max(pos_blocks, axis=1)
    table = rel_bias[_t5_bucket(jnp.arange(BIAS_TABLE, dtype=jnp.int32))]
    table = jnp.transpose(table).reshape(MOBA_HEADS, 1, BIAS_TABLE).astype(F32)
    table = (table - table[:, :, BIAS_TABLE - 1:]) * LOG2E
    pmax_lanes = jnp.pad(pmax, (0, V7X_LANES - nblk)).reshape(1, V7X_LANES)
    blk_id = jnp.arange(nblk, dtype=jnp.int32)
    pair_start = jnp.maximum(blk_id - 1, 0)
    near = jnp.logical_and(pmin[:, None] - pmax[None, :] < BIAS_TABLE - 1, blk_id[None, :] < pair_start[:, None])
    first_near = jnp.min(jnp.where(near, blk_id[None, :], pair_start[:, None]), axis=1)
    qc, kc, vc = (c // MOBA_DH for c in (COL_MOBA_Q, COL_MOBA_K, COL_MOBA_V))
    grid_spec = pltpu.PrefetchScalarGridSpec(
        num_scalar_prefetch=3,
        grid=(MOBA_HEADS, nblk),
        in_specs=[
            pl.BlockSpec((blk, MOBA_DH), lambda h, i, *_: (i, qc + h)),
            pl.BlockSpec((s, MOBA_DH), lambda h, i, *_: (0, kc + h)),
            pl.BlockSpec((s, MOBA_DH), lambda h, i, *_: (0, vc + h)),
            pl.BlockSpec((blk, 1), lambda h, i, *_: (i, 0)),
            pl.BlockSpec((1, s), lambda h, i, *_: (0, 0)),
            pl.BlockSpec((1, V7X_LANES), lambda h, i, *_: (0, 0)),
            pl.BlockSpec((1, 1, BIAS_TABLE), lambda h, i, *_: (h, 0, 0)),
        ],
        out_specs=pl.BlockSpec((blk, MOBA_DH), lambda h, i, *_: (i, h)),
        scratch_shapes=[
            pltpu.VMEM((V7X_LANES, MOBA_DH), F32),
            pltpu.VMEM((s, 2 * MOBA_DH), BF16),
            pltpu.VMEM((MOBA_DH, s), BF16),
        ] + _softmax_scratch(MOBA_DH, blk, group * blk, keys_axis=0),
    )
    return pl.pallas_call(
        functools.partial(_moba_body, nblk=nblk, scale=MOBA_DH ** -0.5 * LOG2E, group=group),
        grid_spec=grid_spec,
        out_shape=jax.ShapeDtypeStruct((s, MOBA_HEADS * MOBA_DH), BF16),
        compiler_params=_params(("arbitrary", "arbitrary")),
        name="moba",
    )(pmin, pmax, first_near, proj, proj, proj, positions.reshape(s, 1), positions.reshape(1, s), pmax_lanes, table)


def _merge_body(oa_ref, ob_ref, ga_ref, gb_ref, wa_ref, wb_ref, wo_ref, x_ref, o_ref):
    @pl.when(pl.program_id(1) == 0)
    def _():
        o_ref[...] = x_ref[...]

    ma = jnp.dot(oa_ref[...], wa_ref[...], preferred_element_type=F32)
    mb = jnp.dot(ob_ref[...], wb_ref[...], preferred_element_type=F32)
    mg = _sigmoid(ga_ref[...].astype(F32)) * ma + _sigmoid(gb_ref[...].astype(F32)) * mb
    o_ref[...] += jnp.dot(mg.astype(BF16), wo_ref[...], preferred_element_type=F32)


def _merge(o_a, o_b, proj, w_a, w_b, w_out, x, *, tm=512, tn=512):
    s, d = x.shape
    ka, kb = o_a.shape[1], o_b.shape[1]
    ga, gb = COL_GATE_A // tn, COL_GATE_B // tn
    return pl.pallas_call(
        _merge_body,
        grid=(s // tm, d // tn),
        in_specs=[
            pl.BlockSpec((tm, ka), lambda i, j: (i, 0)),
            pl.BlockSpec((tm, kb), lambda i, j: (i, 0)),
            pl.BlockSpec((tm, tn), lambda i, j: (i, ga + j)),
            pl.BlockSpec((tm, tn), lambda i, j: (i, gb + j)),
            pl.BlockSpec((ka, tn), lambda i, j: (0, j)),
            pl.BlockSpec((kb, tn), lambda i, j: (0, j)),
            pl.BlockSpec((tn, d), lambda i, j: (j, 0)),
            _row_resident_spec(tm, d),
        ],
        out_specs=pl.BlockSpec((tm, d), lambda i, j: (i, 0)),
        out_shape=jax.ShapeDtypeStruct((s, d), F32),
        compiler_params=_params(("arbitrary", "arbitrary")),
        name="merge",
    )(o_a, o_b, proj, proj, w_a, w_b, w_out, x)


def _proj_layout(w_in):
    w_lat = jnp.pad(w_in[:, :LATENT_REAL].astype(BF16), ((0, 0), (0, LATENT_COLS - LATENT_REAL)))
    return w_lat, w_in[:, LATENT_REAL:].astype(BF16)


def _q_layout(w_uq):
    r = w_uq.shape[0]
    w = w_uq.reshape(r, MLA_HEADS, MLA_NOPE + MLA_ROPE)
    w = jnp.pad(w, ((0, 0), (0, 0), (0, MLA_QK_PAD - MLA_NOPE - MLA_ROPE)))
    return w.reshape(r, MLA_HEADS * MLA_QK_PAD).astype(BF16)


def kernel(x, positions, rel_bias, norm_ffn1, w_ffn1_in, w_ffn1_out, norm_mix, w_in, norm_cq, w_uq, norm_ckv, w_ukv, w_br_a, w_br_b, w_out, norm_ffn2, w_ffn2_in, w_ffn2_out, norm_final):
    b, s, d = x.shape
    assert b == 1 and norm_ffn1.shape[0] == 1
    pos = positions[0]
    x0 = x[0]

    x1 = _ffn(x0, norm_ffn1[0], w_ffn1_in[0].astype(BF16), w_ffn1_out[0].astype(BF16))
    w_lat, w_mix = _proj_layout(w_in[0])
    latent, mixer = _proj(x1, norm_mix[0], w_lat, w_mix)
    tables = _rope_tables(pos)
    q = _q_up(latent, norm_cq[0], _q_layout(w_uq[0]), tables)
    k, v = _kv_up(latent, norm_ckv[0], w_ukv[0].astype(BF16), tables)
    o_a = _mla(q, k, v)
    o_b = _moba(mixer, pos, rel_bias)
    x2 = _merge(o_a, o_b, mixer, w_br_a[0].astype(BF16), w_br_b[0].astype(BF16), w_out[0].astype(BF16), x1)
    out = _ffn(x2, norm_ffn2[0], w_ffn2_in[0].astype(BF16), w_ffn2_out[0].astype(BF16), norm_final)
    return out[None]
```

```python
import functools
import math

import jax
import jax.numpy as jnp
from jax import lax
from jax.experimental import pallas as pl
from jax.experimental.pallas import tpu as pltpu

F32 = jnp.float32
BF16 = jnp.bfloat16

RMS_EPS = 1e-6
ROPE_THETA = 10000.0
LOG2E = math.log2(math.e)
MASKED = -1e30

MLA_HEADS = 16
MLA_Q_LORA = 1024
MLA_KV_LORA = 512
MLA_NOPE = 128
MLA_ROPE = 64
MLA_V = 128
MLA_QK_PAD = 256

MOBA_HEADS = 16
MOBA_DH = 128
MOBA_BLOCK = 256
MOBA_TOPK = 3
REL_BUCKETS = 32
REL_MAX_DIST = 128
BIAS_TABLE = 128

V7X_LANES = 128
V7X_VMEM_LIMIT = 56 * 1024 * 1024

LATENT_REAL = MLA_Q_LORA + MLA_KV_LORA + MLA_ROPE
LATENT_COLS = 2048
COL_CQ = 0
COL_CKV = 1024
COL_KROPE = 1536
COL_MOBA_Q = 0
COL_MOBA_K = 2048
COL_MOBA_V = 4096
COL_GATE_A = 6144
COL_GATE_B = 10240

_NT = (((1,), (1,)), ((), ()))


def _params(sem):
    return pltpu.CompilerParams(dimension_semantics=sem, vmem_limit_bytes=V7X_VMEM_LIMIT)


def _row_resident_spec(tm, d):
    return pl.BlockSpec((tm, d), lambda i, j: (i, 0), pipeline_mode=pl.Buffered(1))


def _rms(xf, gain):
    ms = jnp.mean(xf * xf, axis=-1, keepdims=True)
    return xf * lax.rsqrt(ms + RMS_EPS) * gain


def _sigmoid(z):
    return 1.0 / (1.0 + jnp.exp(-z))


def _ffn_body(x_ref, g_ref, wg_ref, wu_ref, wo_ref, *rest, final_norm):
    if final_norm:
        gf_ref, o_ref, h_ref = rest
    else:
        o_ref, h_ref = rest
    j = pl.program_id(1)

    @pl.when(j == 0)
    def _():
        xf = x_ref[...]
        h_ref[...] = _rms(xf, g_ref[...]).astype(BF16)
        o_ref[...] = xf

    h = h_ref[...]
    g = jnp.dot(h, wg_ref[...], preferred_element_type=F32)
    u = jnp.dot(h, wu_ref[...], preferred_element_type=F32)
    a = (g * _sigmoid(g) * (0.5 * u)).astype(BF16)
    o_ref[...] += jnp.dot(a, wo_ref[...].astype(BF16), preferred_element_type=F32)

    if final_norm:
        @pl.when(j == pl.num_programs(1) - 1)
        def _():
            o_ref[...] = _rms(o_ref[...], gf_ref[...])


def _ffn(x, gain, w_in, w_out, final_gain=None, *, tm=512, tf=256):
    s, d = x.shape
    f = w_out.shape[0]
    nf = f // tf
    assert s % tm == 0 and f % tf == 0 and w_in.shape == (d, 2 * f)
    in_specs = [
        _row_resident_spec(tm, d),
        pl.BlockSpec((1, d), lambda i, j: (0, 0)),
        pl.BlockSpec((d, tf), lambda i, j: (0, j)),
        pl.BlockSpec((d, tf), lambda i, j: (0, j + nf)),
        pl.BlockSpec((tf, d), lambda i, j: (j, 0)),
    ]
    args = [x, gain.reshape(1, d), w_in, w_in, w_out]
    if final_gain is not None:
        in_specs.append(pl.BlockSpec((1, d), lambda i, j: (0, 0)))
        args.append(final_gain.reshape(1, d))
    return pl.pallas_call(
        functools.partial(_ffn_body, final_norm=final_gain is not None),
        grid=(s // tm, nf),
        in_specs=in_specs,
        out_specs=pl.BlockSpec((tm, d), lambda i, j: (i, 0)),
        out_shape=jax.ShapeDtypeStruct((s, d), F32),
        scratch_shapes=[pltpu.VMEM((tm, d), BF16)],
        compiler_params=_params(("arbitrary", "arbitrary")),
        name="ffn",
    )(*args)


def _proj_body(x_ref, g_ref, wa_ref, wb_ref, oa_ref, ob_ref, h_ref, *, na):
    j = pl.program_id(1)

    @pl.when(j == 0)
    def _():
        h_ref[...] = _rms(x_ref[...], g_ref[...]).astype(BF16)

    @pl.when(j < na)
    def _():
        oa_ref[...] = jnp.dot(h_ref[...], wa_ref[...], preferred_element_type=F32).astype(oa_ref.dtype)

    @pl.when(j >= na)
    def _():
        ob_ref[...] = jnp.dot(h_ref[...], wb_ref[...], preferred_element_type=F32).astype(ob_ref.dtype)


def _proj(x, gain, w_a, w_b, *, tm=512, tn=1024):
    s, d = x.shape
    na, nb = w_a.shape[1] // tn, w_b.shape[1] // tn
    assert s % tm == 0 and w_a.shape[1] % tn == 0 and w_b.shape[1] % tn == 0
    col_a = lambda j: jnp.minimum(j, na - 1)
    col_b = lambda j: jnp.maximum(j - na, 0)
    return pl.pallas_call(
        functools.partial(_proj_body, na=na),
        grid=(s // tm, na + nb),
        in_specs=[
            _row_resident_spec(tm, d),
            pl.BlockSpec((1, d), lambda i, j: (0, 0)),
            pl.BlockSpec((d, tn), lambda i, j: (0, col_a(j))),
            pl.BlockSpec((d, tn), lambda i, j: (0, col_b(j))),
        ],
        out_specs=[
            pl.BlockSpec((tm, tn), lambda i, j: (i, col_a(j))),
            pl.BlockSpec((tm, tn), lambda i, j: (i, col_b(j))),
        ],
        out_shape=[jax.ShapeDtypeStruct((s, w_a.shape[1]), BF16), jax.ShapeDtypeStruct((s, w_b.shape[1]), BF16)],
        scratch_shapes=[pltpu.VMEM((tm, d), BF16)],
        compiler_params=_params(("arbitrary", "arbitrary")),
        name="proj",
    )(x, gain.reshape(1, d), w_a, w_b)


def _rope_lanes(r, cf, s1, s2):
    return r * cf + pltpu.roll(r, 96, 1) * s1 + pltpu.roll(r, 32, 1) * s2


def _rope_tables(positions):
    half = MLA_ROPE // 2
    inv_freq = ROPE_THETA ** (-jnp.arange(0, MLA_ROPE, 2, dtype=F32) / MLA_ROPE)
    ang = positions[:, None].astype(F32) * inv_freq
    cos, sin = jnp.cos(ang), jnp.sin(ang)
    z = jnp.zeros_like(cos)
    pad = jnp.zeros((positions.shape[0], V7X_LANES - MLA_ROPE), F32)
    cf = jnp.concatenate([cos, cos, pad], axis=1)
    s1 = jnp.concatenate([-sin, z, pad], axis=1)
    s2 = jnp.concatenate([z, sin, pad], axis=1)
    return cf, s1, s2


def _qup_body(c_ref, g_ref, w_ref, cf_ref, s1_ref, s2_ref, q_ref, h_ref, *, scale):
    @pl.when(pl.program_id(1) == 0)
    def _():
        h_ref[...] = _rms(c_ref[...].astype(F32), g_ref[...]).astype(BF16)

    y = jnp.dot(h_ref[...], w_ref[...], preferred_element_type=F32)
    for hh in range(q_ref.shape[0]):
        c0 = hh * MLA_QK_PAD
        q_ref[hh, :, :MLA_NOPE] = (y[:, c0:c0 + MLA_NOPE] * scale).astype(q_ref.dtype)
        r = _rope_lanes(y[:, c0 + MLA_NOPE:c0 + MLA_QK_PAD], cf_ref[...], s1_ref[...], s2_ref[...])
        q_ref[hh, :, MLA_NOPE:] = (r * scale).astype(q_ref.dtype)


def _q_up(proj, gain, w_q, tables, *, tm=1024, hg=4):
    s = proj.shape[0]
    cf, s1, s2 = tables
    scale = (MLA_NOPE + MLA_ROPE) ** -0.5 * LOG2E
    tab_spec = pl.BlockSpec((tm, V7X_LANES), lambda i, h: (i, 0))
    return pl.pallas_call(
        functools.partial(_qup_body, scale=scale),
        grid=(s // tm, MLA_HEADS // hg),
        in_specs=[
            pl.BlockSpec((tm, MLA_Q_LORA), lambda i, h: (i, COL_CQ // MLA_Q_LORA)),
            pl.BlockSpec((1, MLA_Q_LORA), lambda i, h: (0, 0)),
            pl.BlockSpec((MLA_Q_LORA, hg * MLA_QK_PAD), lambda i, h: (0, h)),
            tab_spec, tab_spec, tab_spec,
        ],
        out_specs=pl.BlockSpec((hg, tm, MLA_QK_PAD), lambda i, h: (h, i, 0)),
        out_shape=jax.ShapeDtypeStruct((MLA_HEADS, s, MLA_QK_PAD), BF16),
        scratch_shapes=[pltpu.VMEM((tm, MLA_Q_LORA), BF16)],
        compiler_params=_params(("arbitrary", "arbitrary")),
        name="q_up",
    )(proj, gain.reshape(1, MLA_Q_LORA), w_q, cf, s1, s2)


def _kvup_body(c_ref, kr_ref, g_ref, w_ref, cf_ref, s1_ref, s2_ref, k_ref, v_ref, h_ref, r_ref):
    @pl.when(pl.program_id(1) == 0)
    def _():
        h_ref[...] = _rms(c_ref[...].astype(F32), g_ref[...]).astype(BF16)
        r_ref[...] = _rope_lanes(kr_ref[...].astype(F32), cf_ref[...], s1_ref[...], s2_ref[...]).astype(BF16)

    y = jnp.dot(h_ref[...], w_ref[...], preferred_element_type=F32)
    for hh in range(k_ref.shape[0]):
        c0 = hh * (MLA_NOPE + MLA_V)
        k_ref[hh, :, :MLA_NOPE] = y[:, c0:c0 + MLA_NOPE].astype(k_ref.dtype)
        k_ref[hh, :, MLA_NOPE:] = r_ref[...]
        v_ref[hh] = y[:, c0 + MLA_NOPE:c0 + MLA_NOPE + MLA_V].astype(v_ref.dtype)


def _kv_up(proj, gain, w_ukv, tables, *, tm=1024, hg=4):
    s = proj.shape[0]
    cf, s1, s2 = tables
    tab_spec = pl.BlockSpec((tm, V7X_LANES), lambda i, h: (i, 0))
    return pl.pallas_call(
        _kvup_body,
        grid=(s // tm, MLA_HEADS // hg),
        in_specs=[
            pl.BlockSpec((tm, MLA_KV_LORA), lambda i, h: (i, COL_CKV // MLA_KV_LORA)),
            pl.BlockSpec((tm, V7X_LANES), lambda i, h: (i, COL_KROPE // V7X_LANES)),
            pl.BlockSpec((1, MLA_KV_LORA), lambda i, h: (0, 0)),
            pl.BlockSpec((MLA_KV_LORA, hg * (MLA_NOPE + MLA_V)), lambda i, h: (0, h)),
            tab_spec, tab_spec, tab_spec,
        ],
        out_specs=[
            pl.BlockSpec((hg, tm, MLA_QK_PAD), lambda i, h: (h, i, 0)),
            pl.BlockSpec((hg, tm, MLA_V), lambda i, h: (h, i, 0)),
        ],
        out_shape=[
            jax.ShapeDtypeStruct((MLA_HEADS, s, MLA_QK_PAD), BF16),
            jax.ShapeDtypeStruct((MLA_HEADS, s, MLA_V), BF16),
        ],
        scratch_shapes=[pltpu.VMEM((tm, MLA_KV_LORA), BF16), pltpu.VMEM((tm, V7X_LANES), BF16)],
        compiler_params=_params(("arbitrary", "arbitrary")),
        name="kv_up",
    )(proj, proj, gain.reshape(1, MLA_KV_LORA), w_ukv, cf, s1, s2)


def _pv(values, p, keys_axis):
    if keys_axis == 0:
        return jnp.dot(values, p, preferred_element_type=F32)
    return jnp.dot(p, values, preferred_element_type=F32)


def _softmax_step(s, values, m, l, acc, keys_axis):
    m_new = jnp.maximum(m, jnp.max(s, axis=keys_axis, keepdims=True))
    alpha = jnp.exp2(m - m_new)
    p = jnp.exp2(s - m_new)
    l = alpha * l + jnp.sum(p, axis=keys_axis, keepdims=True)
    acc = alpha * acc + _pv(values, p.astype(values.dtype), keys_axis)
    return m_new, l, acc


def _softmax_chunks(scores_fn, values_fn, last_chunk, state, s_ref, p_ref, keys_axis):
    m_ref, l_ref, acc_ref, a_ref, cmax_ref = state

    def flush(g, slot):
        pv = _pv(values_fn(jnp.maximum(g, 0)), p_ref[slot], keys_axis)
        acc_ref[slot] = a_ref[slot] * acc_ref[slot] + pv

    def produce(g, slot):
        s = scores_fn(jnp.minimum(g, last_chunk))
        s_ref[slot] = s
        cmax_ref[slot] = jnp.max(s, axis=keys_axis, keepdims=True)

    def step(g, cur, produce_next=True):
        m = m_ref[cur]
        m_new = jnp.maximum(m, cmax_ref[cur])
        alpha = jnp.exp2(m - m_new)
        p = jnp.exp2(s_ref[cur] - m_new)
        l_ref[cur] = alpha * l_ref[cur] + jnp.sum(p, axis=keys_axis, keepdims=True)
        p_ref[cur] = p.astype(p_ref.dtype)
        m_ref[cur] = m_new
        a_ref[cur] = alpha
        if produce_next:
            produce(g + 1, 1 - cur)
        flush(g - 1, 1 - cur)

    def begin():
        produce(0, 0)
        p_ref[1] = jnp.zeros(p_ref.shape[1:], p_ref.dtype)
        a_ref[0] = jnp.ones(a_ref.shape[1:], a_ref.dtype)
        a_ref[1] = jnp.ones(a_ref.shape[1:], a_ref.dtype)
        m_ref[1] = jnp.full(m_ref.shape[1:], -jnp.inf, F32)
        l_ref[1] = jnp.zeros(l_ref.shape[1:], F32)
        acc_ref[1] = jnp.zeros(acc_ref.shape[1:], F32)

    def run(n):
        def pair(k, c):
            step(2 * k, 0)
            step(2 * k + 1, 1)
            return c

        lax.fori_loop(0, n // 2, pair, 0)

        @pl.when(n % 2 == 1)
        def _():
            step(n - 1, 0, produce_next=False)
            flush(n - 1, 0)

        @pl.when(n % 2 == 0)
        def _():
            flush(n - 1, 1)

        m = jnp.maximum(m_ref[0], m_ref[1])
        w0, w1 = jnp.exp2(m_ref[0] - m), jnp.exp2(m_ref[1] - m)
        return m, w0 * l_ref[0] + w1 * l_ref[1], w0 * acc_ref[0] + w1 * acc_ref[1]

    return begin, run


class _Slots:
    def __init__(self, ref0, ref1):
        self.refs = (ref0, ref1)
        self.shape = (2,) + tuple(ref0.shape)
        self.dtype = ref0.dtype

    def __getitem__(self, slot):
        return self.refs[slot][...]

    def __setitem__(self, slot, value):
        self.refs[slot][...] = value


def _softmax_scratch(dv, tq, span, keys_axis):
    stat, acc, tile = ((1, tq), (dv, tq), (span, tq)) if keys_axis == 0 else ((tq, 1), (tq, dv), (tq, span))
    kinds = ((stat, F32), (stat, F32), (acc, F32), (stat, F32), (stat, F32), (tile, F32), (tile, BF16))
    return [pltpu.VMEM(shape, dtype) for shape, dtype in kinds for _ in range(2)]


def _softmax_slots(refs):
    assert len(refs) == 14
    return [_Slots(refs[2 * k], refs[2 * k + 1]) for k in range(7)]


def _mla_body(q_ref, k_ref, v_ref, o_ref, *, t):
    i = pl.program_id(1)
    q = q_ref[0]

    def scores(n):
        return lax.dot_general(q, k_ref[0, pl.ds(pl.multiple_of(n * t, t), t), :], _NT, preferred_element_type=F32)

    def values(n):
        return v_ref[0, pl.ds(pl.multiple_of(n * t, t), t), :]

    def body(n, c):
        return _softmax_step(scores(n), values(n), *c, keys_axis=1)

    def body2(k, c):
        return body(2 * k + 1, body(2 * k, c))

    init = (jnp.full((t, 1), -jnp.inf, F32), jnp.zeros((t, 1), F32), jnp.zeros((t, MLA_V), F32))
    state = lax.fori_loop(0, i // 2, body2, init)
    m, l, acc = lax.cond(i % 2 == 1, lambda c: body(i - 1, c), lambda c: c, state)
    row = lax.broadcasted_iota(jnp.int32, (t, t), 0)
    col = lax.broadcasted_iota(jnp.int32, (t, t), 1)
    m, l, acc = _softmax_step(jnp.where(col <= row, scores(i), -jnp.inf), values(i), m, l, acc, keys_axis=1)
    o_ref[...] = (acc / l).astype(o_ref.dtype)


def _mla(q, k, v, *, t=1024):
    h, s, _ = q.shape
    assert s % t == 0
    return pl.pallas_call(
        functools.partial(_mla_body, t=t),
        grid=(h, s // t),
        in_specs=[
            pl.BlockSpec((1, t, MLA_QK_PAD), lambda h, i: (h, i, 0)),
            pl.BlockSpec((1, s, MLA_QK_PAD), lambda h, i: (h, 0, 0)),
            pl.BlockSpec((1, s, MLA_V), lambda h, i: (h, 0, 0)),
        ],
        out_specs=pl.BlockSpec((t, MLA_V), lambda h, i: (i, h)),
        out_shape=jax.ShapeDtypeStruct((s, h * MLA_V), BF16),
        compiler_params=_params(("arbitrary", "arbitrary")),
        name="mla",
    )(q, k, v)


def _t5_bucket(n):
    max_exact = REL_BUCKETS // 2
    n_f = jnp.maximum(n, max_exact).astype(F32)
    large = max_exact + (jnp.log(n_f / max_exact) / math.log(REL_MAX_DIST / max_exact)
                         * (REL_BUCKETS - max_exact)).astype(jnp.int32)
    large = jnp.minimum(large, REL_BUCKETS - 1)
    return jnp.where(n < max_exact, n, large)


def _moba_body(pmin_ref, pmax_ref, first_ref, q_ref, k_ref, v_ref, pq_ref, pk_ref, pmaxl_ref, tbl_ref, o_ref,
               kmean_ref, kaug_ref, vt_ref, *softmax_refs, nblk, scale, group):
    i = pl.program_id(1)
    blk = MOBA_BLOCK
    lane_id = lax.broadcasted_iota(jnp.int32, (blk, V7X_LANES), 1)
    m_ref, l_ref, acc_ref, a_ref, cmax_ref, s_ref, p_ref = _softmax_slots(softmax_refs)

    @pl.when(i == 0)
    def _():
        kmean_ref[...] = jnp.zeros_like(kmean_ref)

        def fill(n, c):
            rows = pl.ds(pl.multiple_of(n * blk, blk), blk)
            kb = k_ref[rows, :]
            kmean_ref[pl.ds(n, 1), :] = jnp.sum(kb.astype(F32), axis=0, keepdims=True) * (1.0 / blk)
            kaug_ref[rows, :MOBA_DH] = kb
            kaug_ref[rows, MOBA_DH:] = jnp.where(lane_id == n, 1.0, 0.0).astype(kaug_ref.dtype)
            vt_ref[:, rows] = v_ref[rows, :].astype(F32).T.astype(vt_ref.dtype)
            return c

        lax.fori_loop(0, nblk, fill, 0)

    q = q_ref[...]
    qs = (q.astype(F32) * scale).astype(q.dtype)

    gate = lax.dot_general(kmean_ref[...], q.astype(F32), _NT, preferred_element_type=F32)
    bid = lax.broadcasted_iota(jnp.int32, gate.shape, 0)
    bid_f = bid.astype(F32)
    g = jnp.where(bid < i, gate, -jnp.inf)
    picked_t = jnp.zeros(gate.shape, F32)
    for _ in range(min(MOBA_TOPK, nblk)):
        mx = jnp.max(g, axis=0, keepdims=True)
        idx = jnp.min(jnp.where(g == mx, bid_f, float(V7X_LANES)), axis=0, keepdims=True)
        hit = bid_f == jnp.where(mx > -jnp.inf, idx, -1.0)
        picked_t = jnp.where(hit, 1.0, picked_t)
        g = jnp.where(hit, -jnp.inf, g)
    picked = picked_t.T

    prev = jnp.maximum(i - 1, 0)
    far_lanes = jnp.logical_and((pmin_ref[i] - pmaxl_ref[...]) >= BIAS_TABLE - 1, lane_id[:1] != prev)
    m_any = jnp.where(picked > 0.0, 0.0, MASKED)
    q_any = jnp.concatenate([qs, m_any.astype(qs.dtype)], axis=1)
    q_far = jnp.concatenate([qs, jnp.where(far_lanes, m_any, MASKED).astype(qs.dtype)], axis=1)
    q_diag = jnp.concatenate([qs, jnp.where(lane_id == i, 0.0, m_any).astype(qs.dtype)], axis=1)

    def is_far(n):
        return (pmin_ref[i] - pmax_ref[n]) >= BIAS_TABLE - 1

    def masked_scores(off, width, q_aug):
        return lax.dot_general(kaug_ref[pl.ds(off, width), :], q_aug, _NT, preferred_element_type=F32)

    def near_bias(off):
        off = pl.multiple_of(off, blk)
        d = jnp.clip(pq_ref[...] - pk_ref[:, pl.ds(off, blk)], 0, BIAS_TABLE - 1)
        tb = jnp.broadcast_to(tbl_ref[0], (blk, BIAS_TABLE))
        parts = [jnp.take_along_axis(tb, d[:, c * V7X_LANES:(c + 1) * V7X_LANES], axis=1)
                 for c in range(blk // V7X_LANES)]
        return jnp.concatenate(parts, axis=1).T

    span = group * blk

    def far_scores(g):
        return masked_scores(pl.multiple_of(g * span, span), span, q_far)

    def far_values_t(g):
        return vt_ref[:, pl.ds(pl.multiple_of(g * span, span), span)]

    far_begin, far_run = _softmax_chunks(far_scores, far_values_t, nblk // group - 1,
                                         (m_ref, l_ref, acc_ref, a_ref, cmax_ref), s_ref, p_ref, keys_axis=0)
    off_d = pl.multiple_of(prev * blk, blk)
    key = prev * blk + lax.broadcasted_iota(jnp.int32, (2 * blk, blk), 0)
    qry = i * blk + lax.broadcasted_iota(jnp.int32, (2 * blk, blk), 1)
    s = masked_scores(off_d, 2 * blk, q_diag)
    far_begin()
    s = s + jnp.concatenate([near_bias(off_d), near_bias(off_d + blk)], axis=0)
    s = jnp.where(key <= qry, s, -jnp.inf)
    m = jnp.max(s, axis=0, keepdims=True)
    p = jnp.exp2(s - m)
    m_ref[0] = m
    l_ref[0] = jnp.sum(p, axis=0, keepdims=True)
    acc_ref[0] = jnp.dot(vt_ref[:, pl.ds(off_d, 2 * blk)], p.astype(vt_ref.dtype), preferred_element_type=F32)
    m_ref[0], l_ref[0], acc_ref[0] = far_run((i + group - 1) // group)

    def near_block(n, c):
        @pl.when(jnp.logical_not(is_far(n)))
        def _():
            off = pl.multiple_of(n * blk, blk)
            s = masked_scores(off, blk, q_any) + near_bias(off)
            m_ref[0], l_ref[0], acc_ref[0] = _softmax_step(
                s, vt_ref[:, pl.ds(off, blk)], m_ref[0], l_ref[0], acc_ref[0], keys_axis=0)

        return c

    lax.fori_loop(first_ref[i], prev, near_block, 0)
    o_ref[...] = (acc_ref[0] / l_ref[0]).T.astype(o_ref.dtype)


def _moba(proj, positions, rel_bias, *, group=4):
    s = proj.shape[0]
    blk = MOBA_BLOCK
    nblk = s // blk
    assert s % blk == 0 and 2 <= nblk <= V7X_LANES and nblk % group == 0
    pos_blocks = positions.reshape(nblk, blk)
    pmin = jnp.min(pos_blocks, axis=1)
    pmax = jnp.max(pos_blocks, axis=1)
    table = rel_bias[_t5_bucket(jnp.arange(BIAS_TABLE, dtype=jnp.int32))]
    table = jnp.transpose(table).reshape(MOBA_HEADS, 1, BIAS_TABLE).astype(F32)
    table = (table - table[:, :, BIAS_TABLE - 1:]) * LOG2E
    pmax_lanes = jnp.pad(pmax, (0, V7X_LANES - nblk)).reshape(1, V7X_LANES)
    blk_id = jnp.arange(nblk, dtype=jnp.int32)
    pair_start = jnp.maximum(blk_id - 1, 0)
    near = jnp.logical_and(pmin[:, None] - pmax[None, :] < BIAS_TABLE - 1, blk_id[None, :] < pair_start[:, None])
    first_near = jnp.min(jnp.where(near, blk_id[None, :], pair_start[:, None]), axis=1)
    qc, kc, vc = (c // MOBA_DH for c in (COL_MOBA_Q, COL_MOBA_K, COL_MOBA_V))
    grid_spec = pltpu.PrefetchScalarGridSpec(
        num_scalar_prefetch=3,
        grid=(MOBA_HEADS, nblk),
        in_specs=[
            pl.BlockSpec((blk, MOBA_DH), lambda h, i, *_: (i, qc + h)),
            pl.BlockSpec((s, MOBA_DH), lambda h, i, *_: (0, kc + h)),
            pl.BlockSpec((s, MOBA_DH), lambda h, i, *_: (0, vc + h)),
            pl.BlockSpec((blk, 1), lambda h, i, *_: (i, 0)),
            pl.BlockSpec((1, s), lambda h, i, *_: (0, 0)),
            pl.BlockSpec((1, V7X_LANES), lambda h, i, *_: (0, 0)),
            pl.BlockSpec((1, 1, BIAS_TABLE), lambda h, i, *_: (h, 0, 0)),
        ],
        out_specs=pl.BlockSpec((blk, MOBA_DH), lambda h, i, *_: (i, h)),
        scratch_shapes=[
            pltpu.VMEM((V7X_LANES, MOBA_DH), F32),
            pltpu.VMEM((s, 2 * MOBA_DH), BF16),
            pltpu.VMEM((MOBA_DH, s), BF16),
        ] + _softmax_scratch(MOBA_DH, blk, group * blk, keys_axis=0),
    )
    return pl.pallas_call(
        functools.partial(_moba_body, nblk=nblk, scale=MOBA_DH ** -0.5 * LOG2E, group=group),
        grid_spec=grid_spec,
        out_shape=jax.ShapeDtypeStruct((s, MOBA_HEADS * MOBA_DH), BF16),
        compiler_params=_params(("arbitrary", "arbitrary")),
        name="moba",
    )(pmin, pmax, first_near, proj, proj, proj, positions.reshape(s, 1), positions.reshape(1, s), pmax_lanes, table)


def _merge_body(oa_ref, ob_ref, ga_ref, gb_ref, wa_ref, wb_ref, wo_ref, x_ref, o_ref):
    @pl.when(pl.program_id(1) == 0)
    def _():
        o_ref[...] = x_ref[...]

    ma = jnp.dot(oa_ref[...], wa_ref[...], preferred_element_type=F32)
    mb = jnp.dot(ob_ref[...], wb_ref[...], preferred_element_type=F32)
    mg = _sigmoid(ga_ref[...].astype(F32)) * ma + _sigmoid(gb_ref[...].astype(F32)) * mb
    o_ref[...] += jnp.dot(mg.astype(BF16), wo_ref[...], preferred_element_type=F32)


def _merge(o_a, o_b, proj, w_a, w_b, w_out, x, *, tm=512, tn=512):
    s, d = x.shape
    ka, kb = o_a.shape[1], o_b.shape[1]
    ga, gb = COL_GATE_A // tn, COL_GATE_B // tn
    return pl.pallas_call(
        _merge_body,
        grid=(s // tm, d // tn),
        in_specs=[
            pl.BlockSpec((tm, ka), lambda i, j: (i, 0)),
            pl.BlockSpec((tm, kb), lambda i, j: (i, 0)),
            pl.BlockSpec((tm, tn), lambda i, j: (i, ga + j)),
            pl.BlockSpec((tm, tn), lambda i, j: (i, gb + j)),
            pl.BlockSpec((ka, tn), lambda i, j: (0, j)),
            pl.BlockSpec((kb, tn), lambda i, j: (0, j)),
            pl.BlockSpec((tn, d), lambda i, j: (j, 0)),
            _row_resident_spec(tm, d),
        ],
        out_specs=pl.BlockSpec((tm, d), lambda i, j: (i, 0)),
        out_shape=jax.ShapeDtypeStruct((s, d), F32),
        compiler_params=_params(("arbitrary", "arbitrary")),
        name="merge",
    )(o_a, o_b, proj, proj, w_a, w_b, w_out, x)


def _proj_layout(w_in):
    w_lat = jnp.pad(w_in[:, :LATENT_REAL].astype(BF16), ((0, 0), (0, LATENT_COLS - LATENT_REAL)))
    return w_lat, w_in[:, LATENT_REAL:].astype(BF16)


def _q_layout(w_uq):
    r = w_uq.shape[0]
    w = w_uq.reshape(r, MLA_HEADS, MLA_NOPE + MLA_ROPE)
    w = jnp.pad(w, ((0, 0), (0, 0), (0, MLA_QK_PAD - MLA_NOPE - MLA_ROPE)))
    return w.reshape(r, MLA_HEADS * MLA_QK_PAD).astype(BF16)


def kernel(x, positions, rel_bias, norm_ffn1, w_ffn1_in, w_ffn1_out, norm_mix, w_in, norm_cq, w_uq, norm_ckv, w_ukv, w_br_a, w_br_b, w_out, norm_ffn2, w_ffn2_in, w_ffn2_out, norm_final):
    b, s, d = x.shape
    assert b == 1 and norm_ffn1.shape[0] == 1
    pos = positions[0]
    x0 = x[0]

    x1 = _ffn(x0, norm_ffn1[0], w_ffn1_in[0].astype(BF16), w_ffn1_out[0])
    w_lat, w_mix = _proj_layout(w_in[0])
    latent, mixer = _proj(x1, norm_mix[0], w_lat, w_mix)
    tables = _rope_tables(pos)
    q = _q_up(latent, norm_cq[0], _q_layout(w_uq[0]), tables)
    k, v = _kv_up(latent, norm_ckv[0], w_ukv[0].astype(BF16), tables)
    o_a = _mla(q, k, v)
    o_b = _moba(mixer, pos, rel_bias)
    x2 = _merge(o_a, o_b, mixer, w_br_a[0].astype(BF16), w_br_b[0].astype(BF16), w_out[0].astype(BF16), x1)
    out = _ffn(x2, norm_ffn2[0], w_ffn2_in[0].astype(BF16), w_ffn2_out[0], norm_final)
    return out[None]
```

```python
import functools
import math

import jax
import jax.numpy as jnp
from jax import lax
from jax.experimental import pallas as pl
from jax.experimental.pallas import tpu as pltpu

F32 = jnp.float32
BF16 = jnp.bfloat16

RMS_EPS = 1e-6
ROPE_THETA = 10000.0
LOG2E = math.log2(math.e)
MASKED = -1e30

MLA_HEADS = 16
MLA_Q_LORA = 1024
MLA_KV_LORA = 512
MLA_NOPE = 128
MLA_ROPE = 64
MLA_V = 128
MLA_QK_PAD = 256

MOBA_HEADS = 16
MOBA_DH = 128
MOBA_BLOCK = 256
MOBA_TOPK = 3
REL_BUCKETS = 32
REL_MAX_DIST = 128
BIAS_TABLE = 128

V7X_LANES = 128
V7X_VMEM_LIMIT = 56 * 1024 * 1024

LATENT_REAL = MLA_Q_LORA + MLA_KV_LORA + MLA_ROPE
LATENT_COLS = 2048
COL_CQ = 0
COL_CKV = 1024
COL_KROPE = 1536
COL_MOBA_Q = 0
COL_MOBA_K = 2048
COL_MOBA_V = 4096
COL_GATE_A = 6144
COL_GATE_B = 10240

_NT = (((1,), (1,)), ((), ()))


def _params(sem):
    return pltpu.CompilerParams(dimension_semantics=sem, vmem_limit_bytes=V7X_VMEM_LIMIT)


def _row_resident_spec(tm, d):
    return pl.BlockSpec((tm, d), lambda i, j: (i, 0), pipeline_mode=pl.Buffered(1))


def _rms(xf, gain):
    ms = jnp.mean(xf * xf, axis=-1, keepdims=True)
    return xf * lax.rsqrt(ms + RMS_EPS) * gain


def _sigmoid(z):
    return 1.0 / (1.0 + jnp.exp(-z))


def _ffn_body(x_ref, g_ref, wg_ref, wu_ref, wo_ref, *rest, final_norm):
    if final_norm:
        gf_ref, o_ref, h_ref = rest
    else:
        o_ref, h_ref = rest
    j = pl.program_id(1)

    @pl.when(j == 0)
    def _():
        xf = x_ref[...]
        h_ref[...] = _rms(xf, g_ref[...]).astype(BF16)
        o_ref[...] = xf

    h = h_ref[...]
    g = jnp.dot(h, wg_ref[...], preferred_element_type=F32)
    u = jnp.dot(h, wu_ref[...], preferred_element_type=F32)
    a = (g * _sigmoid(g) * (0.5 * u)).astype(BF16)
    o_ref[...] += jnp.dot(a, wo_ref[...].astype(BF16), preferred_element_type=F32)

    if final_norm:
        @pl.when(j == pl.num_programs(1) - 1)
        def _():
            o_ref[...] = _rms(o_ref[...], gf_ref[...])


def _ffn(x, gain, w_in, w_out, final_gain=None, *, tm=512, tf=256):
    s, d = x.shape
    f = w_out.shape[0]
    nf = f // tf
    assert s % tm == 0 and f % tf == 0 and w_in.shape == (d, 2 * f)
    in_specs = [
        _row_resident_spec(tm, d),
        pl.BlockSpec((1, d), lambda i, j: (0, 0)),
        pl.BlockSpec((d, tf), lambda i, j: (0, j)),
        pl.BlockSpec((d, tf), lambda i, j: (0, j + nf)),
        pl.BlockSpec((tf, d), lambda i, j: (j, 0)),
    ]
    args = [x, gain.reshape(1, d), w_in, w_in, w_out]
    if final_gain is not None:
        in_specs.append(pl.BlockSpec((1, d), lambda i, j: (0, 0)))
        args.append(final_gain.reshape(1, d))
    return pl.pallas_call(
        functools.partial(_ffn_body, final_norm=final_gain is not None),
        grid=(s // tm, nf),
        in_specs=in_specs,
        out_specs=pl.BlockSpec((tm, d), lambda i, j: (i, 0)),
        out_shape=jax.ShapeDtypeStruct((s, d), F32),
        scratch_shapes=[pltpu.VMEM((tm, d), BF16)],
        compiler_params=_params(("arbitrary", "arbitrary")),
        name="ffn",
    )(*args)


def _proj_body(x_ref, g_ref, wa_ref, wb_ref, oa_ref, ob_ref, h_ref, *, na):
    j = pl.program_id(1)

    @pl.when(j == 0)
    def _():
        h_ref[...] = _rms(x_ref[...], g_ref[...]).astype(BF16)

    @pl.when(j < na)
    def _():
        oa_ref[...] = jnp.dot(h_ref[...], wa_ref[...], preferred_element_type=F32).astype(oa_ref.dtype)

    @pl.when(j >= na)
    def _():
        ob_ref[...] = jnp.dot(h_ref[...], wb_ref[...], preferred_element_type=F32).astype(ob_ref.dtype)


def _proj(x, gain, w_a, w_b, *, tm=512, tn=1024):
    s, d = x.shape
    na, nb = w_a.shape[1] // tn, w_b.shape[1] // tn
    assert s % tm == 0 and w_a.shape[1] % tn == 0 and w_b.shape[1] % tn == 0
    col_a = lambda j: jnp.minimum(j, na - 1)
    col_b = lambda j: jnp.maximum(j - na, 0)
    return pl.pallas_call(
        functools.partial(_proj_body, na=na),
        grid=(s // tm, na + nb),
        in_specs=[
            _row_resident_spec(tm, d),
            pl.BlockSpec((1, d), lambda i, j: (0, 0)),
            pl.BlockSpec((d, tn), lambda i, j: (0, col_a(j))),
            pl.BlockSpec((d, tn), lambda i, j: (0, col_b(j))),
        ],
        out_specs=[
            pl.BlockSpec((tm, tn), lambda i, j: (i, col_a(j))),
            pl.BlockSpec((tm, tn), lambda i, j: (i, col_b(j))),
        ],
        out_shape=[jax.ShapeDtypeStruct((s, w_a.shape[1]), BF16), jax.ShapeDtypeStruct((s, w_b.shape[1]), BF16)],
        scratch_shapes=[pltpu.VMEM((tm, d), BF16)],
        compiler_params=_params(("arbitrary", "arbitrary")),
        name="proj",
    )(x, gain.reshape(1, d), w_a, w_b)


def _rope_lanes(r, cf, s1, s2):
    return r * cf + pltpu.roll(r, 96, 1) * s1 + pltpu.roll(r, 32, 1) * s2


def _rope_tables(positions):
    half = MLA_ROPE // 2
    inv_freq = ROPE_THETA ** (-jnp.arange(0, MLA_ROPE, 2, dtype=F32) / MLA_ROPE)
    ang = positions[:, None].astype(F32) * inv_freq
    cos, sin = jnp.cos(ang), jnp.sin(ang)
    z = jnp.zeros_like(cos)
    pad = jnp.zeros((positions.shape[0], V7X_LANES - MLA_ROPE), F32)
    cf = jnp.concatenate([cos, cos, pad], axis=1)
    s1 = jnp.concatenate([-sin, z, pad], axis=1)
    s2 = jnp.concatenate([z, sin, pad], axis=1)
    return cf, s1, s2


def _qup_body(c_ref, g_ref, w_ref, cf_ref, s1_ref, s2_ref, q_ref, h_ref, *, scale):
    @pl.when(pl.program_id(1) == 0)
    def _():
        h_ref[...] = _rms(c_ref[...].astype(F32), g_ref[...]).astype(BF16)

    y = jnp.dot(h_ref[...], w_ref[...], preferred_element_type=F32)
    for hh in range(q_ref.shape[0]):
        c0 = hh * MLA_QK_PAD
        q_ref[hh, :, :MLA_NOPE] = (y[:, c0:c0 + MLA_NOPE] * scale).astype(q_ref.dtype)
        r = _rope_lanes(y[:, c0 + MLA_NOPE:c0 + MLA_QK_PAD], cf_ref[...], s1_ref[...], s2_ref[...])
        q_ref[hh, :, MLA_NOPE:] = (r * scale).astype(q_ref.dtype)


def _q_up(proj, gain, w_q, tables, *, tm=1024, hg=4):
    s = proj.shape[0]
    cf, s1, s2 = tables
    scale = (MLA_NOPE + MLA_ROPE) ** -0.5 * LOG2E
    tab_spec = pl.BlockSpec((tm, V7X_LANES), lambda i, h: (i, 0))
    return pl.pallas_call(
        functools.partial(_qup_body, scale=scale),
        grid=(s // tm, MLA_HEADS // hg),
        in_specs=[
            pl.BlockSpec((tm, MLA_Q_LORA), lambda i, h: (i, COL_CQ // MLA_Q_LORA)),
            pl.BlockSpec((1, MLA_Q_LORA), lambda i, h: (0, 0)),
            pl.BlockSpec((MLA_Q_LORA, hg * MLA_QK_PAD), lambda i, h: (0, h)),
            tab_spec, tab_spec, tab_spec,
        ],
        out_specs=pl.BlockSpec((hg, tm, MLA_QK_PAD), lambda i, h: (h, i, 0)),
        out_shape=jax.ShapeDtypeStruct((MLA_HEADS, s, MLA_QK_PAD), BF16),
        scratch_shapes=[pltpu.VMEM((tm, MLA_Q_LORA), BF16)],
        compiler_params=_params(("arbitrary", "arbitrary")),
        name="q_up",
    )(proj, gain.reshape(1, MLA_Q_LORA), w_q, cf, s1, s2)


def _kvup_body(c_ref, kr_ref, g_ref, w_ref, cf_ref, s1_ref, s2_ref, k_ref, v_ref, h_ref, r_ref):
    @pl.when(pl.program_id(1) == 0)
    def _():
        h_ref[...] = _rms(c_ref[...].astype(F32), g_ref[...]).astype(BF16)
        r_ref[...] = _rope_lanes(kr_ref[...].astype(F32), cf_ref[...], s1_ref[...], s2_ref[...]).astype(BF16)

    y = jnp.dot(h_ref[...], w_ref[...], preferred_element_type=F32)
    for hh in range(k_ref.shape[0]):
        c0 = hh * (MLA_NOPE + MLA_V)
        k_ref[hh, :, :MLA_NOPE] = y[:, c0:c0 + MLA_NOPE].astype(k_ref.dtype)
        k_ref[hh, :, MLA_NOPE:] = r_ref[...]
        v_ref[hh] = y[:, c0 + MLA_NOPE:c0 + MLA_NOPE + MLA_V].astype(v_ref.dtype)


def _kv_up(proj, gain, w_ukv, tables, *, tm=1024, hg=4):
    s = proj.shape[0]
    cf, s1, s2 = tables
    tab_spec = pl.BlockSpec((tm, V7X_LANES), lambda i, h: (i, 0))
    return pl.pallas_call(
        _kvup_body,
        grid=(s // tm, MLA_HEADS // hg),
        in_specs=[
            pl.BlockSpec((tm, MLA_KV_LORA), lambda i, h: (i, COL_CKV // MLA_KV_LORA)),
            pl.BlockSpec((tm, V7X_LANES), lambda i, h: (i, COL_KROPE // V7X_LANES)),
            pl.BlockSpec((1, MLA_KV_LORA), lambda i, h: (0, 0)),
            pl.BlockSpec((MLA_KV_LORA, hg * (MLA_NOPE + MLA_V)), lambda i, h: (0, h)),
            tab_spec, tab_spec, tab_spec,
        ],
        out_specs=[
            pl.BlockSpec((hg, tm, MLA_QK_PAD), lambda i, h: (h, i, 0)),
            pl.BlockSpec((hg, tm, MLA_V), lambda i, h: (h, i, 0)),
        ],
        out_shape=[
            jax.ShapeDtypeStruct((MLA_HEADS, s, MLA_QK_PAD), BF16),
            jax.ShapeDtypeStruct((MLA_HEADS, s, MLA_V), BF16),
        ],
        scratch_shapes=[pltpu.VMEM((tm, MLA_KV_LORA), BF16), pltpu.VMEM((tm, V7X_LANES), BF16)],
        compiler_params=_params(("arbitrary", "arbitrary")),
        name="kv_up",
    )(proj, proj, gain.reshape(1, MLA_KV_LORA), w_ukv, cf, s1, s2)


def _pv(values, p, keys_axis):
    if keys_axis == 0:
        return jnp.dot(values, p, preferred_element_type=F32)
    return jnp.dot(p, values, preferred_element_type=F32)


def _softmax_step(s, values, m, l, acc, keys_axis):
    m_new = jnp.maximum(m, jnp.max(s, axis=keys_axis, keepdims=True))
    alpha = jnp.exp2(m - m_new)
    p = jnp.exp2(s - m_new)
    l = alpha * l + jnp.sum(p, axis=keys_axis, keepdims=True)
    acc = alpha * acc + _pv(values, p.astype(values.dtype), keys_axis)
    return m_new, l, acc


def _softmax_chunks(scores_fn, values_fn, last_chunk, state, s_ref, p_ref, keys_axis):
    m_ref, l_ref, acc_ref, a_ref, cmax_ref = state

    def flush(g, slot):
        pv = _pv(values_fn(jnp.maximum(g, 0)), p_ref[slot], keys_axis)
        acc_ref[slot] = a_ref[slot] * acc_ref[slot] + pv

    def produce(g, slot):
        s = scores_fn(jnp.minimum(g, last_chunk))
        s_ref[slot] = s
        cmax_ref[slot] = jnp.max(s, axis=keys_axis, keepdims=True)

    def step(g, cur, produce_next=True):
        m = m_ref[cur]
        m_new = jnp.maximum(m, cmax_ref[cur])
        alpha = jnp.exp2(m - m_new)
        p = jnp.exp2(s_ref[cur] - m_new)
        l_ref[cur] = alpha * l_ref[cur] + jnp.sum(p, axis=keys_axis, keepdims=True)
        p_ref[cur] = p.astype(p_ref.dtype)
        m_ref[cur] = m_new
        a_ref[cur] = alpha
        if produce_next:
            produce(g + 1, 1 - cur)
        flush(g - 1, 1 - cur)

    def begin():
        produce(0, 0)
        p_ref[1] = jnp.zeros(p_ref.shape[1:], p_ref.dtype)
        a_ref[0] = jnp.ones(a_ref.shape[1:], a_ref.dtype)
        a_ref[1] = jnp.ones(a_ref.shape[1:], a_ref.dtype)
        m_ref[1] = jnp.full(m_ref.shape[1:], -jnp.inf, F32)
        l_ref[1] = jnp.zeros(l_ref.shape[1:], F32)
        acc_ref[1] = jnp.zeros(acc_ref.shape[1:], F32)

    def run(n):
        def pair(k, c):
            step(2 * k, 0)
            step(2 * k + 1, 1)
            return c

        lax.fori_loop(0, n // 2, pair, 0)

        @pl.when(n % 2 == 1)
        def _():
            step(n - 1, 0, produce_next=False)
            flush(n - 1, 0)

        @pl.when(n % 2 == 0)
        def _():
            flush(n - 1, 1)

        m = jnp.maximum(m_ref[0], m_ref[1])
        w0, w1 = jnp.exp2(m_ref[0] - m), jnp.exp2(m_ref[1] - m)
        return m, w0 * l_ref[0] + w1 * l_ref[1], w0 * acc_ref[0] + w1 * acc_ref[1]

    return begin, run


class _Slots:
    def __init__(self, ref0, ref1):
        self.refs = (ref0, ref1)
        self.shape = (2,) + tuple(ref0.shape)
        self.dtype = ref0.dtype

    def __getitem__(self, slot):
        return self.refs[slot][...]

    def __setitem__(self, slot, value):
        self.refs[slot][...] = value


def _softmax_scratch(dv, tq, span, keys_axis):
    stat, acc, tile = ((1, tq), (dv, tq), (span, tq)) if keys_axis == 0 else ((tq, 1), (tq, dv), (tq, span))
    kinds = ((stat, F32), (stat, F32), (acc, F32), (stat, F32), (stat, F32), (tile, F32), (tile, BF16))
    return [pltpu.VMEM(shape, dtype) for shape, dtype in kinds for _ in range(2)]


def _softmax_slots(refs):
    assert len(refs) == 14
    return [_Slots(refs[2 * k], refs[2 * k + 1]) for k in range(7)]


def _mla_body(q_ref, k_ref, v_ref, o_ref, *, t, per_trip):
    i = pl.program_id(1)
    q = q_ref[0]

    def scores(n):
        return lax.dot_general(q, k_ref[0, pl.ds(pl.multiple_of(n * t, t), t), :], _NT, preferred_element_type=F32)

    def values(n):
        return v_ref[0, pl.ds(pl.multiple_of(n * t, t), t), :]

    def body(n, c):
        return _softmax_step(scores(n), values(n), *c, keys_axis=1)

    def body_group(k, c):
        for u in range(per_trip):
            c = body(per_trip * k + u, c)
        return c

    init = (jnp.full((t, 1), -jnp.inf, F32), jnp.zeros((t, 1), F32), jnp.zeros((t, MLA_V), F32))
    state = lax.fori_loop(0, i // per_trip, body_group, init)
    m, l, acc = lax.fori_loop((i // per_trip) * per_trip, i, body, state)
    row = lax.broadcasted_iota(jnp.int32, (t, t), 0)
    col = lax.broadcasted_iota(jnp.int32, (t, t), 1)
    m, l, acc = _softmax_step(jnp.where(col <= row, scores(i), -jnp.inf), values(i), m, l, acc, keys_axis=1)
    o_ref[...] = (acc / l).astype(o_ref.dtype)


def _mla(q, k, v, *, t=1024, per_trip=4):
    h, s, _ = q.shape
    assert s % t == 0
    return pl.pallas_call(
        functools.partial(_mla_body, t=t, per_trip=per_trip),
        grid=(h, s // t),
        in_specs=[
            pl.BlockSpec((1, t, MLA_QK_PAD), lambda h, i: (h, i, 0)),
            pl.BlockSpec((1, s, MLA_QK_PAD), lambda h, i: (h, 0, 0)),
            pl.BlockSpec((1, s, MLA_V), lambda h, i: (h, 0, 0)),
        ],
        out_specs=pl.BlockSpec((t, MLA_V), lambda h, i: (i, h)),
        out_shape=jax.ShapeDtypeStruct((s, h * MLA_V), BF16),
        compiler_params=_params(("arbitrary", "arbitrary")),
        name="mla",
    )(q, k, v)


def _t5_bucket(n):
    max_exact = REL_BUCKETS // 2
    n_f = jnp.maximum(n, max_exact).astype(F32)
    large = max_exact + (jnp.log(n_f / max_exact) / math.log(REL_MAX_DIST / max_exact)
                         * (REL_BUCKETS - max_exact)).astype(jnp.int32)
    large = jnp.minimum(large, REL_BUCKETS - 1)
    return jnp.where(n < max_exact, n, large)


def _moba_body(pmin_ref, pmax_ref, first_ref, q_ref, k_ref, v_ref, pq_ref, pk_ref, pmaxl_ref, tbl_ref, o_ref,
               kmean_ref, kaug_ref, vt_ref, *softmax_refs, nblk, scale, group):
    i = pl.program_id(1)
    blk = MOBA_BLOCK
    lane_id = lax.broadcasted_iota(jnp.int32, (blk, V7X_LANES), 1)
    m_ref, l_ref, acc_ref, a_ref, cmax_ref, s_ref, p_ref = _softmax_slots(softmax_refs)

    @pl.when(i == 0)
    def _():
        kmean_ref[...] = jnp.zeros_like(kmean_ref)

        def fill(n, c):
            rows = pl.ds(pl.multiple_of(n * blk, blk), blk)
            kb = k_ref[rows, :]
            kmean_ref[pl.ds(n, 1), :] = jnp.sum(kb.astype(F32), axis=0, keepdims=True) * (1.0 / blk)
            kaug_ref[rows, :MOBA_DH] = kb
            kaug_ref[rows, MOBA_DH:] = jnp.where(lane_id == n, 1.0, 0.0).astype(kaug_ref.dtype)
            vt_ref[:, rows] = v_ref[rows, :].astype(F32).T.astype(vt_ref.dtype)
            return c

        lax.fori_loop(0, nblk, fill, 0)

    q = q_ref[...]
    qs = (q.astype(F32) * scale).astype(q.dtype)

    gate = lax.dot_general(kmean_ref[...], q.astype(F32), _NT, preferred_element_type=F32)
    bid = lax.broadcasted_iota(jnp.int32, gate.shape, 0)
    bid_f = bid.astype(F32)
    g = jnp.where(bid < i, gate, -jnp.inf)
    picked_t = jnp.zeros(gate.shape, F32)
    for _ in range(min(MOBA_TOPK, nblk)):
        mx = jnp.max(g, axis=0, keepdims=True)
        idx = jnp.min(jnp.where(g == mx, bid_f, float(V7X_LANES)), axis=0, keepdims=True)
        hit = bid_f == jnp.where(mx > -jnp.inf, idx, -1.0)
        picked_t = jnp.where(hit, 1.0, picked_t)
        g = jnp.where(hit, -jnp.inf, g)
    picked = picked_t.T

    prev = jnp.maximum(i - 1, 0)
    far_lanes = jnp.logical_and((pmin_ref[i] - pmaxl_ref[...]) >= BIAS_TABLE - 1, lane_id[:1] != prev)
    m_any = jnp.where(picked > 0.0, 0.0, MASKED)
    q_any = jnp.concatenate([qs, m_any.astype(qs.dtype)], axis=1)
    q_far = jnp.concatenate([qs, jnp.where(far_lanes, m_any, MASKED).astype(qs.dtype)], axis=1)
    q_diag = jnp.concatenate([qs, jnp.where(lane_id == i, 0.0, m_any).astype(qs.dtype)], axis=1)

    def is_far(n):
        return (pmin_ref[i] - pmax_ref[n]) >= BIAS_TABLE - 1

    def masked_scores(off, width, q_aug):
        return lax.dot_general(kaug_ref[pl.ds(off, width), :], q_aug, _NT, preferred_element_type=F32)

    def near_bias(off):
        off = pl.multiple_of(off, blk)
        d = jnp.clip(pq_ref[...] - pk_ref[:, pl.ds(off, blk)], 0, BIAS_TABLE - 1)
        tb = jnp.broadcast_to(tbl_ref[0], (blk, BIAS_TABLE))
        parts = [jnp.take_along_axis(tb, d[:, c * V7X_LANES:(c + 1) * V7X_LANES], axis=1)
                 for c in range(blk // V7X_LANES)]
        return jnp.concatenate(parts, axis=1).T

    span = group * blk

    def far_scores(g):
        return masked_scores(pl.multiple_of(g * span, span), span, q_far)

    def far_values_t(g):
        return vt_ref[:, pl.ds(pl.multiple_of(g * span, span), span)]

    far_begin, far_run = _softmax_chunks(far_scores, far_values_t, nblk // group - 1,
                                         (m_ref, l_ref, acc_ref, a_ref, cmax_ref), s_ref, p_ref, keys_axis=0)
    off_d = pl.multiple_of(prev * blk, blk)
    key = prev * blk + lax.broadcasted_iota(jnp.int32, (2 * blk, blk), 0)
    qry = i * blk + lax.broadcasted_iota(jnp.int32, (2 * blk, blk), 1)
    s = masked_scores(off_d, 2 * blk, q_diag)
    far_begin()
    s = s + jnp.concatenate([near_bias(off_d), near_bias(off_d + blk)], axis=0)
    s = jnp.where(key <= qry, s, -jnp.inf)
    m = jnp.max(s, axis=0, keepdims=True)
    p = jnp.exp2(s - m)
    m_ref[0] = m
    l_ref[0] = jnp.sum(p, axis=0, keepdims=True)
    acc_ref[0] = jnp.dot(vt_ref[:, pl.ds(off_d, 2 * blk)], p.astype(vt_ref.dtype), preferred_element_type=F32)
    m_ref[0], l_ref[0], acc_ref[0] = far_run((i + group - 1) // group)

    def near_block(n, c):
        @pl.when(jnp.logical_not(is_far(n)))
        def _():
            off = pl.multiple_of(n * blk, blk)
            s = masked_scores(off, blk, q_any) + near_bias(off)
            m_ref[0], l_ref[0], acc_ref[0] = _softmax_step(
                s, vt_ref[:, pl.ds(off, blk)], m_ref[0], l_ref[0], acc_ref[0], keys_axis=0)

        return c

    lax.fori_loop(first_ref[i], prev, near_block, 0)
    o_ref[...] = (acc_ref[0] / l_ref[0]).T.astype(o_ref.dtype)


def _moba(proj, positions, rel_bias, *, group=4):
    s = proj.shape[0]
    blk = MOBA_BLOCK
    nblk = s // blk
    assert s % blk == 0 and 2 <= nblk <= V7X_LANES and nblk % group == 0
    pos_blocks = positions.reshape(nblk, blk)
    pmin = jnp.min(pos_blocks, axis=1)
    pmax = jnp.max(pos_blocks, axis=1)
    table = rel_bias[_t5_bucket(jnp.arange(BIAS_TABLE, dtype=jnp.int32))]
    table = jnp.transpose(table).reshape(MOBA_HEADS, 1, BIAS_TABLE).astype(F32)
    table = (table - table[:, :, BIAS_TABLE - 1:]) * LOG2E
    pmax_lanes = jnp.pad(pmax, (0, V7X_LANES - nblk)).reshape(1, V7X_LANES)
    blk_id = jnp.arange(nblk, dtype=jnp.int32)
    pair_start = jnp.maximum(blk_id - 1, 0)
    near = jnp.logical_and(pmin[:, None] - pmax[None, :] < BIAS_TABLE - 1, blk_id[None, :] < pair_start[:, None])
    first_near = jnp.min(jnp.where(near, blk_id[None, :], pair_start[:, None]), axis=1)
    qc, kc, vc = (c // MOBA_DH for c in (COL_MOBA_Q, COL_MOBA_K, COL_MOBA_V))
    grid_spec = pltpu.PrefetchScalarGridSpec(
        num_scalar_prefetch=3,
        grid=(MOBA_HEADS, nblk),
        in_specs=[
            pl.BlockSpec((blk, MOBA_DH), lambda h, i, *_: (i, qc + h)),
            pl.BlockSpec((s, MOBA_DH), lambda h, i, *_: (0, kc + h)),
            pl.BlockSpec((s, MOBA_DH), lambda h, i, *_: (0, vc + h)),
            pl.BlockSpec((blk, 1), lambda h, i, *_: (i, 0)),
            pl.BlockSpec((1, s), lambda h, i, *_: (0, 0)),
            pl.BlockSpec((1, V7X_LANES), lambda h, i, *_: (0, 0)),
            pl.BlockSpec((1, 1, BIAS_TABLE), lambda h, i, *_: (h, 0, 0)),
        ],
        out_specs=pl.BlockSpec((blk, MOBA_DH), lambda h, i, *_: (i, h)),
        scratch_shapes=[
            pltpu.VMEM((V7X_LANES, MOBA_DH), F32),
            pltpu.VMEM((s, 2 * MOBA_DH), BF16),
            pltpu.VMEM((MOBA_DH, s), BF16),
        ] + _softmax_scratch(MOBA_DH, blk, group * blk, keys_axis=0),
    )
    return pl.pallas_call(
        functools.partial(_moba_body, nblk=nblk, scale=MOBA_DH ** -0.5 * LOG2E, group=group),
        grid_spec=grid_spec,
        out_shape=jax.ShapeDtypeStruct((s, MOBA_HEADS * MOBA_DH), BF16),
        compiler_params=_params(("arbitrary", "arbitrary")),
        name="moba",
    )(pmin, pmax, first_near, proj, proj, proj, positions.reshape(s, 1), positions.reshape(1, s), pmax_lanes, table)


def _merge_body(oa_ref, ob_ref, ga_ref, gb_ref, wa_ref, wb_ref, wo_ref, x_ref, o_ref):
    @pl.when(pl.program_id(1) == 0)
    def _():
        o_ref[...] = x_ref[...]

    ma = jnp.dot(oa_ref[...], wa_ref[...], preferred_element_type=F32)
    mb = jnp.dot(ob_ref[...], wb_ref[...], preferred_element_type=F32)
    mg = _sigmoid(ga_ref[...].astype(F32)) * ma + _sigmoid(gb_ref[...].astype(F32)) * mb
    o_ref[...] += jnp.dot(mg.astype(BF16), wo_ref[...], preferred_element_type=F32)


def _merge(o_a, o_b, proj, w_a, w_b, w_out, x, *, tm=512, tn=512):
    s, d = x.shape
    ka, kb = o_a.shape[1], o_b.shape[1]
    ga, gb = COL_GATE_A // tn, COL_GATE_B // tn
    return pl.pallas_call(
        _merge_body,
        grid=(s // tm, d // tn),
        in_specs=[
            pl.BlockSpec((tm, ka), lambda i, j: (i, 0)),
            pl.BlockSpec((tm, kb), lambda i, j: (i, 0)),
            pl.BlockSpec((tm, tn), lambda i, j: (i, ga + j)),
            pl.BlockSpec((tm, tn), lambda i, j: (i, gb + j)),
            pl.BlockSpec((ka, tn), lambda i, j: (0, j)),
            pl.BlockSpec((kb, tn), lambda i, j: (0, j)),
            pl.BlockSpec((tn, d), lambda i, j: (j, 0)),
            _row_resident_spec(tm, d),
        ],
        out_specs=pl.BlockSpec((tm, d), lambda i, j: (i, 0)),
        out_shape=jax.ShapeDtypeStruct((s, d), F32),
        compiler_params=_params(("arbitrary", "arbitrary")),
        name="merge",
    )(o_a, o_b, proj, proj, w_a, w_b, w_out, x)


def _proj_layout(w_in):
    w_lat = jnp.pad(w_in[:, :LATENT_REAL].astype(BF16), ((0, 0), (0, LATENT_COLS - LATENT_REAL)))
    return w_lat, w_in[:, LATENT_REAL:].astype(BF16)


def _q_layout(w_uq):
    r = w_uq.shape[0]
    w = w_uq.reshape(r, MLA_HEADS, MLA_NOPE + MLA_ROPE)
    w = jnp.pad(w, ((0, 0), (0, 0), (0, MLA_QK_PAD - MLA_NOPE - MLA_ROPE)))
    return w.reshape(r, MLA_HEADS * MLA_QK_PAD).astype(BF16)


def kernel(x, positions, rel_bias, norm_ffn1, w_ffn1_in, w_ffn1_out, norm_mix, w_in, norm_cq, w_uq, norm_ckv, w_ukv, w_br_a, w_br_b, w_out, norm_ffn2, w_ffn2_in, w_ffn2_out, norm_final):
    b, s, d = x.shape
    assert b == 1 and norm_ffn1.shape[0] == 1
    pos = positions[0]
    x0 = x[0]

    x1 = _ffn(x0, norm_ffn1[0], w_ffn1_in[0].astype(BF16), w_ffn1_out[0])
    w_lat, w_mix = _proj_layout(w_in[0])
    latent, mixer = _proj(x1, norm_mix[0], w_lat, w_mix)
    tables = _rope_tables(pos)
    q = _q_up(latent, norm_cq[0], _q_layout(w_uq[0]), tables)
    k, v = _kv_up(latent, norm_ckv[0], w_ukv[0].astype(BF16), tables)
    o_a = _mla(q, k, v)
    o_b = _moba(mixer, pos, rel_bias)
    x2 = _merge(o_a, o_b, mixer, w_br_a[0].astype(BF16), w_br_b[0].astype(BF16), w_out[0].astype(BF16), x1)
    out = _ffn(x2, norm_ffn2[0], w_ffn2_in[0].astype(BF16), w_ffn2_out[0], norm_final)
    return out[None]
```

```python
import functools
import math

import jax
import jax.numpy as jnp
from jax import lax
from jax.experimental import pallas as pl
from jax.experimental.pallas import tpu as pltpu

F32 = jnp.float32
BF16 = jnp.bfloat16

RMS_EPS = 1e-6
ROPE_THETA = 10000.0
LOG2E = math.log2(math.e)
MASKED = -1e30

MLA_HEADS = 16
MLA_Q_LORA = 1024
MLA_KV_LORA = 512
MLA_NOPE = 128
MLA_ROPE = 64
MLA_V = 128
MLA_QK_PAD = 256

MOBA_HEADS = 16
MOBA_DH = 128
MOBA_BLOCK = 256
MOBA_TOPK = 3
REL_BUCKETS = 32
REL_MAX_DIST = 128
BIAS_TABLE = 128

V7X_LANES = 128
V7X_VMEM_LIMIT = 56 * 1024 * 1024

LATENT_REAL = MLA_Q_LORA + MLA_KV_LORA + MLA_ROPE
LATENT_COLS = 2048
COL_CQ = 0
COL_CKV = 1024
COL_KROPE = 1536
COL_MOBA_Q = 0
COL_MOBA_K = 2048
COL_MOBA_V = 4096
COL_GATE_A = 6144
COL_GATE_B = 10240

_NT = (((1,), (1,)), ((), ()))


def _params(sem):
    return pltpu.CompilerParams(dimension_semantics=sem, vmem_limit_bytes=V7X_VMEM_LIMIT)


def _row_resident_spec(tm, d):
    return pl.BlockSpec((tm, d), lambda i, j: (i, 0), pipeline_mode=pl.Buffered(1))


def _rms(xf, gain):
    ms = jnp.mean(xf * xf, axis=-1, keepdims=True)
    return xf * lax.rsqrt(ms + RMS_EPS) * gain


def _sigmoid(z):
    return 1.0 / (1.0 + jnp.exp(-z))


def _ffn_body(x_ref, g_ref, wg_ref, wu_ref, wo_ref, *rest, final_norm):
    if final_norm:
        gf_ref, o_ref, h_ref = rest
    else:
        o_ref, h_ref = rest
    j = pl.program_id(1)

    @pl.when(j == 0)
    def _():
        xf = x_ref[...]
        h_ref[...] = _rms(xf, g_ref[...]).astype(BF16)
        o_ref[...] = xf

    h = h_ref[...]
    g = jnp.dot(h, wg_ref[...], preferred_element_type=F32)
    u = jnp.dot(h, wu_ref[...], preferred_element_type=F32)
    a = (g * _sigmoid(g) * (0.5 * u)).astype(BF16)
    o_ref[...] += jnp.dot(a, wo_ref[...].astype(BF16), preferred_element_type=F32)

    if final_norm:
        @pl.when(j == pl.num_programs(1) - 1)
        def _():
            o_ref[...] = _rms(o_ref[...], gf_ref[...])


def _ffn(x, gain, w_in, w_out, final_gain=None, *, tm=512, tf=256):
    s, d = x.shape
    f = w_out.shape[0]
    nf = f // tf
    assert s % tm == 0 and f % tf == 0 and w_in.shape == (d, 2 * f)
    in_specs = [
        _row_resident_spec(tm, d),
        pl.BlockSpec((1, d), lambda i, j: (0, 0)),
        pl.BlockSpec((d, tf), lambda i, j: (0, j)),
        pl.BlockSpec((d, tf), lambda i, j: (0, j + nf)),
        pl.BlockSpec((tf, d), lambda i, j: (j, 0)),
    ]
    args = [x, gain.reshape(1, d), w_in, w_in, w_out]
    if final_gain is not None:
        in_specs.append(pl.BlockSpec((1, d), lambda i, j: (0, 0)))
        args.append(final_gain.reshape(1, d))
    return pl.pallas_call(
        functools.partial(_ffn_body, final_norm=final_gain is not None),
        grid=(s // tm, nf),
        in_specs=in_specs,
        out_specs=pl.BlockSpec((tm, d), lambda i, j: (i, 0)),
        out_shape=jax.ShapeDtypeStruct((s, d), F32),
        scratch_shapes=[pltpu.VMEM((tm, d), BF16)],
        compiler_params=_params(("arbitrary", "arbitrary")),
        name="ffn",
    )(*args)


def _proj_body(x_ref, g_ref, wa_ref, wb_ref, oa_ref, ob_ref, h_ref, *, na):
    j = pl.program_id(1)

    @pl.when(j == 0)
    def _():
        h_ref[...] = _rms(x_ref[...], g_ref[...]).astype(BF16)

    @pl.when(j < na)
    def _():
        oa_ref[...] = jnp.dot(h_ref[...], wa_ref[...], preferred_element_type=F32).astype(oa_ref.dtype)

    @pl.when(j >= na)
    def _():
        ob_ref[...] = jnp.dot(h_ref[...], wb_ref[...], preferred_element_type=F32).astype(ob_ref.dtype)


def _proj(x, gain, w_a, w_b, *, tm=512, tn=1024):
    s, d = x.shape
    na, nb = w_a.shape[1] // tn, w_b.shape[1] // tn
    assert s % tm == 0 and w_a.shape[1] % tn == 0 and w_b.shape[1] % tn == 0
    col_a = lambda j: jnp.minimum(j, na - 1)
    col_b = lambda j: jnp.maximum(j - na, 0)
    return pl.pallas_call(
        functools.partial(_proj_body, na=na),
        grid=(s // tm, na + nb),
        in_specs=[
            _row_resident_spec(tm, d),
            pl.BlockSpec((1, d), lambda i, j: (0, 0)),
            pl.BlockSpec((d, tn), lambda i, j: (0, col_a(j))),
            pl.BlockSpec((d, tn), lambda i, j: (0, col_b(j))),
        ],
        out_specs=[
            pl.BlockSpec((tm, tn), lambda i, j: (i, col_a(j))),
            pl.BlockSpec((tm, tn), lambda i, j: (i, col_b(j))),
        ],
        out_shape=[jax.ShapeDtypeStruct((s, w_a.shape[1]), BF16), jax.ShapeDtypeStruct((s, w_b.shape[1]), BF16)],
        scratch_shapes=[pltpu.VMEM((tm, d), BF16)],
        compiler_params=_params(("arbitrary", "arbitrary")),
        name="proj",
    )(x, gain.reshape(1, d), w_a, w_b)


def _rope_lanes(r, cf, s1, s2):
    return r * cf + pltpu.roll(r, 96, 1) * s1 + pltpu.roll(r, 32, 1) * s2


def _rope_tables(positions):
    half = MLA_ROPE // 2
    inv_freq = ROPE_THETA ** (-jnp.arange(0, MLA_ROPE, 2, dtype=F32) / MLA_ROPE)
    ang = positions[:, None].astype(F32) * inv_freq
    cos, sin = jnp.cos(ang), jnp.sin(ang)
    z = jnp.zeros_like(cos)
    pad = jnp.zeros((positions.shape[0], V7X_LANES - MLA_ROPE), F32)
    cf = jnp.concatenate([cos, cos, pad], axis=1)
    s1 = jnp.concatenate([-sin, z, pad], axis=1)
    s2 = jnp.concatenate([z, sin, pad], axis=1)
    return cf, s1, s2


def _qup_body(c_ref, g_ref, w_ref, cf_ref, s1_ref, s2_ref, q_ref, h_ref, *, scale):
    @pl.when(pl.program_id(1) == 0)
    def _():
        h_ref[...] = _rms(c_ref[...].astype(F32), g_ref[...]).astype(BF16)

    y = jnp.dot(h_ref[...], w_ref[...], preferred_element_type=F32)
    for hh in range(q_ref.shape[0]):
        c0 = hh * MLA_QK_PAD
        q_ref[hh, :, :MLA_NOPE] = (y[:, c0:c0 + MLA_NOPE] * scale).astype(q_ref.dtype)
        r = _rope_lanes(y[:, c0 + MLA_NOPE:c0 + MLA_QK_PAD], cf_ref[...], s1_ref[...], s2_ref[...])
        q_ref[hh, :, MLA_NOPE:] = (r * scale).astype(q_ref.dtype)


def _q_up(proj, gain, w_q, tables, *, tm=1024, hg=4):
    s = proj.shape[0]
    cf, s1, s2 = tables
    scale = (MLA_NOPE + MLA_ROPE) ** -0.5 * LOG2E
    tab_spec = pl.BlockSpec((tm, V7X_LANES), lambda i, h: (i, 0))
    return pl.pallas_call(
        functools.partial(_qup_body, scale=scale),
        grid=(s // tm, MLA_HEADS // hg),
        in_specs=[
            pl.BlockSpec((tm, MLA_Q_LORA), lambda i, h: (i, COL_CQ // MLA_Q_LORA)),
            pl.BlockSpec((1, MLA_Q_LORA), lambda i, h: (0, 0)),
            pl.BlockSpec((MLA_Q_LORA, hg * MLA_QK_PAD), lambda i, h: (0, h)),
            tab_spec, tab_spec, tab_spec,
        ],
        out_specs=pl.BlockSpec((hg, tm, MLA_QK_PAD), lambda i, h: (h, i, 0)),
        out_shape=jax.ShapeDtypeStruct((MLA_HEADS, s, MLA_QK_PAD), BF16),
        scratch_shapes=[pltpu.VMEM((tm, MLA_Q_LORA), BF16)],
        compiler_params=_params(("arbitrary", "arbitrary")),
        name="q_up",
    )(proj, gain.reshape(1, MLA_Q_LORA), w_q, cf, s1, s2)


def _kvup_body(c_ref, kr_ref, g_ref, w_ref, cf_ref, s1_ref, s2_ref, k_ref, v_ref, h_ref, r_ref):
    @pl.when(pl.program_id(1) == 0)
    def _():
        h_ref[...] = _rms(c_ref[...].astype(F32), g_ref[...]).astype(BF16)
        r_ref[...] = _rope_lanes(kr_ref[...].astype(F32), cf_ref[...], s1_ref[...], s2_ref[...]).astype(BF16)

    y = jnp.dot(h_ref[...], w_ref[...], preferred_element_type=F32)
    for hh in range(k_ref.shape[0]):
        c0 = hh * (MLA_NOPE + MLA_V)
        k_ref[hh, :, :MLA_NOPE] = y[:, c0:c0 + MLA_NOPE].astype(k_ref.dtype)
        k_ref[hh, :, MLA_NOPE:] = r_ref[...]
        v_ref[hh] = y[:, c0 + MLA_NOPE:c0 + MLA_NOPE + MLA_V].astype(v_ref.dtype)


def _kv_up(proj, gain, w_ukv, tables, *, tm=1024, hg=4):
    s = proj.shape[0]
    cf, s1, s2 = tables
    tab_spec = pl.BlockSpec((tm, V7X_LANES), lambda i, h: (i, 0))
    return pl.pallas_call(
        _kvup_body,
        grid=(s // tm, MLA_HEADS // hg),
        in_specs=[
            pl.BlockSpec((tm, MLA_KV_LORA), lambda i, h: (i, COL_CKV // MLA_KV_LORA)),
            pl.BlockSpec((tm, V7X_LANES), lambda i, h: (i, COL_KROPE // V7X_LANES)),
            pl.BlockSpec((1, MLA_KV_LORA), lambda i, h: (0, 0)),
            pl.BlockSpec((MLA_KV_LORA, hg * (MLA_NOPE + MLA_V)), lambda i, h: (0, h)),
            tab_spec, tab_spec, tab_spec,
        ],
        out_specs=[
            pl.BlockSpec((hg, tm, MLA_QK_PAD), lambda i, h: (h, i, 0)),
            pl.BlockSpec((hg, tm, MLA_V), lambda i, h: (h, i, 0)),
        ],
        out_shape=[
            jax.ShapeDtypeStruct((MLA_HEADS, s, MLA_QK_PAD), BF16),
            jax.ShapeDtypeStruct((MLA_HEADS, s, MLA_V), BF16),
        ],
        scratch_shapes=[pltpu.VMEM((tm, MLA_KV_LORA), BF16), pltpu.VMEM((tm, V7X_LANES), BF16)],
        compiler_params=_params(("arbitrary", "arbitrary")),
        name="kv_up",
    )(proj, proj, gain.reshape(1, MLA_KV_LORA), w_ukv, cf, s1, s2)


def _pv(values, p, keys_axis):
    if keys_axis == 0:
        return jnp.dot(values, p, preferred_element_type=F32)
    return jnp.dot(p, values, preferred_element_type=F32)


def _softmax_step(s, values, m, l, acc, keys_axis):
    m_new = jnp.maximum(m, jnp.max(s, axis=keys_axis, keepdims=True))
    alpha = jnp.exp2(m - m_new)
    p = jnp.exp2(s - m_new)
    l = alpha * l + jnp.sum(p, axis=keys_axis, keepdims=True)
    acc = alpha * acc + _pv(values, p.astype(values.dtype), keys_axis)
    return m_new, l, acc


def _softmax_chunks(scores_fn, values_fn, last_chunk, state, s_ref, p_ref, keys_axis):
    m_ref, l_ref, acc_ref, a_ref, cmax_ref = state

    def flush(g, slot):
        pv = _pv(values_fn(jnp.maximum(g, 0)), p_ref[slot], keys_axis)
        acc_ref[slot] = a_ref[slot] * acc_ref[slot] + pv

    def produce(g, slot):
        s = scores_fn(jnp.minimum(g, last_chunk))
        s_ref[slot] = s
        cmax_ref[slot] = jnp.max(s, axis=keys_axis, keepdims=True)

    def step(g, cur, produce_next=True):
        m = m_ref[cur]
        m_new = jnp.maximum(m, cmax_ref[cur])
        alpha = jnp.exp2(m - m_new)
        p = jnp.exp2(s_ref[cur] - m_new)
        l_ref[cur] = alpha * l_ref[cur] + jnp.sum(p, axis=keys_axis, keepdims=True)
        p_ref[cur] = p.astype(p_ref.dtype)
        m_ref[cur] = m_new
        a_ref[cur] = alpha
        if produce_next:
            produce(g + 1, 1 - cur)
        flush(g - 1, 1 - cur)

    def begin():
        produce(0, 0)
        p_ref[1] = jnp.zeros(p_ref.shape[1:], p_ref.dtype)
        a_ref[0] = jnp.ones(a_ref.shape[1:], a_ref.dtype)
        a_ref[1] = jnp.ones(a_ref.shape[1:], a_ref.dtype)
        m_ref[1] = jnp.full(m_ref.shape[1:], -jnp.inf, F32)
        l_ref[1] = jnp.zeros(l_ref.shape[1:], F32)
        acc_ref[1] = jnp.zeros(acc_ref.shape[1:], F32)

    def run(n):
        def pair(k, c):
            step(2 * k, 0)
            step(2 * k + 1, 1)
            return c

        lax.fori_loop(0, n // 2, pair, 0)

        @pl.when(n % 2 == 1)
        def _():
            step(n - 1, 0, produce_next=False)
            flush(n - 1, 0)

        @pl.when(n % 2 == 0)
        def _():
            flush(n - 1, 1)

        m = jnp.maximum(m_ref[0], m_ref[1])
        w0, w1 = jnp.exp2(m_ref[0] - m), jnp.exp2(m_ref[1] - m)
        return m, w0 * l_ref[0] + w1 * l_ref[1], w0 * acc_ref[0] + w1 * acc_ref[1]

    return begin, run


class _Slots:
    def __init__(self, ref0, ref1):
        self.refs = (ref0, ref1)
        self.shape = (2,) + tuple(ref0.shape)
        self.dtype = ref0.dtype

    def __getitem__(self, slot):
        return self.refs[slot][...]

    def __setitem__(self, slot, value):
        self.refs[slot][...] = value


def _softmax_scratch(dv, tq, span, keys_axis):
    stat, acc, tile = ((1, tq), (dv, tq), (span, tq)) if keys_axis == 0 else ((tq, 1), (tq, dv), (tq, span))
    kinds = ((stat, F32), (stat, F32), (acc, F32), (stat, F32), (stat, F32), (tile, F32), (tile, BF16))
    return [pltpu.VMEM(shape, dtype) for shape, dtype in kinds for _ in range(2)]


def _softmax_slots(refs):
    assert len(refs) == 14
    return [_Slots(refs[2 * k], refs[2 * k + 1]) for k in range(7)]


def _mla_body(q_ref, k_ref, v_ref, o_ref, *, t, per_trip):
    i = pl.program_id(1)
    q = q_ref[0]

    def scores(n):
        return lax.dot_general(q, k_ref[0, pl.ds(pl.multiple_of(n * t, t), t), :], _NT, preferred_element_type=F32)

    def values(n):
        return v_ref[0, pl.ds(pl.multiple_of(n * t, t), t), :]

    def body(n, c):
        return _softmax_step(scores(n), values(n), *c, keys_axis=1)

    state = (jnp.full((t, 1), -jnp.inf, F32), jnp.zeros((t, 1), F32), jnp.zeros((t, MLA_V), F32))
    start, width = 0, per_trip
    while width >= 1:
        def body_group(k, c, start=start, width=width):
            for u in range(width):
                c = body(start + width * k + u, c)
            return c

        trips = (i - start) // width
        state = lax.fori_loop(0, trips, body_group, state)
        start, width = start + trips * width, width // 2
    m, l, acc = state
    row = lax.broadcasted_iota(jnp.int32, (t, t), 0)
    col = lax.broadcasted_iota(jnp.int32, (t, t), 1)
    m, l, acc = _softmax_step(jnp.where(col <= row, scores(i), -jnp.inf), values(i), m, l, acc, keys_axis=1)
    o_ref[...] = (acc / l).astype(o_ref.dtype)


def _mla(q, k, v, *, t=1024, per_trip=4):
    h, s, _ = q.shape
    assert s % t == 0
    return pl.pallas_call(
        functools.partial(_mla_body, t=t, per_trip=per_trip),
        grid=(h, s // t),
        in_specs=[
            pl.BlockSpec((1, t, MLA_QK_PAD), lambda h, i: (h, i, 0)),
            pl.BlockSpec((1, s, MLA_QK_PAD), lambda h, i: (h, 0, 0)),
            pl.BlockSpec((1, s, MLA_V), lambda h, i: (h, 0, 0)),
        ],
        out_specs=pl.BlockSpec((t, MLA_V), lambda h, i: (i, h)),
        out_shape=jax.ShapeDtypeStruct((s, h * MLA_V), BF16),
        compiler_params=_params(("arbitrary", "arbitrary")),
        name="mla",
    )(q, k, v)


def _t5_bucket(n):
    max_exact = REL_BUCKETS // 2
    n_f = jnp.maximum(n, max_exact).astype(F32)
    large = max_exact + (jnp.log(n_f / max_exact) / math.log(REL_MAX_DIST / max_exact)
                         * (REL_BUCKETS - max_exact)).astype(jnp.int32)
    large = jnp.minimum(large, REL_BUCKETS - 1)
    return jnp.where(n < max_exact, n, large)


def _moba_body(pmin_ref, pmax_ref, first_ref, q_ref, k_ref, v_ref, pq_ref, pk_ref, pmaxl_ref, tbl_ref, o_ref,
               kmean_ref, kaug_ref, vt_ref, *softmax_refs, nblk, scale, group):
    i = pl.program_id(1)
    blk = MOBA_BLOCK
    lane_id = lax.broadcasted_iota(jnp.int32, (blk, V7X_LANES), 1)
    m_ref, l_ref, acc_ref, a_ref, cmax_ref, s_ref, p_ref = _softmax_slots(softmax_refs)

    @pl.when(i == 0)
    def _():
        kmean_ref[...] = jnp.zeros_like(kmean_ref)

        def fill(n, c):
            rows = pl.ds(pl.multiple_of(n * blk, blk), blk)
            kb = k_ref[rows, :]
            kmean_ref[pl.ds(n, 1), :] = jnp.sum(kb.astype(F32), axis=0, keepdims=True) * (1.0 / blk)
            kaug_ref[rows, :MOBA_DH] = kb
            kaug_ref[rows, MOBA_DH:] = jnp.where(lane_id == n, 1.0, 0.0).astype(kaug_ref.dtype)
            vt_ref[:, rows] = v_ref[rows, :].astype(F32).T.astype(vt_ref.dtype)
            return c

        lax.fori_loop(0, nblk, fill, 0)

    q = q_ref[...]
    qs = (q.astype(F32) * scale).astype(q.dtype)

    gate = lax.dot_general(kmean_ref[...], q.astype(F32), _NT, preferred_element_type=F32)
    bid = lax.broadcasted_iota(jnp.int32, gate.shape, 0)
    bid_f = bid.astype(F32)
    g = jnp.where(bid < i, gate, -jnp.inf)
    picked_t = jnp.zeros(gate.shape, F32)
    for _ in range(min(MOBA_TOPK, nblk)):
        mx = jnp.max(g, axis=0, keepdims=True)
        idx = jnp.min(jnp.where(g == mx, bid_f, float(V7X_LANES)), axis=0, keepdims=True)
        hit = bid_f == jnp.where(mx > -jnp.inf, idx, -1.0)
        picked_t = jnp.where(hit, 1.0, picked_t)
        g = jnp.where(hit, -jnp.inf, g)
    picked = picked_t.T

    prev = jnp.maximum(i - 1, 0)
    far_lanes = jnp.logical_and((pmin_ref[i] - pmaxl_ref[...]) >= BIAS_TABLE - 1, lane_id[:1] != prev)
    m_any = jnp.where(picked > 0.0, 0.0, MASKED)
    q_any = jnp.concatenate([qs, m_any.astype(qs.dtype)], axis=1)
    q_far = jnp.concatenate([qs, jnp.where(far_lanes, m_any, MASKED).astype(qs.dtype)], axis=1)
    q_diag = jnp.concatenate([qs, jnp.where(lane_id == i, 0.0, m_any).astype(qs.dtype)], axis=1)

    def is_far(n):
        return (pmin_ref[i] - pmax_ref[n]) >= BIAS_TABLE - 1

    def masked_scores(off, width, q_aug):
        return lax.dot_general(kaug_ref[pl.ds(off, width), :], q_aug, _NT, preferred_element_type=F32)

    def near_bias(off):
        off = pl.multiple_of(off, blk)
        d = jnp.clip(pq_ref[...] - pk_ref[:, pl.ds(off, blk)], 0, BIAS_TABLE - 1)
        tb = jnp.broadcast_to(tbl_ref[0], (blk, BIAS_TABLE))
        parts = [jnp.take_along_axis(tb, d[:, c * V7X_LANES:(c + 1) * V7X_LANES], axis=1)
                 for c in range(blk // V7X_LANES)]
        return jnp.concatenate(parts, axis=1).T

    span = group * blk

    def far_scores(g):
        return masked_scores(pl.multiple_of(g * span, span), span, q_far)

    def far_values_t(g):
        return vt_ref[:, pl.ds(pl.multiple_of(g * span, span), span)]

    far_begin, far_run = _softmax_chunks(far_scores, far_values_t, nblk // group - 1,
                                         (m_ref, l_ref, acc_ref, a_ref, cmax_ref), s_ref, p_ref, keys_axis=0)
    off_d = pl.multiple_of(prev * blk, blk)
    key = prev * blk + lax.broadcasted_iota(jnp.int32, (2 * blk, blk), 0)
    qry = i * blk + lax.broadcasted_iota(jnp.int32, (2 * blk, blk), 1)
    s = masked_scores(off_d, 2 * blk, q_diag)
    far_begin()
    s = s + jnp.concatenate([near_bias(off_d), near_bias(off_d + blk)], axis=0)
    s = jnp.where(key <= qry, s, -jnp.inf)
    m = jnp.max(s, axis=0, keepdims=True)
    p = jnp.exp2(s - m)
    m_ref[0] = m
    l_ref[0] = jnp.sum(p, axis=0, keepdims=True)
    acc_ref[0] = jnp.dot(vt_ref[:, pl.ds(off_d, 2 * blk)], p.astype(vt_ref.dtype), preferred_element_type=F32)
    m_ref[0], l_ref[0], acc_ref[0] = far_run((i + group - 1) // group)

    def near_block(n, c):
        @pl.when(jnp.logical_not(is_far(n)))
        def _():
            off = pl.multiple_of(n * blk, blk)
            s = masked_scores(off, blk, q_any) + near_bias(off)
            m_ref[0], l_ref[0], acc_ref[0] = _softmax_step(
                s, vt_ref[:, pl.ds(off, blk)], m_ref[0], l_ref[0], acc_ref[0], keys_axis=0)

        return c

    lax.fori_loop(first_ref[i], prev, near_block, 0)
    o_ref[...] = (acc_ref[0] / l_ref[0]).T.astype(o_ref.dtype)


def _moba(proj, positions, rel_bias, *, group=4):
    s = proj.shape[0]
    blk = MOBA_BLOCK
    nblk = s // blk
    assert s % blk == 0 and 2 <= nblk <= V7X_LANES and nblk % group == 0
    pos_blocks = positions.reshape(nblk, blk)
    pmin = jnp.min(pos_blocks, axis=1)
    pmax = jnp.max(pos_blocks, axis=1)
    table = rel_bias[_t5_bucket(jnp.arange(BIAS_TABLE, dtype=jnp.int32))]
    table = jnp.transpose(table).reshape(MOBA_HEADS, 1, BIAS_TABLE).astype(F32)
    table = (table - table[:, :, BIAS_TABLE - 1:]) * LOG2E
    pmax_lanes = jnp.pad(pmax, (0, V7X_LANES - nblk)).reshape(1, V7X_LANES)
    blk_id = jnp.arange(nblk, dtype=jnp.int32)
    pair_start = jnp.maximum(blk_id - 1, 0)
    near = jnp.logical_and(pmin[:, None] - pmax[None, :] < BIAS_TABLE - 1, blk_id[None, :] < pair_start[:, None])
    first_near = jnp.min(jnp.where(near, blk_id[None, :], pair_start[:, None]), axis=1)
    qc, kc, vc = (c // MOBA_DH for c in (COL_MOBA_Q, COL_MOBA_K, COL_MOBA_V))
    grid_spec = pltpu.PrefetchScalarGridSpec(
        num_scalar_prefetch=3,
        grid=(MOBA_HEADS, nblk),
        in_specs=[
            pl.BlockSpec((blk, MOBA_DH), lambda h, i, *_: (i, qc + h)),
            pl.BlockSpec((s, MOBA_DH), lambda h, i, *_: (0, kc + h)),
            pl.BlockSpec((s, MOBA_DH), lambda h, i, *_: (0, vc + h)),
            pl.BlockSpec((blk, 1), lambda h, i, *_: (i, 0)),
            pl.BlockSpec((1, s), lambda h, i, *_: (0, 0)),
            pl.BlockSpec((1, V7X_LANES), lambda h, i, *_: (0, 0)),
            pl.BlockSpec((1, 1, BIAS_TABLE), lambda h, i, *_: (h, 0, 0)),
        ],
        out_specs=pl.BlockSpec((blk, MOBA_DH), lambda h, i, *_: (i, h)),
        scratch_shapes=[
            pltpu.VMEM((V7X_LANES, MOBA_DH), F32),
            pltpu.VMEM((s, 2 * MOBA_DH), BF16),
            pltpu.VMEM((MOBA_DH, s), BF16),
        ] + _softmax_scratch(MOBA_DH, blk, group * blk, keys_axis=0),
    )
    return pl.pallas_call(
        functools.partial(_moba_body, nblk=nblk, scale=MOBA_DH ** -0.5 * LOG2E, group=group),
        grid_spec=grid_spec,
        out_shape=jax.ShapeDtypeStruct((s, MOBA_HEADS * MOBA_DH), BF16),
        compiler_params=_params(("arbitrary", "arbitrary")),
        name="moba",
    )(pmin, pmax, first_near, proj, proj, proj, positions.reshape(s, 1), positions.reshape(1, s), pmax_lanes, table)


def _merge_body(oa_ref, ob_ref, ga_ref, gb_ref, wa_ref, wb_ref, wo_ref, x_ref, o_ref):
    @pl.when(pl.program_id(1) == 0)
    def _():
        o_ref[...] = x_ref[...]

    ma = jnp.dot(oa_ref[...], wa_ref[...], preferred_element_type=F32)
    mb = jnp.dot(ob_ref[...], wb_ref[...], preferred_element_type=F32)
    mg = _sigmoid(ga_ref[...].astype(F32)) * ma + _sigmoid(gb_ref[...].astype(F32)) * mb
    o_ref[...] += jnp.dot(mg.astype(BF16), wo_ref[...], preferred_element_type=F32)


def _merge(o_a, o_b, proj, w_a, w_b, w_out, x, *, tm=512, tn=512):
    s, d = x.shape
    ka, kb = o_a.shape[1], o_b.shape[1]
    ga, gb = COL_GATE_A // tn, COL_GATE_B // tn
    return pl.pallas_call(
        _merge_body,
        grid=(s // tm, d // tn),
        in_specs=[
            pl.BlockSpec((tm, ka), lambda i, j: (i, 0)),
            pl.BlockSpec((tm, kb), lambda i, j: (i, 0)),
            pl.BlockSpec((tm, tn), lambda i, j: (i, ga + j)),
            pl.BlockSpec((tm, tn), lambda i, j: (i, gb + j)),
            pl.BlockSpec((ka, tn), lambda i, j: (0, j)),
            pl.BlockSpec((kb, tn), lambda i, j: (0, j)),
            pl.BlockSpec((tn, d), lambda i, j: (j, 0)),
            _row_resident_spec(tm, d),
        ],
        out_specs=pl.BlockSpec((tm, d), lambda i, j: (i, 0)),
        out_shape=jax.ShapeDtypeStruct((s, d), F32),
        compiler_params=_params(("arbitrary", "arbitrary")),
        name="merge",
    )(o_a, o_b, proj, proj, w_a, w_b, w_out, x)


def _proj_layout(w_in):
    w_lat = jnp.pad(w_in[:, :LATENT_REAL].astype(BF16), ((0, 0), (0, LATENT_COLS - LATENT_REAL)))
    return w_lat, w_in[:, LATENT_REAL:].astype(BF16)


def _q_layout(w_uq):
    r = w_uq.shape[0]
    w = w_uq.reshape(r, MLA_HEADS, MLA_NOPE + MLA_ROPE)
    w = jnp.pad(w, ((0, 0), (0, 0), (0, MLA_QK_PAD - MLA_NOPE - MLA_ROPE)))
    return w.reshape(r, MLA_HEADS * MLA_QK_PAD).astype(BF16)


def kernel(x, positions, rel_bias, norm_ffn1, w_ffn1_in, w_ffn1_out, norm_mix, w_in, norm_cq, w_uq, norm_ckv, w_ukv, w_br_a, w_br_b, w_out, norm_ffn2, w_ffn2_in, w_ffn2_out, norm_final):
    b, s, d = x.shape
    assert b == 1 and norm_ffn1.shape[0] == 1
    pos = positions[0]
    x0 = x[0]

    x1 = _ffn(x0, norm_ffn1[0], w_ffn1_in[0].astype(BF16), w_ffn1_out[0])
    w_lat, w_mix = _proj_layout(w_in[0])
    latent, mixer = _proj(x1, norm_mix[0], w_lat, w_mix)
    tables = _rope_tables(pos)
    q = _q_up(latent, norm_cq[0], _q_layout(w_uq[0]), tables)
    k, v = _kv_up(latent, norm_ckv[0], w_ukv[0].astype(BF16), tables)
    o_a = _mla(q, k, v)
    o_b = _moba(mixer, pos, rel_bias)
    x2 = _merge(o_a, o_b, mixer, w_br_a[0].astype(BF16), w_br_b[0].astype(BF16), w_out[0].astype(BF16), x1)
    out = _ffn(x2, norm_ffn2[0], w_ffn2_in[0].astype(BF16), w_ffn2_out[0], norm_final)
    return out[None]
```

```python
import functools
import math

import jax
import jax.numpy as jnp
from jax import lax
from jax.experimental import pallas as pl
from jax.experimental.pallas import tpu as pltpu

F32 = jnp.float32
BF16 = jnp.bfloat16

RMS_EPS = 1e-6
ROPE_THETA = 10000.0
LOG2E = math.log2(math.e)
MASKED = -1e30

MLA_HEADS = 16
MLA_Q_LORA = 1024
MLA_KV_LORA = 512
MLA_NOPE = 128
MLA_ROPE = 64
MLA_V = 128
MLA_QK_PAD = 256

MOBA_HEADS = 16
MOBA_DH = 128
MOBA_BLOCK = 256
MOBA_TOPK = 3
REL_BUCKETS = 32
REL_MAX_DIST = 128
BIAS_TABLE = 128

V7X_LANES = 128
V7X_VMEM_LIMIT = 56 * 1024 * 1024

LATENT_REAL = MLA_Q_LORA + MLA_KV_LORA + MLA_ROPE
LATENT_COLS = 2048
COL_CQ = 0
COL_CKV = 1024
COL_KROPE = 1536
COL_MOBA_Q = 0
COL_MOBA_K = 2048
COL_MOBA_V = 4096
COL_GATE_A = 6144
COL_GATE_B = 10240

_NT = (((1,), (1,)), ((), ()))


def _params(sem):
    return pltpu.CompilerParams(dimension_semantics=sem, vmem_limit_bytes=V7X_VMEM_LIMIT)


def _row_resident_spec(tm, d):
    return pl.BlockSpec((tm, d), lambda i, j: (i, 0), pipeline_mode=pl.Buffered(1))


def _rms(xf, gain):
    ms = jnp.mean(xf * xf, axis=-1, keepdims=True)
    return xf * lax.rsqrt(ms + RMS_EPS) * gain


def _sigmoid(z):
    return 1.0 / (1.0 + jnp.exp(-z))


def _ffn_body(x_ref, g_ref, wg_ref, wu_ref, wo_ref, *rest, final_norm):
    if final_norm:
        gf_ref, o_ref, h_ref = rest
    else:
        o_ref, h_ref = rest
    j = pl.program_id(1)

    @pl.when(j == 0)
    def _():
        xf = x_ref[...]
        h_ref[...] = _rms(xf, g_ref[...]).astype(BF16)
        o_ref[...] = xf

    h = h_ref[...]
    g = jnp.dot(h, wg_ref[...], preferred_element_type=F32)
    u = jnp.dot(h, wu_ref[...], preferred_element_type=F32)
    a = (g * _sigmoid(g) * (0.5 * u)).astype(BF16)
    o_ref[...] += jnp.dot(a, wo_ref[...].astype(BF16), preferred_element_type=F32)

    if final_norm:
        @pl.when(j == pl.num_programs(1) - 1)
        def _():
            o_ref[...] = _rms(o_ref[...], gf_ref[...])


def _ffn(x, gain, w_in, w_out, final_gain=None, *, tm=512, tf=256):
    s, d = x.shape
    f = w_out.shape[0]
    nf = f // tf
    assert s % tm == 0 and f % tf == 0 and w_in.shape == (d, 2 * f)
    in_specs = [
        _row_resident_spec(tm, d),
        pl.BlockSpec((1, d), lambda i, j: (0, 0)),
        pl.BlockSpec((d, tf), lambda i, j: (0, j)),
        pl.BlockSpec((d, tf), lambda i, j: (0, j + nf)),
        pl.BlockSpec((tf, d), lambda i, j: (j, 0)),
    ]
    args = [x, gain.reshape(1, d), w_in, w_in, w_out]
    if final_gain is not None:
        in_specs.append(pl.BlockSpec((1, d), lambda i, j: (0, 0)))
        args.append(final_gain.reshape(1, d))
    return pl.pallas_call(
        functools.partial(_ffn_body, final_norm=final_gain is not None),
        grid=(s // tm, nf),
        in_specs=in_specs,
        out_specs=pl.BlockSpec((tm, d), lambda i, j: (i, 0)),
        out_shape=jax.ShapeDtypeStruct((s, d), F32),
        scratch_shapes=[pltpu.VMEM((tm, d), BF16)],
        compiler_params=_params(("arbitrary", "arbitrary")),
        name="ffn",
    )(*args)


def _proj_body(x_ref, g_ref, wa_ref, wb_ref, oa_ref, ob_ref, h_ref, *, na):
    j = pl.program_id(1)

    @pl.when(j == 0)
    def _():
        h_ref[...] = _rms(x_ref[...], g_ref[...]).astype(BF16)

    @pl.when(j < na)
    def _():
        oa_ref[...] = jnp.dot(h_ref[...], wa_ref[...], preferred_element_type=F32).astype(oa_ref.dtype)

    @pl.when(j >= na)
    def _():
        ob_ref[...] = jnp.dot(h_ref[...], wb_ref[...], preferred_element_type=F32).astype(ob_ref.dtype)


def _proj(x, gain, w_a, w_b, *, tm=512, tn=1024):
    s, d = x.shape
    na, nb = w_a.shape[1] // tn, w_b.shape[1] // tn
    assert s % tm == 0 and w_a.shape[1] % tn == 0 and w_b.shape[1] % tn == 0
    col_a = lambda j: jnp.minimum(j, na - 1)
    col_b = lambda j: jnp.maximum(j - na, 0)
    return pl.pallas_call(
        functools.partial(_proj_body, na=na),
        grid=(s // tm, na + nb),
        in_specs=[
            _row_resident_spec(tm, d),
            pl.BlockSpec((1, d), lambda i, j: (0, 0)),
            pl.BlockSpec((d, tn), lambda i, j: (0, col_a(j))),
            pl.BlockSpec((d, tn), lambda i, j: (0, col_b(j))),
        ],
        out_specs=[
            pl.BlockSpec((tm, tn), lambda i, j: (i, col_a(j))),
            pl.BlockSpec((tm, tn), lambda i, j: (i, col_b(j))),
        ],
        out_shape=[jax.ShapeDtypeStruct((s, w_a.shape[1]), BF16), jax.ShapeDtypeStruct((s, w_b.shape[1]), BF16)],
        scratch_shapes=[pltpu.VMEM((tm, d), BF16)],
        compiler_params=_params(("arbitrary", "arbitrary")),
        name="proj",
    )(x, gain.reshape(1, d), w_a, w_b)


def _rope_lanes(r, cf, s1, s2):
    return r * cf + pltpu.roll(r, 96, 1) * s1 + pltpu.roll(r, 32, 1) * s2


def _rope_tables(positions):
    half = MLA_ROPE // 2
    inv_freq = ROPE_THETA ** (-jnp.arange(0, MLA_ROPE, 2, dtype=F32) / MLA_ROPE)
    ang = positions[:, None].astype(F32) * inv_freq
    cos, sin = jnp.cos(ang), jnp.sin(ang)
    z = jnp.zeros_like(cos)
    pad = jnp.zeros((positions.shape[0], V7X_LANES - MLA_ROPE), F32)
    cf = jnp.concatenate([cos, cos, pad], axis=1)
    s1 = jnp.concatenate([-sin, z, pad], axis=1)
    s2 = jnp.concatenate([z, sin, pad], axis=1)
    return cf, s1, s2


def _qup_body(c_ref, g_ref, w_ref, cf_ref, s1_ref, s2_ref, q_ref, h_ref, *, scale):
    @pl.when(pl.program_id(1) == 0)
    def _():
        h_ref[...] = _rms(c_ref[...].astype(F32), g_ref[...]).astype(BF16)

    y = jnp.dot(h_ref[...], w_ref[...], preferred_element_type=F32)
    for hh in range(q_ref.shape[0]):
        c0 = hh * MLA_QK_PAD
        q_ref[hh, :, :MLA_NOPE] = (y[:, c0:c0 + MLA_NOPE] * scale).astype(q_ref.dtype)
        r = _rope_lanes(y[:, c0 + MLA_NOPE:c0 + MLA_QK_PAD], cf_ref[...], s1_ref[...], s2_ref[...])
        q_ref[hh, :, MLA_NOPE:] = (r * scale).astype(q_ref.dtype)


def _q_up(proj, gain, w_q, tables, *, tm=1024, hg=8):
    s = proj.shape[0]
    cf, s1, s2 = tables
    scale = (MLA_NOPE + MLA_ROPE) ** -0.5 * LOG2E
    tab_spec = pl.BlockSpec((tm, V7X_LANES), lambda i, h: (i, 0))
    return pl.pallas_call(
        functools.partial(_qup_body, scale=scale),
        grid=(s // tm, MLA_HEADS // hg),
        in_specs=[
            pl.BlockSpec((tm, MLA_Q_LORA), lambda i, h: (i, COL_CQ // MLA_Q_LORA)),
            pl.BlockSpec((1, MLA_Q_LORA), lambda i, h: (0, 0)),
            pl.BlockSpec((MLA_Q_LORA, hg * MLA_QK_PAD), lambda i, h: (0, h)),
            tab_spec, tab_spec, tab_spec,
        ],
        out_specs=pl.BlockSpec((hg, tm, MLA_QK_PAD), lambda i, h: (h, i, 0)),
        out_shape=jax.ShapeDtypeStruct((MLA_HEADS, s, MLA_QK_PAD), BF16),
        scratch_shapes=[pltpu.VMEM((tm, MLA_Q_LORA), BF16)],
        compiler_params=_params(("arbitrary", "arbitrary")),
        name="q_up",
    )(proj, gain.reshape(1, MLA_Q_LORA), w_q, cf, s1, s2)


def _kvup_body(c_ref, kr_ref, g_ref, w_ref, cf_ref, s1_ref, s2_ref, k_ref, v_ref, h_ref, r_ref):
    @pl.when(pl.program_id(1) == 0)
    def _():
        h_ref[...] = _rms(c_ref[...].astype(F32), g_ref[...]).astype(BF16)
        r_ref[...] = _rope_lanes(kr_ref[...].astype(F32), cf_ref[...], s1_ref[...], s2_ref[...]).astype(BF16)

    y = jnp.dot(h_ref[...], w_ref[...], preferred_element_type=F32)
    for hh in range(k_ref.shape[0]):
        c0 = hh * (MLA_NOPE + MLA_V)
        k_ref[hh, :, :MLA_NOPE] = y[:, c0:c0 + MLA_NOPE].astype(k_ref.dtype)
        k_ref[hh, :, MLA_NOPE:] = r_ref[...]
        v_ref[hh] = y[:, c0 + MLA_NOPE:c0 + MLA_NOPE + MLA_V].astype(v_ref.dtype)


def _kv_up(proj, gain, w_ukv, tables, *, tm=1024, hg=8):
    s = proj.shape[0]
    cf, s1, s2 = tables
    tab_spec = pl.BlockSpec((tm, V7X_LANES), lambda i, h: (i, 0))
    return pl.pallas_call(
        _kvup_body,
        grid=(s // tm, MLA_HEADS // hg),
        in_specs=[
            pl.BlockSpec((tm, MLA_KV_LORA), lambda i, h: (i, COL_CKV // MLA_KV_LORA)),
            pl.BlockSpec((tm, V7X_LANES), lambda i, h: (i, COL_KROPE // V7X_LANES)),
            pl.BlockSpec((1, MLA_KV_LORA), lambda i, h: (0, 0)),
            pl.BlockSpec((MLA_KV_LORA, hg * (MLA_NOPE + MLA_V)), lambda i, h: (0, h)),
            tab_spec, tab_spec, tab_spec,
        ],
        out_specs=[
            pl.BlockSpec((hg, tm, MLA_QK_PAD), lambda i, h: (h, i, 0)),
            pl.BlockSpec((hg, tm, MLA_V), lambda i, h: (h, i, 0)),
        ],
        out_shape=[
            jax.ShapeDtypeStruct((MLA_HEADS, s, MLA_QK_PAD), BF16),
            jax.ShapeDtypeStruct((MLA_HEADS, s, MLA_V), BF16),
        ],
        scratch_shapes=[pltpu.VMEM((tm, MLA_KV_LORA), BF16), pltpu.VMEM((tm, V7X_LANES), BF16)],
        compiler_params=_params(("arbitrary", "arbitrary")),
        name="kv_up",
    )(proj, proj, gain.reshape(1, MLA_KV_LORA), w_ukv, cf, s1, s2)


def _pv(values, p, keys_axis):
    if keys_axis == 0:
        return jnp.dot(values, p, preferred_element_type=F32)
    return jnp.dot(p, values, preferred_element_type=F32)


def _softmax_step(s, values, m, l, acc, keys_axis):
    m_new = jnp.maximum(m, jnp.max(s, axis=keys_axis, keepdims=True))
    alpha = jnp.exp2(m - m_new)
    p = jnp.exp2(s - m_new)
    l = alpha * l + jnp.sum(p, axis=keys_axis, keepdims=True)
    acc = alpha * acc + _pv(values, p.astype(values.dtype), keys_axis)
    return m_new, l, acc


def _softmax_chunks(scores_fn, values_fn, last_chunk, state, s_ref, p_ref, keys_axis):
    m_ref, l_ref, acc_ref, a_ref, cmax_ref = state

    def flush(g, slot):
        pv = _pv(values_fn(jnp.maximum(g, 0)), p_ref[slot], keys_axis)
        acc_ref[slot] = a_ref[slot] * acc_ref[slot] + pv

    def produce(g, slot):
        s = scores_fn(jnp.minimum(g, last_chunk))
        s_ref[slot] = s
        cmax_ref[slot] = jnp.max(s, axis=keys_axis, keepdims=True)

    def step(g, cur, produce_next=True):
        m = m_ref[cur]
        m_new = jnp.maximum(m, cmax_ref[cur])
        alpha = jnp.exp2(m - m_new)
        p = jnp.exp2(s_ref[cur] - m_new)
        l_ref[cur] = alpha * l_ref[cur] + jnp.sum(p, axis=keys_axis, keepdims=True)
        p_ref[cur] = p.astype(p_ref.dtype)
        m_ref[cur] = m_new
        a_ref[cur] = alpha
        if produce_next:
            produce(g + 1, 1 - cur)
        flush(g - 1, 1 - cur)

    def begin():
        produce(0, 0)
        p_ref[1] = jnp.zeros(p_ref.shape[1:], p_ref.dtype)
        a_ref[0] = jnp.ones(a_ref.shape[1:], a_ref.dtype)
        a_ref[1] = jnp.ones(a_ref.shape[1:], a_ref.dtype)
        m_ref[1] = jnp.full(m_ref.shape[1:], -jnp.inf, F32)
        l_ref[1] = jnp.zeros(l_ref.shape[1:], F32)
        acc_ref[1] = jnp.zeros(acc_ref.shape[1:], F32)

    def run(n):
        def pair(k, c):
            step(2 * k, 0)
            step(2 * k + 1, 1)
            return c

        lax.fori_loop(0, n // 2, pair, 0)

        @pl.when(n % 2 == 1)
        def _():
            step(n - 1, 0, produce_next=False)
            flush(n - 1, 0)

        @pl.when(n % 2 == 0)
        def _():
            flush(n - 1, 1)

        m = jnp.maximum(m_ref[0], m_ref[1])
        w0, w1 = jnp.exp2(m_ref[0] - m), jnp.exp2(m_ref[1] - m)
        return m, w0 * l_ref[0] + w1 * l_ref[1], w0 * acc_ref[0] + w1 * acc_ref[1]

    return begin, run


class _Slots:
    def __init__(self, ref0, ref1):
        self.refs = (ref0, ref1)
        self.shape = (2,) + tuple(ref0.shape)
        self.dtype = ref0.dtype

    def __getitem__(self, slot):
        return self.refs[slot][...]

    def __setitem__(self, slot, value):
        self.refs[slot][...] = value


def _softmax_scratch(dv, tq, span, keys_axis):
    stat, acc, tile = ((1, tq), (dv, tq), (span, tq)) if keys_axis == 0 else ((tq, 1), (tq, dv), (tq, span))
    kinds = ((stat, F32), (stat, F32), (acc, F32), (stat, F32), (stat, F32), (tile, F32), (tile, BF16))
    return [pltpu.VMEM(shape, dtype) for shape, dtype in kinds for _ in range(2)]


def _softmax_slots(refs):
    assert len(refs) == 14
    return [_Slots(refs[2 * k], refs[2 * k + 1]) for k in range(7)]


def _mla_body(q_ref, k_ref, v_ref, o_ref, *, t, per_trip):
    i = pl.program_id(1)
    q = q_ref[0]

    def scores(n):
        return lax.dot_general(q, k_ref[0, pl.ds(pl.multiple_of(n * t, t), t), :], _NT, preferred_element_type=F32)

    def values(n):
        return v_ref[0, pl.ds(pl.multiple_of(n * t, t), t), :]

    def body(n, c):
        return _softmax_step(scores(n), values(n), *c, keys_axis=1)

    state = (jnp.full((t, 1), -jnp.inf, F32), jnp.zeros((t, 1), F32), jnp.zeros((t, MLA_V), F32))
    start, width = 0, per_trip
    while width >= 1:
        def body_group(k, c, start=start, width=width):
            for u in range(width):
                c = body(start + width * k + u, c)
            return c

        trips = (i - start) // width
        state = lax.fori_loop(0, trips, body_group, state)
        start, width = start + trips * width, width // 2
    m, l, acc = state
    row = lax.broadcasted_iota(jnp.int32, (t, t), 0)
    col = lax.broadcasted_iota(jnp.int32, (t, t), 1)
    m, l, acc = _softmax_step(jnp.where(col <= row, scores(i), -jnp.inf), values(i), m, l, acc, keys_axis=1)
    o_ref[...] = (acc / l).astype(o_ref.dtype)


def _mla(q, k, v, *, t=1024, per_trip=4):
    h, s, _ = q.shape
    assert s % t == 0
    return pl.pallas_call(
        functools.partial(_mla_body, t=t, per_trip=per_trip),
        grid=(h, s // t),
        in_specs=[
            pl.BlockSpec((1, t, MLA_QK_PAD), lambda h, i: (h, i, 0)),
            pl.BlockSpec((1, s, MLA_QK_PAD), lambda h, i: (h, 0, 0)),
            pl.BlockSpec((1, s, MLA_V), lambda h, i: (h, 0, 0)),
        ],
        out_specs=pl.BlockSpec((t, MLA_V), lambda h, i: (i, h)),
        out_shape=jax.ShapeDtypeStruct((s, h * MLA_V), BF16),
        compiler_params=_params(("arbitrary", "arbitrary")),
        name="mla",
    )(q, k, v)


def _t5_bucket(n):
    max_exact = REL_BUCKETS // 2
    n_f = jnp.maximum(n, max_exact).astype(F32)
    large = max_exact + (jnp.log(n_f / max_exact) / math.log(REL_MAX_DIST / max_exact)
                         * (REL_BUCKETS - max_exact)).astype(jnp.int32)
    large = jnp.minimum(large, REL_BUCKETS - 1)
    return jnp.where(n < max_exact, n, large)


def _moba_body(pmin_ref, pmax_ref, first_ref, q_ref, k_ref, v_ref, pq_ref, pk_ref, pmaxl_ref, tbl_ref, o_ref,
               kmean_ref, kaug_ref, vt_ref, *softmax_refs, nblk, scale, group):
    i = pl.program_id(1)
    blk = MOBA_BLOCK
    lane_id = lax.broadcasted_iota(jnp.int32, (blk, V7X_LANES), 1)
    m_ref, l_ref, acc_ref, a_ref, cmax_ref, s_ref, p_ref = _softmax_slots(softmax_refs)

    @pl.when(i == 0)
    def _():
        kmean_ref[...] = jnp.zeros_like(kmean_ref)

        def fill(n, c):
            rows = pl.ds(pl.multiple_of(n * blk, blk), blk)
            kb = k_ref[rows, :]
            kmean_ref[pl.ds(n, 1), :] = jnp.sum(kb.astype(F32), axis=0, keepdims=True) * (1.0 / blk)
            kaug_ref[rows, :MOBA_DH] = kb
            kaug_ref[rows, MOBA_DH:] = jnp.where(lane_id == n, 1.0, 0.0).astype(kaug_ref.dtype)
            vt_ref[:, rows] = v_ref[rows, :].astype(F32).T.astype(vt_ref.dtype)
            return c

        lax.fori_loop(0, nblk, fill, 0)

    q = q_ref[...]
    qs = (q.astype(F32) * scale).astype(q.dtype)

    gate = lax.dot_general(kmean_ref[...], q.astype(F32), _NT, preferred_element_type=F32)
    bid = lax.broadcasted_iota(jnp.int32, gate.shape, 0)
    bid_f = bid.astype(F32)
    g = jnp.where(bid < i, gate, -jnp.inf)
    picked_t = jnp.zeros(gate.shape, F32)
    for _ in range(min(MOBA_TOPK, nblk)):
        mx = jnp.max(g, axis=0, keepdims=True)
        idx = jnp.min(jnp.where(g == mx, bid_f, float(V7X_LANES)), axis=0, keepdims=True)
        hit = bid_f == jnp.where(mx > -jnp.inf, idx, -1.0)
        picked_t = jnp.where(hit, 1.0, picked_t)
        g = jnp.where(hit, -jnp.inf, g)
    picked = picked_t.T

    prev = jnp.maximum(i - 1, 0)
    far_lanes = jnp.logical_and((pmin_ref[i] - pmaxl_ref[...]) >= BIAS_TABLE - 1, lane_id[:1] != prev)
    m_any = jnp.where(picked > 0.0, 0.0, MASKED)
    q_any = jnp.concatenate([qs, m_any.astype(qs.dtype)], axis=1)
    q_far = jnp.concatenate([qs, jnp.where(far_lanes, m_any, MASKED).astype(qs.dtype)], axis=1)
    q_diag = jnp.concatenate([qs, jnp.where(lane_id == i, 0.0, m_any).astype(qs.dtype)], axis=1)

    def is_far(n):
        return (pmin_ref[i] - pmax_ref[n]) >= BIAS_TABLE - 1

    def masked_scores(off, width, q_aug):
        return lax.dot_general(kaug_ref[pl.ds(off, width), :], q_aug, _NT, preferred_element_type=F32)

    def near_bias(off):
        off = pl.multiple_of(off, blk)
        d = jnp.clip(pq_ref[...] - pk_ref[:, pl.ds(off, blk)], 0, BIAS_TABLE - 1)
        tb = jnp.broadcast_to(tbl_ref[0], (blk, BIAS_TABLE))
        parts = [jnp.take_along_axis(tb, d[:, c * V7X_LANES:(c + 1) * V7X_LANES], axis=1)
                 for c in range(blk // V7X_LANES)]
        return jnp.concatenate(parts, axis=1).T

    span = group * blk

    def far_scores(g):
        return masked_scores(pl.multiple_of(g * span, span), span, q_far)

    def far_values_t(g):
        return vt_ref[:, pl.ds(pl.multiple_of(g * span, span), span)]

    far_begin, far_run = _softmax_chunks(far_scores, far_values_t, nblk // group - 1,
                                         (m_ref, l_ref, acc_ref, a_ref, cmax_ref), s_ref, p_ref, keys_axis=0)
    off_d = pl.multiple_of(prev * blk, blk)
    key = prev * blk + lax.broadcasted_iota(jnp.int32, (2 * blk, blk), 0)
    qry = i * blk + lax.broadcasted_iota(jnp.int32, (2 * blk, blk), 1)
    s = masked_scores(off_d, 2 * blk, q_diag)
    far_begin()
    s = s + jnp.concatenate([near_bias(off_d), near_bias(off_d + blk)], axis=0)
    s = jnp.where(key <= qry, s, -jnp.inf)
    m = jnp.max(s, axis=0, keepdims=True)
    p = jnp.exp2(s - m)
    m_ref[0] = m
    l_ref[0] = jnp.sum(p, axis=0, keepdims=True)
    acc_ref[0] = jnp.dot(vt_ref[:, pl.ds(off_d, 2 * blk)], p.astype(vt_ref.dtype), preferred_element_type=F32)
    m_ref[0], l_ref[0], acc_ref[0] = far_run((i + group - 1) // group)

    def near_block(n, c):
        @pl.when(jnp.logical_not(is_far(n)))
        def _():
            off = pl.multiple_of(n * blk, blk)
            s = masked_scores(off, blk, q_any) + near_bias(off)
            m_ref[0], l_ref[0], acc_ref[0] = _softmax_step(
                s, vt_ref[:, pl.ds(off, blk)], m_ref[0], l_ref[0], acc_ref[0], keys_axis=0)

        return c

    lax.fori_loop(first_ref[i], prev, near_block, 0)
    o_ref[...] = (acc_ref[0] / l_ref[0]).T.astype(o_ref.dtype)


def _moba(proj, positions, rel_bias, *, group=4):
    s = proj.shape[0]
    blk = MOBA_BLOCK
    nblk = s // blk
    assert s % blk == 0 and 2 <= nblk <= V7X_LANES and nblk % group == 0
    pos_blocks = positions.reshape(nblk, blk)
    pmin = jnp.min(pos_blocks, axis=1)
    pmax = jnp.max(pos_blocks, axis=1)
    table = rel_bias[_t5_bucket(jnp.arange(BIAS_TABLE, dtype=jnp.int32))]
    table = jnp.transpose(table).reshape(MOBA_HEADS, 1, BIAS_TABLE).astype(F32)
    table = (table - table[:, :, BIAS_TABLE - 1:]) * LOG2E
    pmax_lanes = jnp.pad(pmax, (0, V7X_LANES - nblk)).reshape(1, V7X_LANES)
    blk_id = jnp.arange(nblk, dtype=jnp.int32)
    pair_start = jnp.maximum(blk_id - 1, 0)
    near = jnp.logical_and(pmin[:, None] - pmax[None, :] < BIAS_TABLE - 1, blk_id[None, :] < pair_start[:, None])
    first_near = jnp.min(jnp.where(near, blk_id[None, :], pair_start[:, None]), axis=1)
    qc, kc, vc = (c // MOBA_DH for c in (COL_MOBA_Q, COL_MOBA_K, COL_MOBA_V))
    grid_spec = pltpu.PrefetchScalarGridSpec(
        num_scalar_prefetch=3,
        grid=(MOBA_HEADS, nblk),
        in_specs=[
            pl.BlockSpec((blk, MOBA_DH), lambda h, i, *_: (i, qc + h)),
            pl.BlockSpec((s, MOBA_DH), lambda h, i, *_: (0, kc + h)),
            pl.BlockSpec((s, MOBA_DH), lambda h, i, *_: (0, vc + h)),
            pl.BlockSpec((blk, 1), lambda h, i, *_: (i, 0)),
            pl.BlockSpec((1, s), lambda h, i, *_: (0, 0)),
            pl.BlockSpec((1, V7X_LANES), lambda h, i, *_: (0, 0)),
            pl.BlockSpec((1, 1, BIAS_TABLE), lambda h, i, *_: (h, 0, 0)),
        ],
        out_specs=pl.BlockSpec((blk, MOBA_DH), lambda h, i, *_: (i, h)),
        scratch_shapes=[
            pltpu.VMEM((V7X_LANES, MOBA_DH), F32),
            pltpu.VMEM((s, 2 * MOBA_DH), BF16),
            pltpu.VMEM((MOBA_DH, s), BF16),
        ] + _softmax_scratch(MOBA_DH, blk, group * blk, keys_axis=0),
    )
    return pl.pallas_call(
        functools.partial(_moba_body, nblk=nblk, scale=MOBA_DH ** -0.5 * LOG2E, group=group),
        grid_spec=grid_spec,
        out_shape=jax.ShapeDtypeStruct((s, MOBA_HEADS * MOBA_DH), BF16),
        compiler_params=_params(("arbitrary", "arbitrary")),
        name="moba",
    )(pmin, pmax, first_near, proj, proj, proj, positions.reshape(s, 1), positions.reshape(1, s), pmax_lanes, table)


def _merge_body(oa_ref, ob_ref, ga_ref, gb_ref, wa_ref, wb_ref, wo_ref, x_ref, o_ref):
    @pl.when(pl.program_id(1) == 0)
    def _():
        o_ref[...] = x_ref[...]

    ma = jnp.dot(oa_ref[...], wa_ref[...], preferred_element_type=F32)
    mb = jnp.dot(ob_ref[...], wb_ref[...], preferred_element_type=F32)
    mg = _sigmoid(ga_ref[...].astype(F32)) * ma + _sigmoid(gb_ref[...].astype(F32)) * mb
    o_ref[...] += jnp.dot(mg.astype(BF16), wo_ref[...], preferred_element_type=F32)


def _merge(o_a, o_b, proj, w_a, w_b, w_out, x, *, tm=512, tn=512):
    s, d = x.shape
    ka, kb = o_a.shape[1], o_b.shape[1]
    ga, gb = COL_GATE_A // tn, COL_GATE_B // tn
    return pl.pallas_call(
        _merge_body,
        grid=(s // tm, d // tn),
        in_specs=[
            pl.BlockSpec((tm, ka), lambda i, j: (i, 0)),
            pl.BlockSpec((tm, kb), lambda i, j: (i, 0)),
            pl.BlockSpec((tm, tn), lambda i, j: (i, ga + j)),
            pl.BlockSpec((tm, tn), lambda i, j: (i, gb + j)),
            pl.BlockSpec((ka, tn), lambda i, j: (0, j)),
            pl.BlockSpec((kb, tn), lambda i, j: (0, j)),
            pl.BlockSpec((tn, d), lambda i, j: (j, 0)),
            _row_resident_spec(tm, d),
        ],
        out_specs=pl.BlockSpec((tm, d), lambda i, j: (i, 0)),
        out_shape=jax.ShapeDtypeStruct((s, d), F32),
        compiler_params=_params(("arbitrary", "arbitrary")),
        name="merge",
    )(o_a, o_b, proj, proj, w_a, w_b, w_out, x)


def _proj_layout(w_in):
    w_lat = jnp.pad(w_in[:, :LATENT_REAL].astype(BF16), ((0, 0), (0, LATENT_COLS - LATENT_REAL)))
    return w_lat, w_in[:, LATENT_REAL:].astype(BF16)


def _q_layout(w_uq):
    r = w_uq.shape[0]
    w = w_uq.reshape(r, MLA_HEADS, MLA_NOPE + MLA_ROPE)
    w = jnp.pad(w, ((0, 0), (0, 0), (0, MLA_QK_PAD - MLA_NOPE - MLA_ROPE)))
    return w.reshape(r, MLA_HEADS * MLA_QK_PAD).astype(BF16)


def kernel(x, positions, rel_bias, norm_ffn1, w_ffn1_in, w_ffn1_out, norm_mix, w_in, norm_cq, w_uq, norm_ckv, w_ukv, w_br_a, w_br_b, w_out, norm_ffn2, w_ffn2_in, w_ffn2_out, norm_final):
    b, s, d = x.shape
    assert b == 1 and norm_ffn1.shape[0] == 1
    pos = positions[0]
    x0 = x[0]

    x1 = _ffn(x0, norm_ffn1[0], w_ffn1_in[0].astype(BF16), w_ffn1_out[0])
    w_lat, w_mix = _proj_layout(w_in[0])
    latent, mixer = _proj(x1, norm_mix[0], w_lat, w_mix)
    tables = _rope_tables(pos)
    q = _q_up(latent, norm_cq[0], _q_layout(w_uq[0]), tables)
    k, v = _kv_up(latent, norm_ckv[0], w_ukv[0].astype(BF16), tables)
    o_a = _mla(q, k, v)
    o_b = _moba(mixer, pos, rel_bias)
    x2 = _merge(o_a, o_b, mixer, w_br_a[0].astype(BF16), w_br_b[0].astype(BF16), w_out[0].astype(BF16), x1)
    out = _ffn(x2, norm_ffn2[0], w_ffn2_in[0].astype(BF16), w_ffn2_out[0], norm_final)
    return out[None]
```
